```python
import math
import jax, jax.numpy as jnp
from jax import lax


D_MODEL = 1024
BATCH = 8
SEQ = 2048
DEPTH = 2
DEC_BATCH = 128
DEC_SEQ = 4
PAST_LEN = 16384
PAGE_SIZE = 128

BRANCH_WIDTH = D_MODEL // 2
HEAD_DIM = 64
N_Q_HEADS = BRANCH_WIDTH // HEAD_DIM
N_KV_HEADS = max(N_Q_HEADS // 4, 1)
GROUP = N_Q_HEADS // N_KV_HEADS
WINDOW = 128
BLOCK = WINDOW
N_BUCKETS = 32
MAX_DISTANCE = 128
SSM_WIDTH = BRANCH_WIDTH
SSM_GROUP = 16
N_SSM_GROUPS = SSM_WIDTH // SSM_GROUP
SSM_STATE = 64
LRU_WIDTH = BRANCH_WIDTH
LRU_BLOCKS = 8
LRU_BLOCK_DIM = LRU_WIDTH // LRU_BLOCKS
CONV_WIDTH = 4
LRU_C = 8.0
DN_ALPHA = (2 * DEPTH) ** 0.25
DN_BETA = (8 * DEPTH) ** -0.25
LN_EPS = 1e-5
NEG = -1e30

SPLITS = (N_Q_HEADS * HEAD_DIM, N_KV_HEADS * HEAD_DIM, N_KV_HEADS * HEAD_DIM, BRANCH_WIDTH,
          SSM_WIDTH, SSM_WIDTH, LRU_WIDTH, LRU_WIDTH, 3 * D_MODEL)
IN_COLS = sum(SPLITS)

kernel_name = 'hybrid_gated_swa_s5_rglru_deepnorm_step'

f32 = jnp.float32


def _split_points():
    pts, acc = [], 0
    for s in SPLITS[:-1]:
        acc += s
        pts.append(acc)
    return pts


def _t5_bucket(dist):
    max_exact = N_BUCKETS // 2
    d = jnp.maximum(dist, 0)
    large = max_exact + (jnp.log(jnp.maximum(d, 1).astype(f32) / max_exact)
                         / math.log(MAX_DISTANCE / max_exact) * (N_BUCKETS - max_exact)).astype(jnp.int32)
    large = jnp.minimum(large, N_BUCKETS - 1)
    return jnp.where(d < max_exact, d, large)


def _band_attend(q, k, v, key_valid, rel_bias, sinks):
    T, S = q.shape[2], k.shape[2]
    dist = jnp.arange(T)[:, None] + (S - T) - jnp.arange(S)[None, :]
    in_band = (dist >= 0) & (dist < WINDOW)
    bias = rel_bias.astype(f32)[_t5_bucket(dist)]
    bias = bias.reshape(T, S, N_KV_HEADS, GROUP).transpose(2, 3, 0, 1)
    mask = in_band[None] & key_valid[:, None, :]
    scores = jnp.einsum('bntkgd,bnskd->bnkgts', q.astype(f32), k.astype(f32)) * (HEAD_DIM ** -0.5) + bias
    scores = jnp.where(mask[None, :, None, None], scores, NEG)
    sink = sinks.astype(f32).reshape(N_KV_HEADS, GROUP)[:, :, None, None]
    m = jnp.maximum(scores.max(-1, keepdims=True), sink)
    p = jnp.exp(scores - m)
    denom = p.sum(-1, keepdims=True) + jnp.exp(sink - m)
    return jnp.einsum('bnkgts,bnskd->bntkgd', p / denom, v.astype(f32))


def _attn_prompt(q, k, v, rel_bias, sinks):
    B, T, _ = q.shape
    nb = T // BLOCK
    qb = q.reshape(B, nb, BLOCK, N_KV_HEADS, GROUP, HEAD_DIM)
    kb = k.reshape(B, nb, BLOCK, N_KV_HEADS, HEAD_DIM)
    vb = v.reshape(B, nb, BLOCK, N_KV_HEADS, HEAD_DIM)
    pad = ((0, 0), (1, 0), (0, 0), (0, 0), (0, 0))
    kk = jnp.concatenate([jnp.pad(kb, pad)[:, :-1], kb], axis=2)
    vv = jnp.concatenate([jnp.pad(vb, pad)[:, :-1], vb], axis=2)
    prev_ok = jnp.broadcast_to((jnp.arange(nb) > 0)[:, None], (nb, BLOCK))
    key_valid = jnp.concatenate([prev_ok, jnp.ones((nb, BLOCK), bool)], axis=1)
    out = _band_attend(qb, kk, vv, key_valid, rel_bias, sinks)
    return out.reshape(B, T, N_Q_HEADS * HEAD_DIM)


def _attn_sample(q, k, v, k_cache, v_cache, rel_bias, sinks):
    B, T, _ = q.shape
    kk = jnp.concatenate([k_cache, k.reshape(B, T, N_KV_HEADS, HEAD_DIM)], axis=1)
    vv = jnp.concatenate([v_cache, v.reshape(B, T, N_KV_HEADS, HEAD_DIM)], axis=1)
    key_valid = jnp.ones((1, kk.shape[1]), bool)
    out = _band_attend(q.reshape(B, 1, T, N_KV_HEADS, GROUP, HEAD_DIM), kk[:, None], vv[:, None],
                       key_valid, rel_bias, sinks)
    return out.reshape(B, T, N_Q_HEADS * HEAD_DIM), kk[:, T:], vv[:, T:]


def _lin_scan(a, b):
    def comb(c1, c2):
        a1, b1 = c1
        a2, b2 = c2
        return a1 * a2, a2 * b1 + b2
    return lax.associative_scan(comb, (a, b), axis=1)[1]


def _clin_scan(a_re, a_im, b_re, b_im):
    def comb(c1, c2):
        ar1, ai1, br1, bi1 = c1
        ar2, ai2, br2, bi2 = c2
        return (ar2 * ar1 - ai2 * ai1, ar2 * ai1 + ai2 * ar1,
                ar2 * br1 - ai2 * bi1 + br2, ar2 * bi1 + ai2 * br1 + bi2)
    out = lax.associative_scan(comb, (a_re, a_im, b_re, b_im), axis=1)
    return out[2], out[3]


def _s5_branch(u, h0_re, h0_im, lam_re, lam_im, log_step, b_re, b_im, c_re, c_im, d, w_glu):
    B, T, _ = u.shape
    uf = u.astype(f32).reshape(B, T, N_SSM_GROUPS, SSM_GROUP)
    lr = jnp.minimum(lam_re.astype(f32), -1e-4)
    li = lam_im.astype(f32)
    step = jnp.exp(log_step.astype(f32))[:, None]
    mag = jnp.exp(lr * step)
    ab_re = mag * jnp.cos(li * step)
    ab_im = mag * jnp.sin(li * step)
    den = lr * lr + li * li
    f_re = ((ab_re - 1.0) * lr + ab_im * li) / den
    f_im = (ab_im * lr - (ab_re - 1.0) * li) / den
    bre, bim = b_re.astype(f32), b_im.astype(f32)
    bb_re = f_re[..., None] * bre - f_im[..., None] * bim
    bb_im = f_re[..., None] * bim + f_im[..., None] * bre
    x_re = jnp.einsum('btgc,gpc->btgp', uf, bb_re)
    x_im = jnp.einsum('btgc,gpc->btgp', uf, bb_im)
    x_re = x_re.at[:, 0].add(ab_re * h0_re - ab_im * h0_im)
    x_im = x_im.at[:, 0].add(ab_re * h0_im + ab_im * h0_re)
    h_re, h_im = _clin_scan(jnp.broadcast_to(ab_re, x_re.shape), jnp.broadcast_to(ab_im, x_im.shape), x_re, x_im)
    y = (jnp.einsum('btgp,gcp->btgc', h_re, c_re.astype(f32))
         - jnp.einsum('btgp,gcp->btgc', h_im, c_im.astype(f32)))
    y = y.reshape(B, T, SSM_WIDTH) + d.astype(f32) * uf.reshape(B, T, SSM_WIDTH)
    z = jax.nn.gelu(y)
    z = z * jax.nn.sigmoid(z @ w_glu.astype(f32))
    return z, h_re[:, -1], h_im[:, -1]


def _rglru_branch(xc, conv_buf, h0, conv_w, conv_b, w_a, b_a, w_x, b_x, lam):
    B, T, _ = xc.shape
    xp = jnp.concatenate([conv_buf.astype(xc.dtype), xc], axis=1)
    conv = conv_b + conv_w[0] * xp[:, 0:T]
    for j in range(1, CONV_WIDTH):
        conv = conv + conv_w[j] * xp[:, j:j + T]
    xf = conv.astype(f32)
    xb = xf.reshape(B, T, LRU_BLOCKS, LRU_BLOCK_DIM)
    r = jax.nn.sigmoid(jnp.einsum('btnd,nde->btne', xb, w_a.astype(f32)).reshape(B, T, LRU_WIDTH) + b_a.astype(f32))
    i = jax.nn.sigmoid(jnp.einsum('btnd,nde->btne', xb, w_x.astype(f32)).reshape(B, T, LRU_WIDTH) + b_x.astype(f32))
    log_a = LRU_C * r * jax.nn.log_sigmoid(lam.astype(f32))
    a = jnp.exp(log_a)
    b = xf * i * jnp.sqrt(-jnp.expm1(2.0 * log_a))
    b = b.at[:, 0].add(a[:, 0] * h0.astype(f32))
    h = _lin_scan(a, b)
    return h, h[:, -1], xp[:, -(CONV_WIDTH - 1):]


def _layernorm(h, g, b):
    hf = h.astype(f32)
    mu = hf.mean(-1, keepdims=True)
    var = jnp.square(hf - mu).mean(-1, keepdims=True)
    return (hf - mu) * lax.rsqrt(var + LN_EPS) * g.astype(f32) + b.astype(f32)


def _layer(x, p, rel_bias, state):
    B, T, _ = x.shape
    dt = x.dtype
    proj = x @ p['w_in']
    q, k, v, g_a, u_b, g_b, x_c, g_c, g_m = jnp.split(proj, _split_points(), axis=-1)
    if state is None:
        attn = _attn_prompt(q, k, v, rel_bias, p['sinks'])
        k_buf = k[:, -WINDOW:].reshape(B, WINDOW, N_KV_HEADS, HEAD_DIM)
        v_buf = v[:, -WINDOW:].reshape(B, WINDOW, N_KV_HEADS, HEAD_DIM)
        h0_re = jnp.zeros((B, N_SSM_GROUPS, SSM_STATE), f32)
        h0_im = jnp.zeros((B, N_SSM_GROUPS, SSM_STATE), f32)
        lru_h0 = jnp.zeros((B, LRU_WIDTH), f32)
        conv_buf = jnp.zeros((B, CONV_WIDTH - 1, LRU_WIDTH), dt)
    else:
        k_cache, v_cache, h0_re, h0_im, lru_h0, conv_buf = state
        attn, k_buf, v_buf = _attn_sample(q, k, v, k_cache, v_cache, rel_bias, p['sinks'])
    y_b, h_re, h_im = _s5_branch(u_b, h0_re.astype(f32), h0_im.astype(f32), p['lam_re'], p['lam_im'],
                                 p['log_step'], p['b_re'], p['b_im'], p['c_re'], p['c_im'], p['d'], p['w_glu'])
    y_c, lru_h, conv_new = _rglru_branch(x_c, conv_buf, lru_h0, p['conv_w'], p['conv_b'], p['w_a'],
                                         p['b_a'], p['w_x'], p['b_x'], p['lam'])
    ya = (attn * jax.nn.silu(g_a.astype(f32))).astype(dt) @ p['w_br_a']
    yb = (y_b * jax.nn.silu(g_b.astype(f32))).astype(dt) @ p['w_br_b']
    yc = (y_c * jax.nn.silu(g_c.astype(f32))).astype(dt) @ p['w_br_c']
    ga, gb, gc = jnp.split(jax.nn.sigmoid(g_m.astype(f32)), 3, axis=-1)
    merged = (ga * ya.astype(f32) + gb * yb.astype(f32) + gc * yc.astype(f32)).astype(dt)
    out = merged @ p['w_out']
    y = _layernorm(DN_ALPHA * x.astype(f32) + out.astype(f32), p['ln_g'], p['ln_b']).astype(dt)
    return y, (k_buf, v_buf, h_re, h_im, lru_h, conv_new)


def setup_inputs(seed: int = 0) -> dict:
    key = jax.random.key(seed)
    ks = iter(jax.random.split(key, 40))

    def nrm(shape, scale):
        return scale * jax.random.normal(next(ks), shape, f32)

    win = min(WINDOW, PAST_LEN)
    x_prompt = nrm((BATCH, SEQ, D_MODEL), 1.0)
    x_sample = nrm((DEC_BATCH, DEC_SEQ, D_MODEL), 1.0)
    cache_k = nrm((DEPTH, DEC_BATCH, win, N_KV_HEADS, HEAD_DIM), 1.0)
    cache_v = nrm((DEPTH, DEC_BATCH, win, N_KV_HEADS, HEAD_DIM), 1.0)
    state_ssm_re = nrm((DEPTH, DEC_BATCH, N_SSM_GROUPS, SSM_STATE), 0.1)
    state_ssm_im = nrm((DEPTH, DEC_BATCH, N_SSM_GROUPS, SSM_STATE), 0.1)
    state_lru = nrm((DEPTH, DEC_BATCH, LRU_WIDTH), 0.5)
    state_conv = nrm((DEPTH, DEC_BATCH, CONV_WIDTH - 1, LRU_WIDTH), 1.0)
    rel_bias = nrm((N_BUCKETS, N_Q_HEADS), 0.5)
    w_in = nrm((DEPTH, D_MODEL, IN_COLS), D_MODEL ** -0.5)
    sinks = nrm((DEPTH, N_Q_HEADS), 0.5)
    w_branch_a = nrm((DEPTH, BRANCH_WIDTH, D_MODEL), BRANCH_WIDTH ** -0.5)
    ssm_lambda_re = -0.5 + nrm((DEPTH, N_SSM_GROUPS, SSM_STATE), 0.01)
    ssm_lambda_im = math.pi * jnp.arange(SSM_STATE, dtype=f32) + nrm((DEPTH, N_SSM_GROUPS, SSM_STATE), 0.01)
    ssm_log_step = jax.random.uniform(next(ks), (DEPTH, N_SSM_GROUPS), f32, math.log(1e-3), math.log(1e-1))
    ssm_b_re = nrm((DEPTH, N_SSM_GROUPS, SSM_STATE, SSM_GROUP), (2 * SSM_GROUP) ** -0.5)
    ssm_b_im = nrm((DEPTH, N_SSM_GROUPS, SSM_STATE, SSM_GROUP), (2 * SSM_GROUP) ** -0.5)
    ssm_c_re = nrm((DEPTH, N_SSM_GROUPS, SSM_GROUP, SSM_STATE), (2 * SSM_STATE) ** -0.5)
    ssm_c_im = nrm((DEPTH, N_SSM_GROUPS, SSM_GROUP, SSM_STATE), (2 * SSM_STATE) ** -0.5)
    ssm_d = nrm((DEPTH, SSM_WIDTH), 1.0)
    ssm_w_glu = nrm((DEPTH, SSM_WIDTH, SSM_WIDTH), SSM_WIDTH ** -0.5)
    w_branch_b = nrm((DEPTH, SSM_WIDTH, D_MODEL), SSM_WIDTH ** -0.5)
    conv_w = nrm((DEPTH, CONV_WIDTH, LRU_WIDTH), CONV_WIDTH ** -0.5)
    conv_b = nrm((DEPTH, LRU_WIDTH), 0.01)
    lru_w_a = nrm((DEPTH, LRU_BLOCKS, LRU_BLOCK_DIM, LRU_BLOCK_DIM), LRU_BLOCK_DIM ** -0.5)
    lru_b_a = nrm((DEPTH, LRU_WIDTH), 0.01)
    lru_w_x = nrm((DEPTH, LRU_BLOCKS, LRU_BLOCK_DIM, LRU_BLOCK_DIM), LRU_BLOCK_DIM ** -0.5)
    lru_b_x = nrm((DEPTH, LRU_WIDTH), 0.01)
    a_c = jax.random.uniform(next(ks), (DEPTH, LRU_WIDTH), f32, 0.9, 0.999)
    s = a_c ** (1.0 / LRU_C)
    lru_lambda = jnp.log(s) - jnp.log1p(-s)
    w_branch_c = nrm((DEPTH, LRU_WIDTH, D_MODEL), LRU_WIDTH ** -0.5)
    w_out = nrm((DEPTH, D_MODEL, D_MODEL), DN_BETA * D_MODEL ** -0.5)
    ln_g = 1.0 + nrm((DEPTH, D_MODEL), 0.01)
    ln_b = nrm((DEPTH, D_MODEL), 0.01)
    return {'x_prompt': x_prompt, 'x_sample': x_sample, 'cache_k': cache_k, 'cache_v': cache_v,
            'state_ssm_re': state_ssm_re, 'state_ssm_im': state_ssm_im, 'state_lru': state_lru,
            'state_conv': state_conv, 'rel_bias': rel_bias, 'w_in': w_in, 'sinks': sinks,
            'w_branch_a': w_branch_a, 'ssm_lambda_re': ssm_lambda_re, 'ssm_lambda_im': ssm_lambda_im,
            'ssm_log_step': ssm_log_step, 'ssm_b_re': ssm_b_re, 'ssm_b_im': ssm_b_im,
            'ssm_c_re': ssm_c_re, 'ssm_c_im': ssm_c_im, 'ssm_d': ssm_d, 'ssm_w_glu': ssm_w_glu,
            'w_branch_b': w_branch_b, 'conv_w': conv_w, 'conv_b': conv_b, 'lru_w_a': lru_w_a,
            'lru_b_a': lru_b_a, 'lru_w_x': lru_w_x, 'lru_b_x': lru_b_x, 'lru_lambda': lru_lambda,
            'w_branch_c': w_branch_c, 'w_out': w_out, 'ln_g': ln_g, 'ln_b': ln_b}


def reference(x_prompt, x_sample, cache_k, cache_v, state_ssm_re, state_ssm_im, state_lru, state_conv,
              rel_bias, w_in, sinks, w_branch_a, ssm_lambda_re, ssm_lambda_im, ssm_log_step,
              ssm_b_re, ssm_b_im, ssm_c_re, ssm_c_im, ssm_d, ssm_w_glu, w_branch_b, conv_w, conv_b,
              lru_w_a, lru_b_a, lru_w_x, lru_b_x, lru_lambda, w_branch_c, w_out, ln_g, ln_b):
    y_prompt, y_sample = x_prompt, x_sample
    st_p, st_s = [], []
    for l in range(DEPTH):
        prm = {'w_in': w_in[l], 'sinks': sinks[l], 'w_br_a': w_branch_a[l],
               'lam_re': ssm_lambda_re[l], 'lam_im': ssm_lambda_im[l], 'log_step': ssm_log_step[l],
               'b_re': ssm_b_re[l], 'b_im': ssm_b_im[l], 'c_re': ssm_c_re[l], 'c_im': ssm_c_im[l],
               'd': ssm_d[l], 'w_glu': ssm_w_glu[l], 'w_br_b': w_branch_b[l],
               'conv_w': conv_w[l], 'conv_b': conv_b[l], 'w_a': lru_w_a[l], 'b_a': lru_b_a[l],
               'w_x': lru_w_x[l], 'b_x': lru_b_x[l], 'lam': lru_lambda[l], 'w_br_c': w_branch_c[l],
               'w_out': w_out[l], 'ln_g': ln_g[l], 'ln_b': ln_b[l]}
        y_prompt, sp = _layer(y_prompt, prm, rel_bias, None)
        y_sample, ss = _layer(y_sample, prm, rel_bias,
                              (cache_k[l], cache_v[l], state_ssm_re[l], state_ssm_im[l], state_lru[l], state_conv[l]))
        st_p.append(sp)
        st_s.append(ss)
    new_k_prompt = jnp.stack([s[0] for s in st_p])
    new_v_prompt = jnp.stack([s[1] for s in st_p])
    new_ssm_re_prompt = jnp.stack([s[2] for s in st_p])
    new_ssm_im_prompt = jnp.stack([s[3] for s in st_p])
    new_lru_prompt = jnp.stack([s[4] for s in st_p])
    new_conv_prompt = jnp.stack([s[5] for s in st_p])
    new_k_sample = jnp.stack([s[0] for s in st_s])
    new_v_sample = jnp.stack([s[1] for s in st_s])
    new_ssm_re_sample = jnp.stack([s[2] for s in st_s])
    new_ssm_im_sample = jnp.stack([s[3] for s in st_s])
    new_lru_sample = jnp.stack([s[4] for s in st_s])
    new_conv_sample = jnp.stack([s[5] for s in st_s])
    return (y_prompt, y_sample, new_k_prompt, new_v_prompt, new_ssm_re_prompt, new_ssm_im_prompt,
            new_lru_prompt, new_conv_prompt, new_k_sample, new_v_sample, new_ssm_re_sample,
            new_ssm_im_sample, new_lru_sample, new_conv_sample)
```

```python
import functools
import math

import jax
import jax.numpy as jnp
from jax import lax
from jax.experimental import pallas as pl
from jax.experimental.pallas import tpu as pltpu

f32 = jnp.float32
bf16 = jnp.bfloat16

D_MODEL = 1024
BRANCH = 512
HEAD_DIM = 64
N_Q_HEADS = 8
WINDOW = 128
N_BUCKETS = 32
MAX_DISTANCE = 128
SSM_GROUPS = 32
SSM_GROUP = 16
SSM_STATE = 64
LRU_BLOCKS = 8
CONV_WIDTH = 4
LRU_C = 8.0
LN_EPS = 1e-5
NEG = -1e30

LANES = 128
SUBLANES = 8
VMEM_LIMIT = 56 * 1024 * 1024

C_Q, C_K, C_V, C_GA, C_UB, C_GB, C_XC, C_GC, C_GM = (
    (0, 512), (512, 640), (640, 768), (768, 1280), (1280, 1792), (1792, 2304),
    (2304, 2816), (2816, 3328), (3328, 6400))
MIX_COLS = C_GC[1]
HALF = SSM_GROUPS // 2 * SSM_STATE

V_CONVW, V_CONVB, V_BA, V_BX, V_LAM, V_D = 0, 4, 5, 6, 7, 8


def _mixer_kernel(x_ref, wa_ref, bias_ref, sinks_ref, bbh_ref, cch_ref, ab_ref, wglu_ref, wlru_ref,
                  vec_ref, kv0_ref, ssm0_ref, lru0_ref, conv0_ref,
                  z_ref, kvo_ref, ssmo_ref, lruo_ref, convo_ref,
                  qs, kvs, atts, xs, hst, xcs, la, lb, lst, *, nb, tt, nts, mask_prev):
    ts = pl.program_id(1)
    m = nb * tt
    w = WINDOW * nb
    s_len = WINDOW + tt
    last = ts == nts - 1

    @pl.when(ts == 0)
    def _load_state():
        kvs[:, 0:w, :] = kv0_ref[0]
        hst[...] = ssm0_ref[0]
        lst[...] = lru0_ref[0]
        xcs[0:3 * nb, :] = conv0_ref[0]

    xb = x_ref[...].astype(bf16)

    def proj(cols):
        return jnp.dot(xb, wa_ref[:, cols[0]:cols[1]], preferred_element_type=f32)

    q = proj(C_Q)
    for j in range(4):
        qs[j] = q[:, LANES * j:LANES * (j + 1)]
    kvs[0, w:w + m, :] = proj(C_K)
    kvs[1, w:w + m, :] = proj(C_V)

    qi = lax.broadcasted_iota(jnp.int32, (tt, s_len), 0)
    kj = lax.broadcasted_iota(jnp.int32, (tt, s_len), 1)
    dist = qi + WINDOW - kj
    band = (dist >= 0) & (dist < WINDOW)
    if mask_prev:
        band = band & (kj + ts * tt >= WINDOW)
    lo = lax.broadcasted_iota(jnp.int32, (s_len, LANES), 1) < HEAD_DIM

    def attend(b, carry):
        kk = kvs.at[0][pl.ds(b, s_len, stride=nb), :]
        vv = kvs.at[1][pl.ds(b, s_len, stride=nb), :]
        k_lo = jnp.where(lo, kk, 0.0)
        k_hi = jnp.where(lo, 0.0, kk)
        v_lo = jnp.where(lo, vv, 0.0)
        v_hi = jnp.where(lo, 0.0, vv)
        kx = ((k_lo, pltpu.roll(k_lo, HEAD_DIM, axis=1)), (pltpu.roll(k_hi, HEAD_DIM, axis=1), k_hi))
        vx = ((v_lo, pltpu.roll(v_lo, HEAD_DIM, axis=1)), (pltpu.roll(v_hi, HEAD_DIM, axis=1), v_hi))
        kx = tuple(tuple(a.astype(bf16) for a in r) for r in kx)
        vx = tuple(tuple(a.astype(bf16) for a in r) for r in vx)
        for j in range(4):
            qp = qs.at[j][pl.ds(b, tt, stride=nb), :].astype(bf16)
            acc = None
            for e in range(2):
                h = 2 * j + e
                sc = lax.dot_general(qp, kx[j // 2][e], (((1,), (1,)), ((), ())),
                                     preferred_element_type=f32)
                sc = sc * (HEAD_DIM ** -0.5) + bias_ref[h]
                sc = jnp.where(band, sc, NEG)
                sink = sinks_ref[h]
                mx = jnp.maximum(jnp.max(sc, axis=-1, keepdims=True), sink)
                p = jnp.exp(sc - mx)
                den = jnp.sum(p, axis=-1, keepdims=True) + jnp.exp(sink - mx)
                o = jnp.dot(p.astype(bf16), vx[j // 2][e], preferred_element_type=f32) / den
                acc = o if acc is None else acc + o
            atts.at[j][pl.ds(b, tt, stride=nb), :] = acc
        return carry

    lax.fori_loop(0, nb, attend, 0)

    att = jnp.concatenate([atts[j] for j in range(4)], axis=1)
    z_ref[:, 0:BRANCH] = (att * jax.nn.silu(proj(C_GA))).astype(bf16)

    @pl.when(last)
    def _store_kv():
        kvo_ref[0] = kvs[:, m:m + w, :]

    if nts > 1:
        kvs[:, 0:w, :] = kvs[:, m:m + w, :]

    ub = proj(C_UB)
    ub16 = ub.astype(bf16)
    ys = []
    cw = HALF // 2
    for hf in range(2):
        xs[...] = jnp.dot(ub16[:, 256 * hf:256 * (hf + 1)], bbh_ref[hf], preferred_element_type=f32)
        for ck in range(2):
            re0, im0 = cw * ck, HALF + cw * ck
            ar = jnp.broadcast_to(ab_ref[2 * hf:2 * hf + 1, re0:re0 + cw], (SUBLANES, cw))
            ai = jnp.broadcast_to(ab_ref[2 * hf + 1:2 * hf + 2, re0:re0 + cw], (SUBLANES, cw))
            for bg in range(nb // SUBLANES):
                b0 = SUBLANES * bg

                def step(t, c, b0=b0, re0=re0, im0=im0, ar=ar, ai=ai):
                    hr, hi = c
                    r0 = pl.multiple_of(t * nb + b0, SUBLANES)
                    xr = xs[pl.ds(r0, SUBLANES), re0:re0 + cw]
                    xi = xs[pl.ds(r0, SUBLANES), im0:im0 + cw]
                    nr = ar * hr - ai * hi + xr
                    ni = ar * hi + ai * hr + xi
                    xs[pl.ds(r0, SUBLANES), re0:re0 + cw] = nr
                    xs[pl.ds(r0, SUBLANES), im0:im0 + cw] = ni
                    return nr, ni

                h0 = (hst[hf, b0:b0 + SUBLANES, re0:re0 + cw], hst[hf, b0:b0 + SUBLANES, im0:im0 + cw])
                hr, hi = lax.fori_loop(0, tt, step, h0)
                hst[hf, b0:b0 + SUBLANES, re0:re0 + cw] = hr
                hst[hf, b0:b0 + SUBLANES, im0:im0 + cw] = hi
        ys.append(jnp.dot(xs[...].astype(bf16), cch_ref[hf], preferred_element_type=f32))
    y = jnp.concatenate(ys, axis=1) + vec_ref[V_D:V_D + 1, :] * ub
    zz = jax.nn.gelu(y)
    zz = zz * jax.nn.sigmoid(jnp.dot(zz.astype(bf16), wglu_ref[...], preferred_element_type=f32))
    z_ref[:, BRANCH:2 * BRANCH] = (zz * jax.nn.silu(proj(C_GB))).astype(bf16)

    @pl.when(last)
    def _store_ssm():
        ssmo_ref[0] = hst[...]

    xcs[3 * nb:3 * nb + m, :] = proj(C_XC)
    conv = vec_ref[V_CONVB:V_CONVB + 1, :] + vec_ref[V_CONVW:V_CONVW + 1, :] * xcs[0:m, :]
    for j in range(1, CONV_WIDTH):
        conv = conv + vec_ref[V_CONVW + j:V_CONVW + j + 1, :] * xcs[j * nb:j * nb + m, :]
    c16 = conv.astype(bf16)

    def block_dot(g):
        return jnp.concatenate(
            [jnp.dot(c16[:, 256 * hf:256 * (hf + 1)], wlru_ref[g, hf], preferred_element_type=f32)
             for hf in range(2)], axis=1)

    r = jax.nn.sigmoid(block_dot(0) + vec_ref[V_BA:V_BA + 1, :])
    i = jax.nn.sigmoid(block_dot(1) + vec_ref[V_BX:V_BX + 1, :])
    log_a = LRU_C * r * jax.nn.log_sigmoid(vec_ref[V_LAM:V_LAM + 1, :])
    a = jnp.exp(log_a)
    la[...] = a
    lb[...] = conv * i * jnp.sqrt(-jnp.tanh(log_a) * (a * a + 1.0))
    for bg in range(nb // SUBLANES):
        b0 = SUBLANES * bg

        def lru_step(t, h, b0=b0):
            r0 = pl.multiple_of(t * nb + b0, SUBLANES)
            h = la[pl.ds(r0, SUBLANES), :] * h + lb[pl.ds(r0, SUBLANES), :]
            lb[pl.ds(r0, SUBLANES), :] = h
            return h

        lst[b0:b0 + SUBLANES, :] = lax.fori_loop(0, tt, lru_step, lst[b0:b0 + SUBLANES, :])
    z_ref[:, 2 * BRANCH:3 * BRANCH] = (lb[...] * jax.nn.silu(proj(C_GC))).astype(bf16)

    @pl.when(last)
    def _store_lru():
        lruo_ref[0] = lst[...]
        convo_ref[0] = xcs[m:m + 3 * nb, :]

    if nts > 1:
        xcs[0:3 * nb, :] = xcs[m:m + 3 * nb, :]


def _out_kernel(x_ref, z_ref, wgm_ref, wbr_ref, wout_ref, ln_ref, y_ref, *, alpha):
    x = x_ref[...]
    xb = x.astype(bf16)
    merged = None
    for k in range(3):
        g = jax.nn.sigmoid(jnp.dot(xb, wgm_ref[:, D_MODEL * k:D_MODEL * (k + 1)], preferred_element_type=f32))
        yk = jnp.dot(z_ref[:, BRANCH * k:BRANCH * (k + 1)], wbr_ref[k], preferred_element_type=f32)
        merged = g * yk if merged is None else merged + g * yk
    out = jnp.dot(merged.astype(bf16), wout_ref[...], preferred_element_type=f32)
    h = alpha * x + out
    mu = jnp.mean(h, axis=-1, keepdims=True)
    d = h - mu
    var = jnp.mean(d * d, axis=-1, keepdims=True)
    y_ref[...] = d * lax.rsqrt(var + LN_EPS) * ln_ref[0:1, :] + ln_ref[1:2, :]


def _const_spec(shape):
    return pl.BlockSpec(shape, lambda *_: (0,) * len(shape), pipeline_mode=pl.Buffered(1))


def _mixer_call(x2d, lp, state, *, nbg, nts, nb, tt, mask_prev, name):
    m, w, s_len = nb * tt, WINDOW * nb, WINDOW + tt
    kv0, ssm0, lru0, conv0 = state
    kern = functools.partial(_mixer_kernel, nb=nb, tt=tt, nts=nts, mask_prev=mask_prev)
    in_specs = [
        pl.BlockSpec((m, D_MODEL), lambda g, t: (g * nts + t, 0)),
        _const_spec((D_MODEL, MIX_COLS)),
        _const_spec((N_Q_HEADS, tt, s_len)),
        pl.BlockSpec(memory_space=pltpu.SMEM),
        _const_spec((2, 256, 2 * HALF)),
        _const_spec((2, 2 * HALF, 256)),
        _const_spec((4, HALF)),
        _const_spec((BRANCH, BRANCH)),
        _const_spec((2, 2, 256, 256)),
        _const_spec((16, BRANCH)),
        pl.BlockSpec((1, 2, w, LANES), lambda g, t: (g, 0, 0, 0)),
        pl.BlockSpec((1, 2, nb, 2 * HALF), lambda g, t: (g, 0, 0, 0)),
        pl.BlockSpec((1, nb, BRANCH), lambda g, t: (g, 0, 0)),
        pl.BlockSpec((1, 3 * nb, BRANCH), lambda g, t: (g, 0, 0)),
    ]
    out_specs = [
        pl.BlockSpec((m, 3 * BRANCH), lambda g, t: (g * nts + t, 0)),
        pl.BlockSpec((1, 2, w, LANES), lambda g, t: (g, 0, 0, 0)),
        pl.BlockSpec((1, 2, nb, 2 * HALF), lambda g, t: (g, 0, 0, 0)),
        pl.BlockSpec((1, nb, BRANCH), lambda g, t: (g, 0, 0)),
        pl.BlockSpec((1, 3 * nb, BRANCH), lambda g, t: (g, 0, 0)),
    ]
    out_shape = [
        jax.ShapeDtypeStruct((nbg * nts * m, 3 * BRANCH), bf16),
        jax.ShapeDtypeStruct((nbg, 2, w, LANES), f32),
        jax.ShapeDtypeStruct((nbg, 2, nb, 2 * HALF), f32),
        jax.ShapeDtypeStruct((nbg, nb, BRANCH), f32),
        jax.ShapeDtypeStruct((nbg, 3 * nb, BRANCH), f32),
    ]
    scratch = [
        pltpu.VMEM((4, m, LANES), f32),
        pltpu.VMEM((2, w + m, LANES), f32),
        pltpu.VMEM((4, m, LANES), f32),
        pltpu.VMEM((m, 2 * HALF), f32),
        pltpu.VMEM((2, nb, 2 * HALF), f32),
        pltpu.VMEM((m + 3 * nb, BRANCH), f32),
        pltpu.VMEM((m, BRANCH), f32),
        pltpu.VMEM((m, BRANCH), f32),
        pltpu.VMEM((nb, BRANCH), f32),
    ]
    return pl.pallas_call(
        kern, grid=(nbg, nts), in_specs=in_specs, out_specs=out_specs, out_shape=out_shape,
        scratch_shapes=scratch, name=name,
        compiler_params=pltpu.CompilerParams(
            dimension_semantics=("arbitrary", "arbitrary"), vmem_limit_bytes=VMEM_LIMIT),
    )(x2d, lp["wa"], lp["bias"][tt], lp["sinks"], lp["bbh"], lp["cch"], lp["ab"], lp["wglu"],
      lp["wlru"], lp["vec"], kv0, ssm0, lru0, conv0)


def _out_call(x2d, z2d, lp, *, mb, alpha, name):
    rows = x2d.shape[0]
    return pl.pallas_call(
        functools.partial(_out_kernel, alpha=alpha),
        grid=(rows // mb,),
        in_specs=[
            pl.BlockSpec((mb, D_MODEL), lambda i: (i, 0)),
            pl.BlockSpec((mb, 3 * BRANCH), lambda i: (i, 0)),
            _const_spec((D_MODEL, 3 * D_MODEL)),
            _const_spec((3, BRANCH, D_MODEL)),
            _const_spec((D_MODEL, D_MODEL)),
            _const_spec((2, D_MODEL)),
        ],
        out_specs=pl.BlockSpec((mb, D_MODEL), lambda i: (i, 0)),
        out_shape=jax.ShapeDtypeStruct((rows, D_MODEL), f32),
        name=name,
        compiler_params=pltpu.CompilerParams(
            dimension_semantics=("arbitrary",), vmem_limit_bytes=VMEM_LIMIT),
    )(x2d, z2d, lp["wgm"], lp["wbr"], lp["wout"], lp["ln"])


def _t5_bucket(dist):
    max_exact = N_BUCKETS // 2
    d = jnp.maximum(dist, 0)
    large = max_exact + (jnp.log(jnp.maximum(d, 1).astype(f32) / max_exact)
                         / math.log(MAX_DISTANCE / max_exact) * (N_BUCKETS - max_exact)).astype(jnp.int32)
    large = jnp.minimum(large, N_BUCKETS - 1)
    return jnp.where(d < max_exact, d, large)


def _bias_table(rel_bias, tt):
    s_len = WINDOW + tt
    dist = jnp.arange(tt)[:, None] + WINDOW - jnp.arange(s_len)[None, :]
    return rel_bias.astype(f32)[_t5_bucket(dist)].transpose(2, 0, 1)


def _block_diag(blocks):
    n, r, c = blocks.shape
    eye = jnp.eye(n, dtype=blocks.dtype)
    return jnp.einsum("nrc,nk->nrkc", blocks, eye).reshape(n * r, n * c)


def _layer_params(l, tts, rel_bias, w_in, sinks, w_branch_a, lam_re, lam_im, log_step, b_re, b_im, c_re, c_im,
                  ssm_d, w_glu, w_branch_b, conv_w, conv_b, w_a, b_a, w_x, b_x, lam, w_branch_c, w_out,
                  ln_g, ln_b):
    lr = jnp.minimum(lam_re[l].astype(f32), -1e-4)
    li = lam_im[l].astype(f32)
    step = jnp.exp(log_step[l].astype(f32))[:, None]
    mag = jnp.exp(lr * step)
    ab_re = mag * jnp.cos(li * step)
    ab_im = mag * jnp.sin(li * step)
    den = lr * lr + li * li
    f_re = ((ab_re - 1.0) * lr + ab_im * li) / den
    f_im = (ab_im * lr - (ab_re - 1.0) * li) / den
    bre, bim = b_re[l].astype(f32), b_im[l].astype(f32)
    bb_re = f_re[..., None] * bre - f_im[..., None] * bim
    bb_im = f_re[..., None] * bim + f_im[..., None] * bre
    hg = SSM_GROUPS // 2

    def drive(bb):
        t = bb.transpose(0, 2, 1).reshape(2, hg, SSM_GROUP, SSM_STATE)
        return jnp.stack([_block_diag(t[0]), _block_diag(t[1])])

    def readout(cc):
        t = cc.astype(f32).transpose(0, 2, 1).reshape(2, hg, SSM_STATE, SSM_GROUP)
        return jnp.stack([_block_diag(t[0]), _block_diag(t[1])])

    bbh = jnp.concatenate([drive(bb_re), drive(bb_im)], axis=2).astype(bf16)
    cch = jnp.concatenate([readout(c_re[l]), -readout(c_im[l])], axis=1).astype(bf16)
    ab = jnp.stack([ab_re.reshape(2, HALF)[0], ab_im.reshape(2, HALF)[0],
                    ab_re.reshape(2, HALF)[1], ab_im.reshape(2, HALF)[1]])

    def lru_blocks(wt):
        t = wt.astype(f32).reshape(2, LRU_BLOCKS // 2, 64, 64)
        return jnp.stack([_block_diag(t[0]), _block_diag(t[1])])

    vec = jnp.zeros((16, BRANCH), f32)
    vec = vec.at[V_CONVW:V_CONVW + CONV_WIDTH].set(conv_w[l].astype(f32))
    vec = vec.at[V_CONVB].set(conv_b[l].astype(f32))
    vec = vec.at[V_BA].set(b_a[l].astype(f32))
    vec = vec.at[V_BX].set(b_x[l].astype(f32))
    vec = vec.at[V_LAM].set(lam[l].astype(f32))
    vec = vec.at[V_D].set(ssm_d[l].astype(f32))
    return {
        "wa": w_in[l][:, :MIX_COLS].astype(bf16),
        "wgm": w_in[l][:, MIX_COLS:].astype(bf16),
        "bias": {tt: _bias_table(rel_bias, tt) for tt in tts},
        "sinks": sinks[l].astype(f32),
        "bbh": bbh, "cch": cch, "ab": ab,
        "wglu": w_glu[l].astype(bf16),
        "wlru": jnp.stack([lru_blocks(w_a[l]), lru_blocks(w_x[l])]).astype(bf16),
        "vec": vec,
        "wbr": jnp.stack([w_branch_a[l], w_branch_b[l], w_branch_c[l]]).astype(bf16),
        "wout": w_out[l].astype(bf16),
        "ln": jnp.stack([ln_g[l], ln_b[l]]).astype(f32),
    }


def _group_rows(x, nbg, nb):
    b, t, c = x.shape
    return x.reshape(nbg, nb, t, c).transpose(0, 2, 1, 3).reshape(nbg * t * nb, c)


def _ungroup_rows(x2d, nbg, nb, t):
    c = x2d.shape[-1]
    return x2d.reshape(nbg, t, nb, c).transpose(0, 2, 1, 3).reshape(nbg * nb, t, c)


def _state_in(nbg, nb, cache_k, cache_v, ssm_re, ssm_im, lru, conv):
    def kvw(c):
        return c.astype(f32).reshape(nbg, nb, WINDOW, LANES).transpose(0, 2, 1, 3).reshape(nbg, WINDOW * nb, LANES)

    def ssm(s):
        return s.astype(f32).reshape(nbg, nb, 2, HALF).transpose(0, 2, 1, 3)

    kv0 = jnp.stack([kvw(cache_k), kvw(cache_v)], axis=1)
    ssm0 = jnp.concatenate([ssm(ssm_re), ssm(ssm_im)], axis=-1)
    lru0 = lru.astype(f32).reshape(nbg, nb, BRANCH)
    conv0 = conv.astype(f32).reshape(nbg, nb, 3, BRANCH).transpose(0, 2, 1, 3).reshape(nbg, 3 * nb, BRANCH)
    return kv0, ssm0, lru0, conv0


def _state_out(nbg, nb, kvo, ssmo, lruo, convo):
    def kvw(c):
        return c.reshape(nbg, WINDOW, nb, 2, HEAD_DIM).transpose(0, 2, 1, 3, 4).reshape(nbg * nb, WINDOW, 2, HEAD_DIM)

    def ssm(s):
        return s.transpose(0, 2, 1, 3).reshape(nbg * nb, SSM_GROUPS, SSM_STATE)

    return (kvw(kvo[:, 0]), kvw(kvo[:, 1]), ssm(ssmo[..., :HALF]), ssm(ssmo[..., HALF:]),
            lruo.reshape(nbg * nb, BRANCH),
            convo.reshape(nbg, 3, nb, BRANCH).transpose(0, 2, 1, 3).reshape(nbg * nb, 3, BRANCH))


P_TT = 64
S_NBG = 8
OUT_ROWS = 512


def kernel(x_prompt, x_sample, cache_k, cache_v, state_ssm_re, state_ssm_im, state_lru, state_conv, rel_bias, w_in, sinks, w_branch_a, ssm_lambda_re, ssm_lambda_im, ssm_log_step, ssm_b_re, ssm_b_im, ssm_c_re, ssm_c_im, ssm_d, ssm_w_glu, w_branch_b, conv_w, conv_b, lru_w_a, lru_b_a, lru_w_x, lru_b_x, lru_lambda, w_branch_c, w_out, ln_g, ln_b):
    depth = w_in.shape[0]
    alpha = (2 * depth) ** 0.25
    pb, pt, _ = x_prompt.shape
    sb, st, _ = x_sample.shape
    s_nb = sb // S_NBG
    p_nts = pt // P_TT

    xp = _group_rows(x_prompt, 1, pb)
    xs = _group_rows(x_sample, S_NBG, s_nb)
    zero_state = _state_in(
        1, pb, jnp.zeros((pb, WINDOW, 2, HEAD_DIM), f32), jnp.zeros((pb, WINDOW, 2, HEAD_DIM), f32),
        jnp.zeros((pb, SSM_GROUPS, SSM_STATE), f32), jnp.zeros((pb, SSM_GROUPS, SSM_STATE), f32),
        jnp.zeros((pb, BRANCH), f32), jnp.zeros((pb, 3, BRANCH), f32))
    st_p, st_s = [], []
    for l in range(depth):
        lp = _layer_params(l, (P_TT, st), rel_bias, w_in, sinks, w_branch_a, ssm_lambda_re, ssm_lambda_im,
                           ssm_log_step, ssm_b_re, ssm_b_im, ssm_c_re, ssm_c_im, ssm_d, ssm_w_glu,
                           w_branch_b, conv_w, conv_b, lru_w_a, lru_b_a, lru_w_x, lru_b_x, lru_lambda,
                           w_branch_c, w_out, ln_g, ln_b)
        zp, *sp = _mixer_call(xp, lp, zero_state, nbg=1, nts=p_nts, nb=pb, tt=P_TT, mask_prev=True,
                              name=f"mixer_prompt_{l}")
        xp = _out_call(xp, zp, lp, mb=OUT_ROWS, alpha=alpha, name=f"out_prompt_{l}")
        st_p.append(_state_out(1, pb, *sp))
        s_in = _state_in(S_NBG, s_nb, cache_k[l], cache_v[l], state_ssm_re[l], state_ssm_im[l],
                         state_lru[l], state_conv[l])
        zs, *ss = _mixer_call(xs, lp, s_in, nbg=S_NBG, nts=1, nb=s_nb, tt=st, mask_prev=False,
                              name=f"mixer_sample_{l}")
        xs = _out_call(xs, zs, lp, mb=OUT_ROWS, alpha=alpha, name=f"out_sample_{l}")
        st_s.append(_state_out(S_NBG, s_nb, *ss))

    y_prompt = _ungroup_rows(xp, 1, pb, pt)
    y_sample = _ungroup_rows(xs, S_NBG, s_nb, st)
    stacked_p = [jnp.stack([s[k] for s in st_p]) for k in range(6)]
    stacked_s = [jnp.stack([s[k] for s in st_s]) for k in range(6)]
    return (y_prompt, y_sample, *stacked_p, *stacked_s)
```

```python
import functools
import math

import jax
import jax.numpy as jnp
from jax import lax
from jax.experimental import pallas as pl
from jax.experimental.pallas import tpu as pltpu

f32 = jnp.float32
bf16 = jnp.bfloat16

D_MODEL = 1024
BRANCH = 512
HEAD_DIM = 64
N_Q_HEADS = 8
WINDOW = 128
N_BUCKETS = 32
MAX_DISTANCE = 128
SSM_GROUPS = 32
SSM_GROUP = 16
SSM_STATE = 64
LRU_BLOCKS = 8
CONV_WIDTH = 4
LRU_C = 8.0
LN_EPS = 1e-5
NEG = -1e30

LANES = 128
SUBLANES = 8
VMEM_LIMIT = 56 * 1024 * 1024

C_Q, C_K, C_V, C_GA, C_UB, C_GB, C_XC, C_GC, C_GM = (
    (0, 512), (512, 640), (640, 768), (768, 1280), (1280, 1792), (1792, 2304),
    (2304, 2816), (2816, 3328), (3328, 6400))
MIX_COLS = C_GC[1]
HALF = SSM_GROUPS // 2 * SSM_STATE

V_CONVW, V_CONVB, V_BA, V_BX, V_LAM, V_D = 0, 4, 5, 6, 7, 8


def _split_heads(qa, lo, kvh):
    q_lo = jnp.where(lo, qa, 0.0)
    q_hi = jnp.where(lo, 0.0, qa)
    if kvh == 0:
        return [q_lo, pltpu.roll(q_hi, HEAD_DIM, axis=1)]
    return [pltpu.roll(q_lo, HEAD_DIM, axis=1), q_hi]


def _merge_heads(o0, o1, lo, kvh):
    if kvh == 0:
        return jnp.where(lo, o0, pltpu.roll(o1, HEAD_DIM, axis=1))
    return jnp.where(lo, pltpu.roll(o0, HEAD_DIM, axis=1), o1)


def _softmax_pv(q4, keys, vals, tabs, sink):
    scs = []
    for k, t in zip(keys, tabs):
        sc = lax.dot_general(q4, k, (((1,), (1,)), ((), ())), preferred_element_type=f32)
        scs.append(jnp.where(t > 0.5 * NEG, sc * (HEAD_DIM ** -0.5) + t, NEG))
    mx = sink
    for sc in scs:
        mx = jnp.maximum(mx, jnp.max(sc, axis=-1, keepdims=True))
    den = jnp.exp(sink - mx)
    o = None
    for sc, v in zip(scs, vals):
        p = jnp.exp(sc - mx)
        den = den + jnp.sum(p, axis=-1, keepdims=True)
        pv = jnp.dot(p.astype(bf16), v, preferred_element_type=f32)
        o = pv if o is None else o + pv
    return o / den


def _attend_prompt(last, qs, kvs, atts, tabm, sinks_ref, ko_ref, vo_ref, sc_scr, p_scr, *, nb, tt):
    s_len = WINDOW + tt
    lo = lax.broadcasted_iota(jnp.int32, (tt, LANES), 1) < HEAD_DIM

    for b in range(nb):
        kk16 = kvs.at[0][pl.ds(b, s_len, stride=nb), :].astype(bf16)
        for kvh in range(2):
            parts = []
            for i in range(2):
                parts += _split_heads(qs.at[2 * kvh + i][pl.ds(b, tt, stride=nb), :], lo, kvh)
            q4 = jnp.concatenate(parts, axis=0).astype(bf16)
            sc_scr[b, kvh] = lax.dot_general(q4, kk16, (((1,), (1,)), ((), ())), preferred_element_type=f32)

    for b in range(nb):
        for kvh in range(2):
            for g in range(4):
                rs = slice(g * tt, (g + 1) * tt)
                t = tabm[kvh, rs, :]
                sc = jnp.where(t > 0.5 * NEG, sc_scr[b, kvh, rs, :] * (HEAD_DIM ** -0.5) + t, NEG)
                sink = sinks_ref[4 * kvh + g]
                mx = jnp.maximum(jnp.max(sc, axis=-1, keepdims=True), sink)
                p = jnp.exp(sc - mx)
                den = jnp.sum(p, axis=-1, keepdims=True) + jnp.exp(sink - mx)
                p_scr[b, kvh, rs, :] = (p / den).astype(bf16)

    for b in range(nb):
        vv16 = kvs.at[1][pl.ds(b, s_len, stride=nb), :].astype(bf16)
        for kvh in range(2):
            o = jnp.dot(p_scr[b, kvh], vv16, preferred_element_type=f32)
            for i in range(2):
                slab = _merge_heads(o[2 * i * tt:(2 * i + 1) * tt], o[(2 * i + 1) * tt:(2 * i + 2) * tt], lo, kvh)
                atts.at[2 * kvh + i][pl.ds(b, tt, stride=nb), :] = slab

    @pl.when(last)
    def _store_cache():
        for b in range(nb):
            ko_ref[b] = kvs.at[0][pl.ds(tt * nb + b, WINDOW, stride=nb), :]
            vo_ref[b] = kvs.at[1][pl.ds(tt * nb + b, WINDOW, stride=nb), :]


def _attend_sample(qs, kvn, atts, kc_ref, vc_ref, tab1_ref, tab2_ref, sink_ref, ko_ref, vo_ref, *, nb, tt):
    lo = lax.broadcasted_iota(jnp.int32, (SUBLANES * tt, LANES), 1) < HEAD_DIM
    for sg in range(nb // SUBLANES):
        b0 = SUBLANES * sg

        def rows(ref, b0=b0):
            return jnp.concatenate([ref[t * nb + b0:t * nb + b0 + SUBLANES, :] for t in range(tt)], axis=0)

        kc = kc_ref[b0:b0 + SUBLANES].reshape(SUBLANES * WINDOW, LANES).astype(bf16)
        vc = vc_ref[b0:b0 + SUBLANES].reshape(SUBLANES * WINDOW, LANES).astype(bf16)
        kn = rows(kvn.at[0]).astype(bf16)
        vn = rows(kvn.at[1]).astype(bf16)
        for kvh in range(2):
            parts = []
            for i in range(2):
                parts += _split_heads(rows(qs.at[2 * kvh + i]), lo, kvh)
            q4 = jnp.concatenate(parts, axis=0).astype(bf16)
            o = _softmax_pv(q4, [kc, kn], [vc, vn], [tab1_ref[kvh], tab2_ref[kvh]], sink_ref[kvh])
            r = SUBLANES * tt
            for i in range(2):
                slab = _merge_heads(o[2 * i * r:(2 * i + 1) * r], o[(2 * i + 1) * r:(2 * i + 2) * r], lo, kvh)
                for t in range(tt):
                    atts[2 * kvh + i, t * nb + b0:t * nb + b0 + SUBLANES, :] = slab[SUBLANES * t:SUBLANES * (t + 1)]
    for b in range(nb):
        ko_ref[b, 0:WINDOW - tt, :] = kc_ref[b, tt:WINDOW, :]
        vo_ref[b, 0:WINDOW - tt, :] = vc_ref[b, tt:WINDOW, :]
        ko_ref[b, WINDOW - tt:WINDOW, :] = kvn.at[0][pl.ds(b, tt, stride=nb), :]
        vo_ref[b, WINDOW - tt:WINDOW, :] = kvn.at[1][pl.ds(b, tt, stride=nb), :]


def _s5_block(ub, xs, hst, bbh_ref, cch_ref, ab_ref, *, nb, tt):
    ub16 = ub.astype(bf16)
    ys = []
    cw = HALF // 2
    for hf in range(2):
        xs[...] = jnp.dot(ub16[:, 256 * hf:256 * (hf + 1)], bbh_ref[hf], preferred_element_type=f32)
        for ck in range(2):
            re0, im0 = cw * ck, HALF + cw * ck
            ar = jnp.broadcast_to(ab_ref[2 * hf:2 * hf + 1, re0:re0 + cw], (SUBLANES, cw))
            ai = jnp.broadcast_to(ab_ref[2 * hf + 1:2 * hf + 2, re0:re0 + cw], (SUBLANES, cw))
            for bg in range(nb // SUBLANES):
                b0 = SUBLANES * bg

                def step(t, c, b0=b0, re0=re0, im0=im0, ar=ar, ai=ai):
                    hr, hi = c
                    r0 = pl.multiple_of(t * nb + b0, SUBLANES)
                    xr = xs[pl.ds(r0, SUBLANES), re0:re0 + cw]
                    xi = xs[pl.ds(r0, SUBLANES), im0:im0 + cw]
                    nr = ar * hr - ai * hi + xr
                    ni = ar * hi + ai * hr + xi
                    xs[pl.ds(r0, SUBLANES), re0:re0 + cw] = nr
                    xs[pl.ds(r0, SUBLANES), im0:im0 + cw] = ni
                    return nr, ni

                h0 = (hst[hf, b0:b0 + SUBLANES, re0:re0 + cw], hst[hf, b0:b0 + SUBLANES, im0:im0 + cw])
                hr, hi = lax.fori_loop(0, tt, step, h0)
                hst[hf, b0:b0 + SUBLANES, re0:re0 + cw] = hr
                hst[hf, b0:b0 + SUBLANES, im0:im0 + cw] = hi
        ys.append(jnp.dot(xs[...].astype(bf16), cch_ref[hf], preferred_element_type=f32))
    return jnp.concatenate(ys, axis=1)


def _lru_block(xcs, la, lb, lst, wlru_ref, vec_ref, *, nb, tt):
    m = nb * tt
    conv = vec_ref[V_CONVB:V_CONVB + 1, :] + vec_ref[V_CONVW:V_CONVW + 1, :] * xcs[0:m, :]
    for j in range(1, CONV_WIDTH):
        conv = conv + vec_ref[V_CONVW + j:V_CONVW + j + 1, :] * xcs[j * nb:j * nb + m, :]
    c16 = conv.astype(bf16)

    def block_dot(g):
        return jnp.concatenate(
            [jnp.dot(c16[:, 256 * hf:256 * (hf + 1)], wlru_ref[g, hf], preferred_element_type=f32)
             for hf in range(2)], axis=1)

    r = jax.nn.sigmoid(block_dot(0) + vec_ref[V_BA:V_BA + 1, :])
    i = jax.nn.sigmoid(block_dot(1) + vec_ref[V_BX:V_BX + 1, :])
    log_a = LRU_C * r * jax.nn.log_sigmoid(vec_ref[V_LAM:V_LAM + 1, :])
    a = jnp.exp(log_a)
    la[...] = a
    lb[...] = conv * i * jnp.sqrt(-jnp.tanh(log_a) * (a * a + 1.0))
    for bg in range(nb // SUBLANES):
        b0 = SUBLANES * bg

        def lru_step(t, h, b0=b0):
            r0 = pl.multiple_of(t * nb + b0, SUBLANES)
            h = la[pl.ds(r0, SUBLANES), :] * h + lb[pl.ds(r0, SUBLANES), :]
            lb[pl.ds(r0, SUBLANES), :] = h
            return h

        lst[b0:b0 + SUBLANES, :] = lax.fori_loop(0, tt, lru_step, lst[b0:b0 + SUBLANES, :])


def _mixer_kernel(*refs, nb, tt, nts, sample):
    if sample:
        (x_ref, wa_ref, tab_ref, tab2_ref, sink_ref, bbh_ref, cch_ref, ab_ref, wglu_ref, wlru_ref, vec_ref,
         kc_ref, vc_ref, ssm0_ref, lru0_ref, conv0_ref,
         z_ref, ko_ref, vo_ref, ssmo_ref, lruo_ref, convo_ref,
         qs, kvs, atts, xs, hst, xcs, la, lb, lst) = refs
    else:
        (x_ref, wa_ref, tab_ref, sink_ref, bbh_ref, cch_ref, ab_ref, wglu_ref, wlru_ref, vec_ref,
         z_ref, ko_ref, vo_ref, ssmo_ref, lruo_ref, convo_ref,
         qs, kvs, atts, xs, hst, xcs, la, lb, lst, tabm, sc_scr, p_scr) = refs
    ts = pl.program_id(1)
    m = nb * tt
    w = WINDOW * nb
    last = ts == nts - 1

    @pl.when(ts == 0)
    def _load_state():
        if sample:
            hst[...] = ssm0_ref[0]
            lst[...] = lru0_ref[0]
            xcs[0:3 * nb, :] = conv0_ref[0]
        else:
            kvs[:, 0:w, :] = jnp.zeros((2, w, LANES), f32)
            hst[...] = jnp.zeros(hst.shape, f32)
            lst[...] = jnp.zeros(lst.shape, f32)
            xcs[0:3 * nb, :] = jnp.zeros((3 * nb, BRANCH), f32)

    xb = x_ref[...].astype(bf16)

    def proj(cols):
        return jnp.dot(xb, wa_ref[:, cols[0]:cols[1]], preferred_element_type=f32)

    q = proj(C_Q)
    for j in range(4):
        qs[j] = q[:, LANES * j:LANES * (j + 1)]
    if sample:
        kvs[0] = proj(C_K)
        kvs[1] = proj(C_V)
        _attend_sample(qs, kvs, atts, kc_ref, vc_ref, tab_ref, tab2_ref, sink_ref, ko_ref, vo_ref, nb=nb, tt=tt)
    else:
        kvs[0, w:w + m, :] = proj(C_K)
        kvs[1, w:w + m, :] = proj(C_V)
        kj = lax.broadcasted_iota(jnp.int32, (2, 4 * tt, WINDOW + tt), 2)
        tabm[...] = jnp.where(kj + ts * tt >= WINDOW, tab_ref[...], NEG)
        _attend_prompt(last, qs, kvs, atts, tabm, sink_ref, ko_ref, vo_ref, sc_scr, p_scr, nb=nb, tt=tt)
        if nts > 1:
            kvs[:, 0:w, :] = kvs[:, m:m + w, :]
    att = jnp.concatenate([atts[j] for j in range(4)], axis=1)
    z_ref[:, 0:BRANCH] = (att * jax.nn.silu(proj(C_GA))).astype(bf16)

    ub = proj(C_UB)
    y = _s5_block(ub, xs, hst, bbh_ref, cch_ref, ab_ref, nb=nb, tt=tt) + vec_ref[V_D:V_D + 1, :] * ub
    zz = jax.nn.gelu(y)
    zz = zz * jax.nn.sigmoid(jnp.dot(zz.astype(bf16), wglu_ref[...], preferred_element_type=f32))
    z_ref[:, BRANCH:2 * BRANCH] = (zz * jax.nn.silu(proj(C_GB))).astype(bf16)

    xcs[3 * nb:3 * nb + m, :] = proj(C_XC)
    _lru_block(xcs, la, lb, lst, wlru_ref, vec_ref, nb=nb, tt=tt)
    z_ref[:, 2 * BRANCH:3 * BRANCH] = (lb[...] * jax.nn.silu(proj(C_GC))).astype(bf16)

    @pl.when(last)
    def _store_state():
        ssmo_ref[0] = hst[...]
        lruo_ref[0] = lst[...]
        convo_ref[0] = xcs[m:m + 3 * nb, :]

    if nts > 1:
        xcs[0:3 * nb, :] = xcs[m:m + 3 * nb, :]


def _out_kernel(x_ref, z_ref, wgm_ref, wbr_ref, wout_ref, ln_ref, y_ref, *, alpha):
    x = x_ref[...]
    xb = x.astype(bf16)
    merged = None
    for k in range(3):
        g = jax.nn.sigmoid(jnp.dot(xb, wgm_ref[:, D_MODEL * k:D_MODEL * (k + 1)], preferred_element_type=f32))
        yk = jnp.dot(z_ref[:, BRANCH * k:BRANCH * (k + 1)], wbr_ref[k], preferred_element_type=f32)
        merged = g * yk if merged is None else merged + g * yk
    out = jnp.dot(merged.astype(bf16), wout_ref[...], preferred_element_type=f32)
    h = alpha * x + out
    mu = jnp.mean(h, axis=-1, keepdims=True)
    d = h - mu
    var = jnp.mean(d * d, axis=-1, keepdims=True)
    y_ref[...] = d * lax.rsqrt(var + LN_EPS) * ln_ref[0:1, :] + ln_ref[1:2, :]


def _const_spec(shape):
    return pl.BlockSpec(shape, lambda *_: (0,) * len(shape), pipeline_mode=pl.Buffered(1))


def _mixer_call(x2d, lp, tabs, state, *, nbg, nts, nb, tt, sample, name):
    m, w, s_len = nb * tt, WINDOW * nb, WINDOW + tt
    kern = functools.partial(_mixer_kernel, nb=nb, tt=tt, nts=nts, sample=sample)
    weights = [lp["bbh"], lp["cch"], lp["ab"], lp["wglu"], lp["wlru"], lp["vec"]]
    weight_specs = [
        _const_spec((2, 256, 2 * HALF)),
        _const_spec((2, 2 * HALF, 256)),
        _const_spec((4, HALF)),
        _const_spec((BRANCH, BRANCH)),
        _const_spec((2, 2, 256, 256)),
        _const_spec((16, BRANCH)),
    ]
    x_spec = pl.BlockSpec((m, D_MODEL), lambda g, t: (g * nts + t, 0))
    kv_spec = pl.BlockSpec((nb, WINDOW, LANES), lambda g, t: (g, 0, 0))
    ssm_spec = pl.BlockSpec((1, 2, nb, 2 * HALF), lambda g, t: (g, 0, 0, 0))
    lru_spec = pl.BlockSpec((1, nb, BRANCH), lambda g, t: (g, 0, 0))
    conv_spec = pl.BlockSpec((1, 3 * nb, BRANCH), lambda g, t: (g, 0, 0))
    if sample:
        sink = lp["sinkcol"]
        args = [x2d, lp["wa"], tabs[0], tabs[1], sink, *weights, *state]
        in_specs = [x_spec, _const_spec((D_MODEL, MIX_COLS)), _const_spec(tabs[0].shape),
                    _const_spec(tabs[1].shape), _const_spec(sink.shape), *weight_specs,
                    kv_spec, kv_spec, ssm_spec, lru_spec, conv_spec]
    else:
        args = [x2d, lp["wa"], tabs[0], lp["sinks"], *weights]
        in_specs = [x_spec, _const_spec((D_MODEL, MIX_COLS)), _const_spec(tabs[0].shape),
                    pl.BlockSpec(memory_space=pltpu.SMEM), *weight_specs]
    out_specs = [pl.BlockSpec((m, 3 * BRANCH), lambda g, t: (g * nts + t, 0)),
                 kv_spec, kv_spec, ssm_spec, lru_spec, conv_spec]
    out_shape = [
        jax.ShapeDtypeStruct((nbg * nts * m, 3 * BRANCH), bf16),
        jax.ShapeDtypeStruct((nbg * nb, WINDOW, LANES), f32),
        jax.ShapeDtypeStruct((nbg * nb, WINDOW, LANES), f32),
        jax.ShapeDtypeStruct((nbg, 2, nb, 2 * HALF), f32),
        jax.ShapeDtypeStruct((nbg, nb, BRANCH), f32),
        jax.ShapeDtypeStruct((nbg, 3 * nb, BRANCH), f32),
    ]
    scratch = [
        pltpu.VMEM((4, m, LANES), f32),
        pltpu.VMEM((2, m, LANES) if sample else (2, w + m, LANES), f32),
        pltpu.VMEM((4, m, LANES), f32),
        pltpu.VMEM((m, 2 * HALF), f32),
        pltpu.VMEM((2, nb, 2 * HALF), f32),
        pltpu.VMEM((m + 3 * nb, BRANCH), f32),
        pltpu.VMEM((m, BRANCH), f32),
        pltpu.VMEM((m, BRANCH), f32),
        pltpu.VMEM((nb, BRANCH), f32),
    ]
    if not sample:
        scratch += [
            pltpu.VMEM((2, 4 * tt, s_len), f32),
            pltpu.VMEM((nb, 2, 4 * tt, s_len), f32),
            pltpu.VMEM((nb, 2, 4 * tt, s_len), bf16),
        ]
    return pl.pallas_call(
        kern, grid=(nbg, nts), in_specs=in_specs, out_specs=out_specs, out_shape=out_shape,
        scratch_shapes=scratch, name=name,
        compiler_params=pltpu.CompilerParams(
            dimension_semantics=("arbitrary", "arbitrary"), vmem_limit_bytes=VMEM_LIMIT),
    )(*args)


def _out_call(x2d, z2d, lp, *, mb, alpha, name):
    rows = x2d.shape[0]
    return pl.pallas_call(
        functools.partial(_out_kernel, alpha=alpha),
        grid=(rows // mb,),
        in_specs=[
            pl.BlockSpec((mb, D_MODEL), lambda i: (i, 0)),
            pl.BlockSpec((mb, 3 * BRANCH), lambda i: (i, 0)),
            _const_spec((D_MODEL, 3 * D_MODEL)),
            _const_spec((3, BRANCH, D_MODEL)),
            _const_spec((D_MODEL, D_MODEL)),
            _const_spec((2, D_MODEL)),
        ],
        out_specs=pl.BlockSpec((mb, D_MODEL), lambda i: (i, 0)),
        out_shape=jax.ShapeDtypeStruct((rows, D_MODEL), f32),
        name=name,
        compiler_params=pltpu.CompilerParams(
            dimension_semantics=("arbitrary",), vmem_limit_bytes=VMEM_LIMIT),
    )(x2d, z2d, lp["wgm"], lp["wbr"], lp["wout"], lp["ln"])


def _t5_bucket(dist):
    max_exact = N_BUCKETS // 2
    d = jnp.maximum(dist, 0)
    large = max_exact + (jnp.log(jnp.maximum(d, 1).astype(f32) / max_exact)
                         / math.log(MAX_DISTANCE / max_exact) * (N_BUCKETS - max_exact)).astype(jnp.int32)
    large = jnp.minimum(large, N_BUCKETS - 1)
    return jnp.where(d < max_exact, d, large)


def _band_bias(rel_bias, tt):
    s_len = WINDOW + tt
    dist = jnp.arange(tt)[:, None] + WINDOW - jnp.arange(s_len)[None, :]
    bias = rel_bias.astype(f32)[_t5_bucket(dist)].transpose(2, 0, 1)
    return jnp.where(((dist >= 0) & (dist < WINDOW))[None], bias, NEG)


def _prompt_tables(rel_bias, tt):
    return (_band_bias(rel_bias, tt).reshape(2, 4 * tt, WINDOW + tt),)


def _sample_tables(rel_bias, tt):
    bs = _band_bias(rel_bias, tt).reshape(2, 4, tt, WINDOW + tt)
    same = jnp.eye(SUBLANES, dtype=bool)
    t1 = jnp.where(same[None, None, None, :, :, None], bs[:, :, :, None, None, :WINDOW], NEG)
    t2 = jnp.where(same[None, None, None, :, None, :], bs[:, :, :, None, WINDOW:, None], NEG)
    rows = 4 * tt * SUBLANES
    return t1.reshape(2, rows, SUBLANES * WINDOW), t2.reshape(2, rows, tt * SUBLANES)


def _block_diag(blocks):
    n, r, c = blocks.shape
    eye = jnp.eye(n, dtype=blocks.dtype)
    return jnp.einsum("nrc,nk->nrkc", blocks, eye).reshape(n * r, n * c)


def _layer_params(l, p_tt, s_tt, w_in, sinks, w_branch_a, lam_re, lam_im, log_step, b_re, b_im, c_re, c_im,
                  ssm_d, w_glu, w_branch_b, conv_w, conv_b, w_a, b_a, w_x, b_x, lam, w_branch_c, w_out,
                  ln_g, ln_b):
    lr = jnp.minimum(lam_re[l].astype(f32), -1e-4)
    li = lam_im[l].astype(f32)
    step = jnp.exp(log_step[l].astype(f32))[:, None]
    mag = jnp.exp(lr * step)
    ab_re = mag * jnp.cos(li * step)
    ab_im = mag * jnp.sin(li * step)
    den = lr * lr + li * li
    f_re = ((ab_re - 1.0) * lr + ab_im * li) / den
    f_im = (ab_im * lr - (ab_re - 1.0) * li) / den
    bre, bim = b_re[l].astype(f32), b_im[l].astype(f32)
    bb_re = f_re[..., None] * bre - f_im[..., None] * bim
    bb_im = f_re[..., None] * bim + f_im[..., None] * bre
    hg = SSM_GROUPS // 2

    def drive(bb):
        t = bb.transpose(0, 2, 1).reshape(2, hg, SSM_GROUP, SSM_STATE)
        return jnp.stack([_block_diag(t[0]), _block_diag(t[1])])

    def readout(cc):
        t = cc.astype(f32).transpose(0, 2, 1).reshape(2, hg, SSM_STATE, SSM_GROUP)
        return jnp.stack([_block_diag(t[0]), _block_diag(t[1])])

    bbh = jnp.concatenate([drive(bb_re), drive(bb_im)], axis=2).astype(bf16)
    cch = jnp.concatenate([readout(c_re[l]), -readout(c_im[l])], axis=1).astype(bf16)
    ab = jnp.stack([ab_re.reshape(2, HALF)[0], ab_im.reshape(2, HALF)[0],
                    ab_re.reshape(2, HALF)[1], ab_im.reshape(2, HALF)[1]])

    def lru_blocks(wt):
        t = wt.astype(f32).reshape(2, LRU_BLOCKS // 2, 64, 64)
        return jnp.stack([_block_diag(t[0]), _block_diag(t[1])])

    vec = jnp.zeros((16, BRANCH), f32)
    vec = vec.at[V_CONVW:V_CONVW + CONV_WIDTH].set(conv_w[l].astype(f32))
    vec = vec.at[V_CONVB].set(conv_b[l].astype(f32))
    vec = vec.at[V_BA].set(b_a[l].astype(f32))
    vec = vec.at[V_BX].set(b_x[l].astype(f32))
    vec = vec.at[V_LAM].set(lam[l].astype(f32))
    vec = vec.at[V_D].set(ssm_d[l].astype(f32))
    sk = sinks[l].astype(f32).reshape(2, 4, 1)
    return {
        "wa": w_in[l][:, :MIX_COLS].astype(bf16),
        "wgm": w_in[l][:, MIX_COLS:].astype(bf16),
        "sinks": sinks[l].astype(f32),
        "sinkcol": jnp.broadcast_to(sk, (2, 4, s_tt * SUBLANES)).reshape(2, 4 * s_tt * SUBLANES, 1),
        "bbh": bbh, "cch": cch, "ab": ab,
        "wglu": w_glu[l].astype(bf16),
        "wlru": jnp.stack([lru_blocks(w_a[l]), lru_blocks(w_x[l])]).astype(bf16),
        "vec": vec,
        "wbr": jnp.stack([w_branch_a[l], w_branch_b[l], w_branch_c[l]]).astype(bf16),
        "wout": w_out[l].astype(bf16),
        "ln": jnp.stack([ln_g[l], ln_b[l]]).astype(f32),
    }


def _group_rows(x, nbg, nb):
    b, t, c = x.shape
    return x.reshape(nbg, nb, t, c).transpose(0, 2, 1, 3).reshape(nbg * t * nb, c)


def _ungroup_rows(x2d, nbg, nb, t):
    c = x2d.shape[-1]
    return x2d.reshape(nbg, t, nb, c).transpose(0, 2, 1, 3).reshape(nbg * nb, t, c)


def _state_in(nbg, nb, cache_k, cache_v, ssm_re, ssm_im, lru, conv):
    def ssm(s):
        return s.astype(f32).reshape(nbg, nb, 2, HALF).transpose(0, 2, 1, 3)

    ssm0 = jnp.concatenate([ssm(ssm_re), ssm(ssm_im)], axis=-1)
    lru0 = lru.astype(f32).reshape(nbg, nb, BRANCH)
    conv0 = conv.astype(f32).reshape(nbg, nb, 3, BRANCH).transpose(0, 2, 1, 3).reshape(nbg, 3 * nb, BRANCH)
    return (cache_k.astype(f32).reshape(nbg * nb, WINDOW, LANES), cache_v.astype(f32).reshape(nbg * nb, WINDOW, LANES),
            ssm0, lru0, conv0)


def _state_out(nbg, nb, ko, vo, ssmo, lruo, convo):
    def ssm(s):
        return s.transpose(0, 2, 1, 3).reshape(nbg * nb, SSM_GROUPS, SSM_STATE)

    return (ko.reshape(nbg * nb, WINDOW, 2, HEAD_DIM), vo.reshape(nbg * nb, WINDOW, 2, HEAD_DIM),
            ssm(ssmo[..., :HALF]), ssm(ssmo[..., HALF:]),
            lruo.reshape(nbg * nb, BRANCH),
            convo.reshape(nbg, 3, nb, BRANCH).transpose(0, 2, 1, 3).reshape(nbg * nb, 3, BRANCH))


P_TT = 64
S_NBG = 4
OUT_ROWS = 512


def kernel(x_prompt, x_sample, cache_k, cache_v, state_ssm_re, state_ssm_im, state_lru, state_conv, rel_bias, w_in, sinks, w_branch_a, ssm_lambda_re, ssm_lambda_im, ssm_log_step, ssm_b_re, ssm_b_im, ssm_c_re, ssm_c_im, ssm_d, ssm_w_glu, w_branch_b, conv_w, conv_b, lru_w_a, lru_b_a, lru_w_x, lru_b_x, lru_lambda, w_branch_c, w_out, ln_g, ln_b):
    depth = w_in.shape[0]
    alpha = (2 * depth) ** 0.25
    pb, pt, _ = x_prompt.shape
    sb, st, _ = x_sample.shape
    s_nb = sb // S_NBG
    p_nts = pt // P_TT

    xp = _group_rows(x_prompt, 1, pb)
    xs = _group_rows(x_sample, S_NBG, s_nb)
    p_tabs = _prompt_tables(rel_bias, P_TT)
    s_tabs = _sample_tables(rel_bias, st)
    st_p, st_s = [], []
    for l in range(depth):
        lp = _layer_params(l, P_TT, st, w_in, sinks, w_branch_a, ssm_lambda_re, ssm_lambda_im,
                           ssm_log_step, ssm_b_re, ssm_b_im, ssm_c_re, ssm_c_im, ssm_d, ssm_w_glu,
                           w_branch_b, conv_w, conv_b, lru_w_a, lru_b_a, lru_w_x, lru_b_x, lru_lambda,
                           w_branch_c, w_out, ln_g, ln_b)
        zp, *sp = _mixer_call(xp, lp, p_tabs, None, nbg=1, nts=p_nts, nb=pb, tt=P_TT, sample=False,
                              name=f"mixer_prompt_{l}")
        xp = _out_call(xp, zp, lp, mb=OUT_ROWS, alpha=alpha, name=f"out_prompt_{l}")
        st_p.append(_state_out(1, pb, *sp))
        s_in = _state_in(S_NBG, s_nb, cache_k[l], cache_v[l], state_ssm_re[l], state_ssm_im[l],
                         state_lru[l], state_conv[l])
        zs, *ss = _mixer_call(xs, lp, s_tabs, s_in, nbg=S_NBG, nts=1, nb=s_nb, tt=st, sample=True,
                              name=f"mixer_sample_{l}")
        xs = _out_call(xs, zs, lp, mb=OUT_ROWS, alpha=alpha, name=f"out_sample_{l}")
        st_s.append(_state_out(S_NBG, s_nb, *ss))

    y_prompt = _ungroup_rows(xp, 1, pb, pt)
    y_sample = _ungroup_rows(xs, S_NBG, s_nb, st)
    stacked_p = [jnp.stack([s[k] for s in st_p]) for k in range(6)]
    stacked_s = [jnp.stack([s[k] for s in st_s]) for k in range(6)]
    return (y_prompt, y_sample, *stacked_p, *stacked_s)
```

```python
import functools
import math

import jax
import jax.numpy as jnp
from jax import lax
from jax.experimental import pallas as pl
from jax.experimental.pallas import tpu as pltpu

f32 = jnp.float32
bf16 = jnp.bfloat16

D_MODEL = 1024
BRANCH = 512
HEAD_DIM = 64
N_Q_HEADS = 8
WINDOW = 128
N_BUCKETS = 32
MAX_DISTANCE = 128
SSM_GROUPS = 32
SSM_GROUP = 16
SSM_STATE = 64
LRU_BLOCKS = 8
CONV_WIDTH = 4
LRU_C = 8.0
LN_EPS = 1e-5
NEG = -1e30

LANES = 128
SUBLANES = 8
VMEM_LIMIT = 56 * 1024 * 1024

C_Q, C_K, C_V, C_GA, C_UB, C_GB, C_XC, C_GC, C_GM = (
    (0, 512), (512, 640), (640, 768), (768, 1280), (1280, 1792), (1792, 2304),
    (2304, 2816), (2816, 3328), (3328, 6400))
IN_COLS = C_GM[1]
HALF = SSM_GROUPS // 2 * SSM_STATE
SSM_WIDTH = 2 * HALF
N_SLABS = D_MODEL // LANES

V_CONVW, V_CONVB, V_BA, V_BX, V_LAM, V_D = 0, 4, 5, 6, 7, 8


def _split_heads(qa, lo, kvh):
    q_lo = jnp.where(lo, qa, 0.0)
    q_hi = jnp.where(lo, 0.0, qa)
    if kvh == 0:
        return [q_lo, pltpu.roll(q_hi, HEAD_DIM, axis=1)]
    return [pltpu.roll(q_lo, HEAD_DIM, axis=1), q_hi]


def _merge_heads(o0, o1, lo, kvh):
    if kvh == 0:
        return jnp.where(lo, o0, pltpu.roll(o1, HEAD_DIM, axis=1))
    return jnp.where(lo, pltpu.roll(o0, HEAD_DIM, axis=1), o1)


def _softmax_pv(q4, keys, vals, tabs, sink):
    scs = []
    for k, t in zip(keys, tabs):
        sc = lax.dot_general(q4, k, (((1,), (1,)), ((), ())), preferred_element_type=f32)
        scs.append(jnp.where(t > 0.5 * NEG, sc * (HEAD_DIM ** -0.5) + t, NEG))
    mx = sink
    for sc in scs:
        mx = jnp.maximum(mx, jnp.max(sc, axis=-1, keepdims=True))
    den = jnp.exp(sink - mx)
    o = None
    for sc, v in zip(scs, vals):
        p = jnp.exp(sc - mx)
        den = den + jnp.sum(p, axis=-1, keepdims=True)
        pv = jnp.dot(p.astype(bf16), v, preferred_element_type=f32)
        o = pv if o is None else o + pv
    return o / den


def _attend_prompt(last, layer, qs, kvs, atts, tabm, sinks_ref, ko_ref, vo_ref, sc_scr, p_scr, *, nb, tt):
    s_len = WINDOW + tt
    lo = lax.broadcasted_iota(jnp.int32, (tt, LANES), 1) < HEAD_DIM

    for b in range(nb):
        kk16 = kvs.at[0][pl.ds(b, s_len, stride=nb), :].astype(bf16)
        for kvh in range(2):
            parts = []
            for i in range(2):
                parts += _split_heads(qs.at[2 * kvh + i][pl.ds(b, tt, stride=nb), :], lo, kvh)
            q4 = jnp.concatenate(parts, axis=0).astype(bf16)
            sc_scr[b, kvh] = lax.dot_general(q4, kk16, (((1,), (1,)), ((), ())), preferred_element_type=f32)

    for b in range(nb):
        for kvh in range(2):
            for g in range(4):
                rs = slice(g * tt, (g + 1) * tt)
                t = tabm[kvh, rs, :]
                sc = jnp.where(t > 0.5 * NEG, sc_scr[b, kvh, rs, :] * (HEAD_DIM ** -0.5) + t, NEG)
                sink = sinks_ref[layer, 4 * kvh + g]
                mx = jnp.maximum(jnp.max(sc, axis=-1, keepdims=True), sink)
                p = jnp.exp(sc - mx)
                den = jnp.sum(p, axis=-1, keepdims=True) + jnp.exp(sink - mx)
                p_scr[b, kvh, rs, :] = (p / den).astype(bf16)

    for b in range(nb):
        vv16 = kvs.at[1][pl.ds(b, s_len, stride=nb), :].astype(bf16)
        for kvh in range(2):
            o = jnp.dot(p_scr[b, kvh], vv16, preferred_element_type=f32)
            for i in range(2):
                slab = _merge_heads(o[2 * i * tt:(2 * i + 1) * tt], o[(2 * i + 1) * tt:(2 * i + 2) * tt], lo, kvh)
                atts.at[2 * kvh + i][pl.ds(b, tt, stride=nb), :] = slab

    @pl.when(last)
    def _store_cache():
        for b in range(nb):
            ko_ref[b] = kvs.at[0][pl.ds(tt * nb + b, WINDOW, stride=nb), :]
            vo_ref[b] = kvs.at[1][pl.ds(tt * nb + b, WINDOW, stride=nb), :]


def _attend_sample(qs, kvn, atts, kc_ref, vc_ref, tab1_ref, tab2_ref, sink_ref, ko_ref, vo_ref, *, nb, tt):
    lo = lax.broadcasted_iota(jnp.int32, (SUBLANES * tt, LANES), 1) < HEAD_DIM
    for sg in range(nb // SUBLANES):
        b0 = SUBLANES * sg

        def rows(ref, b0=b0):
            return jnp.concatenate([ref[t * nb + b0:t * nb + b0 + SUBLANES, :] for t in range(tt)], axis=0)

        kc = kc_ref[b0:b0 + SUBLANES].reshape(SUBLANES * WINDOW, LANES).astype(bf16)
        vc = vc_ref[b0:b0 + SUBLANES].reshape(SUBLANES * WINDOW, LANES).astype(bf16)
        kn = rows(kvn.at[0]).astype(bf16)
        vn = rows(kvn.at[1]).astype(bf16)
        for kvh in range(2):
            parts = []
            for i in range(2):
                parts += _split_heads(rows(qs.at[2 * kvh + i]), lo, kvh)
            q4 = jnp.concatenate(parts, axis=0).astype(bf16)
            o = _softmax_pv(q4, [kc, kn], [vc, vn], [tab1_ref[kvh], tab2_ref[kvh]], sink_ref[kvh])
            r = SUBLANES * tt
            for i in range(2):
                slab = _merge_heads(o[2 * i * r:(2 * i + 1) * r], o[(2 * i + 1) * r:(2 * i + 2) * r], lo, kvh)
                for t in range(tt):
                    atts[2 * kvh + i, t * nb + b0:t * nb + b0 + SUBLANES, :] = slab[SUBLANES * t:SUBLANES * (t + 1)]
    for b in range(nb):
        ko_ref[b, 0:WINDOW - tt, :] = kc_ref[b, tt:WINDOW, :]
        vo_ref[b, 0:WINDOW - tt, :] = vc_ref[b, tt:WINDOW, :]
        ko_ref[b, WINDOW - tt:WINDOW, :] = kvn.at[0][pl.ds(b, tt, stride=nb), :]
        vo_ref[b, WINDOW - tt:WINDOW, :] = kvn.at[1][pl.ds(b, tt, stride=nb), :]


def _s5_block(ub, xs, hst, bbh_ref, cch_ref, ab_ref, *, nb, tt):
    ub16 = ub.astype(bf16)
    ys = []
    cw = HALF // 2
    for hf in range(2):
        xs[...] = jnp.dot(ub16[:, 256 * hf:256 * (hf + 1)], bbh_ref[hf], preferred_element_type=f32)
        for ck in range(2):
            re0, im0, st0 = cw * ck, HALF + cw * ck, HALF * hf + cw * ck
            ar = jnp.broadcast_to(ab_ref[2 * hf:2 * hf + 1, re0:re0 + cw], (SUBLANES, cw))
            ai = jnp.broadcast_to(ab_ref[2 * hf + 1:2 * hf + 2, re0:re0 + cw], (SUBLANES, cw))
            for bg in range(nb // SUBLANES):
                b0 = SUBLANES * bg

                def step(t, c, b0=b0, re0=re0, im0=im0, ar=ar, ai=ai):
                    hr, hi = c
                    r0 = pl.multiple_of(t * nb + b0, SUBLANES)
                    xr = xs[pl.ds(r0, SUBLANES), re0:re0 + cw]
                    xi = xs[pl.ds(r0, SUBLANES), im0:im0 + cw]
                    nr = ar * hr - ai * hi + xr
                    ni = ar * hi + ai * hr + xi
                    xs[pl.ds(r0, SUBLANES), re0:re0 + cw] = nr
                    xs[pl.ds(r0, SUBLANES), im0:im0 + cw] = ni
                    return nr, ni

                h0 = (hst[0, b0:b0 + SUBLANES, st0:st0 + cw], hst[1, b0:b0 + SUBLANES, st0:st0 + cw])
                hr, hi = lax.fori_loop(0, tt, step, h0)
                hst[0, b0:b0 + SUBLANES, st0:st0 + cw] = hr
                hst[1, b0:b0 + SUBLANES, st0:st0 + cw] = hi
        ys.append(jnp.dot(xs[...].astype(bf16), cch_ref[hf], preferred_element_type=f32))
    return jnp.concatenate(ys, axis=1)


def _lru_block(xcs, la, lb, lst, wlru_ref, vec_ref, *, nb, tt):
    m = nb * tt
    conv = vec_ref[V_CONVB:V_CONVB + 1, :] + vec_ref[V_CONVW:V_CONVW + 1, :] * xcs[0:m, :]
    for j in range(1, CONV_WIDTH):
        conv = conv + vec_ref[V_CONVW + j:V_CONVW + j + 1, :] * xcs[j * nb:j * nb + m, :]
    c16 = conv.astype(bf16)

    def block_dot(g):
        return jnp.concatenate(
            [jnp.dot(c16[:, 256 * hf:256 * (hf + 1)], wlru_ref[g, hf], preferred_element_type=f32)
             for hf in range(2)], axis=1)

    r = jax.nn.sigmoid(block_dot(0) + vec_ref[V_BA:V_BA + 1, :])
    i = jax.nn.sigmoid(block_dot(1) + vec_ref[V_BX:V_BX + 1, :])
    log_a = LRU_C * r * jax.nn.log_sigmoid(vec_ref[V_LAM:V_LAM + 1, :])
    a = jnp.exp(log_a)
    la[...] = a
    lb[...] = conv * i * jnp.sqrt(-jnp.tanh(log_a) * (a * a + 1.0))
    for bg in range(nb // SUBLANES):
        b0 = SUBLANES * bg

        def lru_step(t, h, b0=b0):
            r0 = pl.multiple_of(t * nb + b0, SUBLANES)
            h = la[pl.ds(r0, SUBLANES), :] * h + lb[pl.ds(r0, SUBLANES), :]
            lb[pl.ds(r0, SUBLANES), :] = h
            return h

        lst[b0:b0 + SUBLANES, :] = lax.fori_loop(0, tt, lru_step, lst[b0:b0 + SUBLANES, :])


def _mixer_tail(xb, win_ref, atts, xs, hst, xcs, la, lb, lst, bbh_ref, cch_ref, ab_ref, wglu_ref, wlru_ref,
                vec_ref, *, nb, tt):
    m = nb * tt

    def proj(cols):
        return jnp.dot(xb, win_ref[:, cols[0]:cols[1]], preferred_element_type=f32)

    att = jnp.concatenate([atts[j] for j in range(4)], axis=1)
    za = (att * jax.nn.silu(proj(C_GA))).astype(bf16)

    ub = proj(C_UB)
    y = _s5_block(ub, xs, hst, bbh_ref, cch_ref, ab_ref, nb=nb, tt=tt) + vec_ref[V_D:V_D + 1, :] * ub
    zz = jax.nn.gelu(y)
    zz = zz * jax.nn.sigmoid(jnp.dot(zz.astype(bf16), wglu_ref[...], preferred_element_type=f32))
    zb = (zz * jax.nn.silu(proj(C_GB))).astype(bf16)

    xcs[3 * nb:3 * nb + m, :] = proj(C_XC)
    _lru_block(xcs, la, lb, lst, wlru_ref, vec_ref, nb=nb, tt=tt)
    zc = (lb[...] * jax.nn.silu(proj(C_GC))).astype(bf16)
    return za, zb, zc


def _out_core(x, zs, win_ref, wbr_refs, wout_ref, ln_g, ln_b, alpha):
    xb = x.astype(bf16)
    merged = None
    for k in range(3):
        c0 = C_GM[0] + D_MODEL * k
        g = jax.nn.sigmoid(jnp.dot(xb, win_ref[:, c0:c0 + D_MODEL], preferred_element_type=f32))
        yk = jnp.dot(zs[k], wbr_refs[k][...], preferred_element_type=f32)
        merged = g * yk if merged is None else merged + g * yk
    out = jnp.dot(merged.astype(bf16), wout_ref[...], preferred_element_type=f32)
    h = alpha * x + out
    mu = jnp.mean(h, axis=-1, keepdims=True)
    d = h - mu
    var = jnp.mean(d * d, axis=-1, keepdims=True)
    return d * lax.rsqrt(var + LN_EPS) * ln_g + ln_b


def _prompt_mixer_kernel(*refs, layer, nb, tt, nts, natural_in):
    (x_ref, win_ref, tab_ref, sinks_ref, bbh_ref, cch_ref, ab_ref, wglu_ref, wlru_ref, vec_ref), refs = refs[:10], refs[10:]
    z_ref, refs = refs[0], refs[1:]
    if natural_in:
        xtb_ref, refs = refs[0], refs[1:]
    (ko_ref, vo_ref, sre_ref, sim_ref, lruo_ref, convo_ref,
     qs, kvs, atts, xs, hst, xcs, la, lb, lst, tabm, sc_scr, p_scr), refs = refs[:18], refs[18:]
    ts = pl.program_id(0)
    m = nb * tt
    w = WINDOW * nb
    last = ts == nts - 1

    @pl.when(ts == 0)
    def _zero_state():
        kvs[:, 0:w, :] = jnp.zeros((2, w, LANES), f32)
        hst[...] = jnp.zeros(hst.shape, f32)
        lst[...] = jnp.zeros(lst.shape, f32)
        xcs[0:3 * nb, :] = jnp.zeros((3 * nb, BRANCH), f32)

    if natural_in:
        (xsl,) = refs
        for b in range(nb):
            for k in range(N_SLABS):
                xsl.at[k][pl.ds(b, tt, stride=nb), :] = x_ref[b, :, LANES * k:LANES * (k + 1)]
        x = jnp.concatenate([xsl[k] for k in range(N_SLABS)], axis=1)
        xtb_ref[...] = x
    else:
        x = x_ref[...]
    xb = x.astype(bf16)

    def proj(cols):
        return jnp.dot(xb, win_ref[:, cols[0]:cols[1]], preferred_element_type=f32)

    q = proj(C_Q)
    for j in range(4):
        qs[j] = q[:, LANES * j:LANES * (j + 1)]
    kvs[0, w:w + m, :] = proj(C_K)
    kvs[1, w:w + m, :] = proj(C_V)
    kj = lax.broadcasted_iota(jnp.int32, (2, 4 * tt, WINDOW + tt), 2)
    tabm[...] = jnp.where(kj + ts * tt >= WINDOW, tab_ref[...], NEG)
    _attend_prompt(last, layer, qs, kvs, atts, tabm, sinks_ref, ko_ref, vo_ref, sc_scr, p_scr, nb=nb, tt=tt)
    kvs[:, 0:w, :] = kvs[:, m:m + w, :]

    za, zb, zc = _mixer_tail(xb, win_ref, atts, xs, hst, xcs, la, lb, lst, bbh_ref, cch_ref, ab_ref, wglu_ref,
                             wlru_ref, vec_ref, nb=nb, tt=tt)
    z_ref[:, 0:BRANCH] = za
    z_ref[:, BRANCH:2 * BRANCH] = zb
    z_ref[:, 2 * BRANCH:3 * BRANCH] = zc

    @pl.when(last)
    def _store_state():
        sre_ref[...] = hst[0]
        sim_ref[...] = hst[1]
        lruo_ref[...] = lst[...]
        for t in range(3):
            convo_ref[:, BRANCH * t:BRANCH * (t + 1)] = xcs[m + t * nb:m + (t + 1) * nb, :]

    xcs[0:3 * nb, :] = xcs[m:m + 3 * nb, :]


def _prompt_out_kernel(*refs, layer, alpha, nb, natural_out):
    if natural_out:
        x_ref, z_ref, win_ref, wa_ref, wb_ref, wc_ref, wout_ref, lng_ref, lnb_ref, y_ref, ysl = refs
    else:
        x_ref, z_ref, win_ref, wa_ref, wb_ref, wc_ref, wout_ref, lng_ref, lnb_ref, y_ref = refs
    zs = [z_ref[:, BRANCH * k:BRANCH * (k + 1)] for k in range(3)]
    y = _out_core(x_ref[...], zs, win_ref, (wa_ref, wb_ref, wc_ref), wout_ref,
                  lng_ref[layer:layer + 1, :], lnb_ref[layer:layer + 1, :], alpha)
    if natural_out:
        tt = y.shape[0] // nb
        for k in range(N_SLABS):
            ysl[k] = y[:, LANES * k:LANES * (k + 1)]
        for b in range(nb):
            for k in range(N_SLABS):
                y_ref[b, :, LANES * k:LANES * (k + 1)] = ysl.at[k][pl.ds(b, tt, stride=nb), :]
    else:
        y_ref[...] = y


def _sample_kernel(x_ref, win_ref, tab1_ref, tab2_ref, sink_ref, bbh_ref, cch_ref, ab_ref, wglu_ref, wlru_ref,
                   vec_ref, wa_ref, wb_ref, wc_ref, wout_ref, lng_ref, lnb_ref,
                   kc_ref, vc_ref, sre0_ref, sim0_ref, lru0_ref, conv0_ref,
                   y_ref, ko_ref, vo_ref, sre_ref, sim_ref, lruo_ref, convo_ref,
                   xcar, qs, kvn, atts, xs, hst, xcs, la, lb, lst, *, alpha, nb, tt):
    layer = pl.program_id(0)
    g = pl.program_id(1)
    m = nb * tt

    @pl.when(layer == 0)
    def _first_layer_input():
        xcar[g] = x_ref[...]

    x = xcar[g]
    xb = x.astype(bf16)
    hst[0] = sre0_ref[...]
    hst[1] = sim0_ref[...]
    lst[...] = lru0_ref[...]
    for t in range(3):
        xcs[t * nb:(t + 1) * nb, :] = conv0_ref[:, BRANCH * t:BRANCH * (t + 1)]

    def proj(cols):
        return jnp.dot(xb, win_ref[:, cols[0]:cols[1]], preferred_element_type=f32)

    q = proj(C_Q)
    for j in range(4):
        qs[j] = q[:, LANES * j:LANES * (j + 1)]
    kvn[0] = proj(C_K)
    kvn[1] = proj(C_V)
    _attend_sample(qs, kvn, atts, kc_ref, vc_ref, tab1_ref, tab2_ref, sink_ref, ko_ref, vo_ref, nb=nb, tt=tt)
    zs = _mixer_tail(xb, win_ref, atts, xs, hst, xcs, la, lb, lst, bbh_ref, cch_ref, ab_ref, wglu_ref,
                     wlru_ref, vec_ref, nb=nb, tt=tt)
    sre_ref[...] = hst[0]
    sim_ref[...] = hst[1]
    lruo_ref[...] = lst[...]
    for t in range(3):
        convo_ref[:, BRANCH * t:BRANCH * (t + 1)] = xcs[m + t * nb:m + (t + 1) * nb, :]

    y = _out_core(x, zs, win_ref, (wa_ref, wb_ref, wc_ref), wout_ref,
                  lng_ref[pl.ds(layer, 1), :], lnb_ref[pl.ds(layer, 1), :], alpha)
    xcar[g] = y
    y_ref[...] = y


def _layer_spec(shape, layer=None):
    if layer is None:
        idx = lambda l, *_: (l,) + (0,) * len(shape)
    else:
        idx = lambda *_: (layer,) + (0,) * len(shape)
    return pl.BlockSpec((None,) + tuple(shape), idx, pipeline_mode=pl.Buffered(1))


def _const_spec(shape):
    return pl.BlockSpec(tuple(shape), lambda *_: (0,) * len(shape), pipeline_mode=pl.Buffered(1))


def _mixer_weight_specs(layer):
    return [
        _layer_spec((2, 256, SSM_WIDTH), layer),
        _layer_spec((2, SSM_WIDTH, 256), layer),
        _layer_spec((4, HALF), layer),
        _layer_spec((BRANCH, BRANCH), layer),
        _layer_spec((2, 2, 256, 256), layer),
        _layer_spec((16, BRANCH), layer),
    ]


def _mixer_scratch(nb, tt, kv_rows):
    m = nb * tt
    return [
        pltpu.VMEM((4, m, LANES), f32),
        pltpu.VMEM((2, kv_rows, LANES), f32),
        pltpu.VMEM((4, m, LANES), f32),
        pltpu.VMEM((m, SSM_WIDTH), f32),
        pltpu.VMEM((2, nb, SSM_WIDTH), f32),
        pltpu.VMEM((m + 3 * nb, BRANCH), f32),
        pltpu.VMEM((m, BRANCH), f32),
        pltpu.VMEM((m, BRANCH), f32),
        pltpu.VMEM((nb, BRANCH), f32),
    ]


def _prompt_mixer_call(x, wp, tab, layer, *, nb, tt, natural_in):
    nts = (x.shape[1] if natural_in else x.shape[0] // nb) // tt
    m, w, s_len = nb * tt, WINDOW * nb, WINDOW + tt
    kern = functools.partial(_prompt_mixer_kernel, layer=layer, nb=nb, tt=tt, nts=nts, natural_in=natural_in)
    x_spec = (pl.BlockSpec((nb, tt, D_MODEL), lambda t: (0, t, 0)) if natural_in
              else pl.BlockSpec((m, D_MODEL), lambda t: (t, 0)))
    in_specs = [x_spec,
                pl.BlockSpec((None, D_MODEL, C_GM[0]), lambda t: (layer, 0, 0), pipeline_mode=pl.Buffered(1)),
                _const_spec(tab.shape), pl.BlockSpec(memory_space=pltpu.SMEM), *_mixer_weight_specs(layer)]
    whole = lambda shape: pl.BlockSpec(shape, lambda t: (0,) * len(shape))
    out_specs = [pl.BlockSpec((m, 3 * BRANCH), lambda t: (t, 0))]
    out_shape = [jax.ShapeDtypeStruct((nts * m, 3 * BRANCH), bf16)]
    if natural_in:
        out_specs.append(pl.BlockSpec((m, D_MODEL), lambda t: (t, 0)))
        out_shape.append(jax.ShapeDtypeStruct((nts * m, D_MODEL), f32))
    state_shapes = [(nb, WINDOW, LANES), (nb, WINDOW, LANES), (nb, SSM_WIDTH), (nb, SSM_WIDTH),
                    (nb, BRANCH), (nb, 3 * BRANCH)]
    out_specs += [whole(s) for s in state_shapes]
    out_shape += [jax.ShapeDtypeStruct(s, f32) for s in state_shapes]
    scratch = _mixer_scratch(nb, tt, w + m) + [
        pltpu.VMEM((2, 4 * tt, s_len), f32),
        pltpu.VMEM((nb, 2, 4 * tt, s_len), f32),
        pltpu.VMEM((nb, 2, 4 * tt, s_len), bf16),
    ]
    if natural_in:
        scratch.append(pltpu.VMEM((N_SLABS, m, LANES), f32))
    return pl.pallas_call(
        kern, grid=(nts,), in_specs=in_specs, out_specs=out_specs, out_shape=out_shape,
        scratch_shapes=scratch, name=f"mixer_prompt_{layer}",
        compiler_params=pltpu.CompilerParams(dimension_semantics=("arbitrary",), vmem_limit_bytes=VMEM_LIMIT),
    )(x, wp["win"], tab, wp["sinks"], wp["bbh"], wp["cch"], wp["ab"], wp["wglu"], wp["wlru"], wp["vec"])


def _prompt_out_call(x2d, z2d, wp, layer, *, nb, mb, alpha, natural_out):
    rows = x2d.shape[0]
    kern = functools.partial(_prompt_out_kernel, layer=layer, alpha=alpha, nb=nb, natural_out=natural_out)
    if natural_out:
        out_spec = pl.BlockSpec((nb, mb // nb, D_MODEL), lambda i: (0, i, 0))
        out_shape = jax.ShapeDtypeStruct((nb, rows // nb, D_MODEL), f32)
        scratch = [pltpu.VMEM((N_SLABS, mb, LANES), f32)]
    else:
        out_spec = pl.BlockSpec((mb, D_MODEL), lambda i: (i, 0))
        out_shape = jax.ShapeDtypeStruct((rows, D_MODEL), f32)
        scratch = []
    return pl.pallas_call(
        kern, grid=(rows // mb,),
        in_specs=[
            pl.BlockSpec((mb, D_MODEL), lambda i: (i, 0)),
            pl.BlockSpec((mb, 3 * BRANCH), lambda i: (i, 0)),
            _layer_spec((D_MODEL, IN_COLS), layer),
            _layer_spec((BRANCH, D_MODEL), layer), _layer_spec((BRANCH, D_MODEL), layer),
            _layer_spec((BRANCH, D_MODEL), layer),
            _layer_spec((D_MODEL, D_MODEL), layer),
            _const_spec(wp["ln_g"].shape), _const_spec(wp["ln_b"].shape),
        ],
        out_specs=out_spec, out_shape=out_shape, scratch_shapes=scratch, name=f"out_prompt_{layer}",
        compiler_params=pltpu.CompilerParams(dimension_semantics=("arbitrary",), vmem_limit_bytes=VMEM_LIMIT),
    )(x2d, z2d, wp["win"], wp["wbr_a"], wp["wbr_b"], wp["wbr_c"], wp["wout"], wp["ln_g"], wp["ln_b"])


def _sample_call(x2d, wp, tabs, state, *, depth, nbg, nb, tt, alpha):
    m = nb * tt
    kern = functools.partial(_sample_kernel, alpha=alpha, nb=nb, tt=tt)
    row_spec = pl.BlockSpec((m, D_MODEL), lambda l, g: (g, 0))
    st = lambda *shape: pl.BlockSpec((nb,) + shape, lambda l, g: (l * nbg + g,) + (0,) * len(shape))
    state_specs = [st(WINDOW, LANES), st(WINDOW, LANES), st(SSM_WIDTH), st(SSM_WIDTH), st(BRANCH), st(3 * BRANCH)]
    state_shapes = [(WINDOW, LANES), (WINDOW, LANES), (SSM_WIDTH,), (SSM_WIDTH,), (BRANCH,), (3 * BRANCH,)]
    in_specs = [row_spec, _layer_spec((D_MODEL, IN_COLS)), _const_spec(tabs[0].shape), _const_spec(tabs[1].shape),
                _layer_spec(wp["sinkcol"].shape[1:]), *_mixer_weight_specs(None),
                _layer_spec((BRANCH, D_MODEL)), _layer_spec((BRANCH, D_MODEL)), _layer_spec((BRANCH, D_MODEL)),
                _layer_spec((D_MODEL, D_MODEL)), _const_spec(wp["ln_g"].shape), _const_spec(wp["ln_b"].shape),
                *state_specs]
    out_specs = [pl.BlockSpec((m, D_MODEL), lambda l, g: (l * nbg + g, 0)), *state_specs]
    out_shape = [jax.ShapeDtypeStruct((depth * nbg * m, D_MODEL), f32)] + [
        jax.ShapeDtypeStruct((depth * nbg * nb,) + s, f32) for s in state_shapes]
    scratch = [pltpu.VMEM((nbg, m, D_MODEL), f32)] + _mixer_scratch(nb, tt, m)
    return pl.pallas_call(
        kern, grid=(depth, nbg), in_specs=in_specs, out_specs=out_specs, out_shape=out_shape,
        scratch_shapes=scratch, name="sample_layers",
        compiler_params=pltpu.CompilerParams(
            dimension_semantics=("arbitrary", "arbitrary"), vmem_limit_bytes=VMEM_LIMIT),
    )(x2d, wp["win"], tabs[0], tabs[1], wp["sinkcol"], wp["bbh"], wp["cch"], wp["ab"], wp["wglu"], wp["wlru"],
      wp["vec"], wp["wbr_a"], wp["wbr_b"], wp["wbr_c"], wp["wout"], wp["ln_g"], wp["ln_b"], *state)


def _t5_bucket(dist):
    max_exact = N_BUCKETS // 2
    d = jnp.maximum(dist, 0)
    large = max_exact + (jnp.log(jnp.maximum(d, 1).astype(f32) / max_exact)
                         / math.log(MAX_DISTANCE / max_exact) * (N_BUCKETS - max_exact)).astype(jnp.int32)
    large = jnp.minimum(large, N_BUCKETS - 1)
    return jnp.where(d < max_exact, d, large)


def _band_bias(rel_bias, tt):
    s_len = WINDOW + tt
    by_dist = rel_bias.astype(f32)[_t5_bucket(jnp.arange(WINDOW))].T
    row0 = jnp.concatenate([jnp.full((N_Q_HEADS, 1), NEG, f32), by_dist[:, ::-1],
                            jnp.full((N_Q_HEADS, tt), NEG, f32)], axis=1)
    flat = jnp.tile(row0, (1, tt))[:, :tt * s_len]
    return flat.reshape(N_Q_HEADS, tt, s_len)


def _prompt_table(rel_bias, tt):
    return _band_bias(rel_bias, tt).reshape(2, 4 * tt, WINDOW + tt)


def _sample_tables(rel_bias, tt):
    bs = _band_bias(rel_bias, tt).reshape(2, 4, tt, WINDOW + tt)
    same = jnp.eye(SUBLANES, dtype=bool)
    t1 = jnp.where(same[None, None, None, :, :, None], bs[:, :, :, None, None, :WINDOW], NEG)
    t2 = jnp.where(same[None, None, None, :, None, :], bs[:, :, :, None, WINDOW:, None], NEG)
    rows = 4 * tt * SUBLANES
    return t1.reshape(2, rows, SUBLANES * WINDOW), t2.reshape(2, rows, tt * SUBLANES)


def _weights(s_tt, w_in, sinks, w_branch_a, lam_re, lam_im, log_step, b_re, b_im, c_re, c_im, ssm_d, w_glu,
             w_branch_b, conv_w, conv_b, w_a, b_a, w_x, b_x, lam, w_branch_c, w_out, ln_g, ln_b):
    depth = w_in.shape[0]
    hg = SSM_GROUPS // 2
    lr = jnp.minimum(lam_re.astype(f32), -1e-4)
    li = lam_im.astype(f32)
    step = jnp.exp(log_step.astype(f32))[..., None]
    mag = jnp.exp(lr * step)
    ab_re = mag * jnp.cos(li * step)
    ab_im = mag * jnp.sin(li * step)
    den = lr * lr + li * li
    f_re = ((ab_re - 1.0) * lr + ab_im * li) / den
    f_im = (ab_im * lr - (ab_re - 1.0) * li) / den
    bre, bim = b_re.astype(f32), b_im.astype(f32)
    bb_re = f_re[..., None] * bre - f_im[..., None] * bim
    bb_im = f_re[..., None] * bim + f_im[..., None] * bre
    eye = jnp.eye(hg, dtype=f32)
    drv = jnp.stack([bb_re, bb_im], axis=1).reshape(depth, 2, 2, hg, SSM_STATE, SSM_GROUP)
    drv = drv.transpose(0, 2, 3, 5, 1, 4)[:, :, :, :, :, None, :] * eye[None, None, :, None, None, :, None]
    bbh = drv.reshape(depth, 2, hg * SSM_GROUP, SSM_WIDTH).astype(bf16)
    rdo = jnp.stack([c_re.astype(f32), -c_im.astype(f32)], axis=1).reshape(depth, 2, 2, hg, SSM_GROUP, SSM_STATE)
    rdo = rdo.transpose(0, 2, 1, 3, 5, 4)[:, :, :, :, :, None, :] * eye[None, None, None, :, None, :, None]
    cch = rdo.reshape(depth, 2, SSM_WIDTH, hg * SSM_GROUP).astype(bf16)
    ab = jnp.stack([ab_re.reshape(depth, 2, HALF), ab_im.reshape(depth, 2, HALF)], axis=2).reshape(depth, 4, HALF)
    eye4 = jnp.eye(LRU_BLOCKS // 2, dtype=f32)
    blk = jnp.stack([w_a.astype(f32), w_x.astype(f32)], axis=1).reshape(depth, 2, 2, LRU_BLOCKS // 2, 64, 64)
    wlru = (blk[:, :, :, :, :, None, :] * eye4[None, None, None, :, None, :, None]).reshape(depth, 2, 2, 256, 256)
    row = lambda v: v.astype(f32)[:, None, :]
    vec = jnp.concatenate([conv_w.astype(f32), row(conv_b), row(b_a), row(b_x), row(lam), row(ssm_d),
                           jnp.zeros((depth, 16 - V_D - 1, BRANCH), f32)], axis=1)
    sk = sinks.astype(f32).reshape(depth, 2, 4, 1)
    return {
        "win": w_in.astype(bf16),
        "sinks": sinks.astype(f32),
        "sinkcol": jnp.broadcast_to(sk, (depth, 2, 4, s_tt * SUBLANES)).reshape(depth, 2, 4 * s_tt * SUBLANES, 1),
        "bbh": bbh, "cch": cch, "ab": ab,
        "wglu": w_glu.astype(bf16),
        "wlru": wlru.astype(bf16),
        "vec": vec,
        "wbr_a": w_branch_a.astype(bf16), "wbr_b": w_branch_b.astype(bf16), "wbr_c": w_branch_c.astype(bf16),
        "wout": w_out.astype(bf16),
        "ln_g": ln_g.astype(f32), "ln_b": ln_b.astype(f32),
    }


def _group_rows(x, nbg, nb):
    b, t, c = x.shape
    return x.reshape(nbg, nb, t, c).transpose(0, 2, 1, 3).reshape(nbg * t * nb, c)


def _ungroup_rows(x2d, nbg, nb, t):
    c = x2d.shape[-1]
    return x2d.reshape(nbg, t, nb, c).transpose(0, 2, 1, 3).reshape(nbg * nb, t, c)


def _states_out(lead, ko, vo, sre, sim, lru, conv):
    return (ko.reshape(*lead, WINDOW, 2, HEAD_DIM), vo.reshape(*lead, WINDOW, 2, HEAD_DIM),
            sre.reshape(*lead, SSM_GROUPS, SSM_STATE), sim.reshape(*lead, SSM_GROUPS, SSM_STATE),
            lru.reshape(*lead, BRANCH), conv.reshape(*lead, CONV_WIDTH - 1, BRANCH))


P_TT = 64
S_NBG = 4


def kernel(x_prompt, x_sample, cache_k, cache_v, state_ssm_re, state_ssm_im, state_lru, state_conv, rel_bias, w_in, sinks, w_branch_a, ssm_lambda_re, ssm_lambda_im, ssm_log_step, ssm_b_re, ssm_b_im, ssm_c_re, ssm_c_im, ssm_d, ssm_w_glu, w_branch_b, conv_w, conv_b, lru_w_a, lru_b_a, lru_w_x, lru_b_x, lru_lambda, w_branch_c, w_out, ln_g, ln_b):
    depth = w_in.shape[0]
    alpha = (2 * depth) ** 0.25
    pb, pt, _ = x_prompt.shape
    sb, st, _ = x_sample.shape
    s_nb = sb // S_NBG

    wp = _weights(st, w_in, sinks, w_branch_a, ssm_lambda_re, ssm_lambda_im, ssm_log_step, ssm_b_re, ssm_b_im,
                  ssm_c_re, ssm_c_im, ssm_d, ssm_w_glu, w_branch_b, conv_w, conv_b, lru_w_a, lru_b_a, lru_w_x,
                  lru_b_x, lru_lambda, w_branch_c, w_out, ln_g, ln_b)
    p_tab = _prompt_table(rel_bias, P_TT)
    s_tabs = _sample_tables(rel_bias, st)

    st_p = []
    x = x_prompt
    for l in range(depth):
        outs = _prompt_mixer_call(x, wp, p_tab, l, nb=pb, tt=P_TT, natural_in=(l == 0))
        if l == 0:
            z, x, *sp = outs
        else:
            z, *sp = outs
        x = _prompt_out_call(x, z, wp, l, nb=pb, mb=pb * P_TT, alpha=alpha, natural_out=(l == depth - 1))
        st_p.append(_states_out((pb,), *sp))
    y_prompt = x
    stacked_p = [jnp.stack([s[k] for s in st_p]) for k in range(6)]

    n = depth * sb
    s_state = (cache_k.astype(f32).reshape(n, WINDOW, LANES), cache_v.astype(f32).reshape(n, WINDOW, LANES),
               state_ssm_re.astype(f32).reshape(n, SSM_WIDTH), state_ssm_im.astype(f32).reshape(n, SSM_WIDTH),
               state_lru.astype(f32).reshape(n, BRANCH), state_conv.astype(f32).reshape(n, 3 * BRANCH))
    ys, *ss = _sample_call(_group_rows(x_sample, S_NBG, s_nb), wp, s_tabs, s_state, depth=depth, nbg=S_NBG,
                           nb=s_nb, tt=st, alpha=alpha)
    y_sample = _ungroup_rows(ys[(depth - 1) * sb * st:], S_NBG, s_nb, st)
    stacked_s = _states_out((depth, sb), *ss)
    return (y_prompt, y_sample, *stacked_p, *stacked_s)
```

```python
import functools
import math

import jax
import jax.numpy as jnp
from jax import lax
from jax.experimental import pallas as pl
from jax.experimental.pallas import tpu as pltpu

f32 = jnp.float32
bf16 = jnp.bfloat16

D_MODEL = 1024
BRANCH = 512
HEAD_DIM = 64
N_Q_HEADS = 8
WINDOW = 128
N_BUCKETS = 32
MAX_DISTANCE = 128
SSM_GROUPS = 32
SSM_GROUP = 16
SSM_STATE = 64
LRU_BLOCKS = 8
CONV_WIDTH = 4
LRU_C = 8.0
LN_EPS = 1e-5
NEG = -1e30

LANES = 128
SUBLANES = 8
VMEM_LIMIT = 56 * 1024 * 1024

C_Q, C_K, C_V, C_GA, C_UB, C_GB, C_XC, C_GC, C_GM = (
    (0, 512), (512, 640), (640, 768), (768, 1280), (1280, 1792), (1792, 2304),
    (2304, 2816), (2816, 3328), (3328, 6400))
IN_COLS = C_GM[1]
HALF = SSM_GROUPS // 2 * SSM_STATE
SSM_WIDTH = 2 * HALF
N_SLABS = D_MODEL // LANES

V_CONVW, V_CONVB, V_BA, V_BX, V_LAM, V_D = 0, 4, 5, 6, 7, 8


def _split_heads(qa, lo, kvh):
    q_lo = jnp.where(lo, qa, 0.0)
    q_hi = jnp.where(lo, 0.0, qa)
    if kvh == 0:
        return [q_lo, pltpu.roll(q_hi, HEAD_DIM, axis=1)]
    return [pltpu.roll(q_lo, HEAD_DIM, axis=1), q_hi]


def _merge_heads(o0, o1, lo, kvh):
    if kvh == 0:
        return jnp.where(lo, o0, pltpu.roll(o1, HEAD_DIM, axis=1))
    return jnp.where(lo, pltpu.roll(o0, HEAD_DIM, axis=1), o1)


def _softmax_pv(q4, keys, vals, tabs, sink):
    scs = []
    for k, t in zip(keys, tabs):
        sc = lax.dot_general(q4, k, (((1,), (1,)), ((), ())), preferred_element_type=f32)
        scs.append(jnp.where(t > 0.5 * NEG, sc * (HEAD_DIM ** -0.5) + t, NEG))
    mx = sink
    for sc in scs:
        mx = jnp.maximum(mx, jnp.max(sc, axis=-1, keepdims=True))
    den = jnp.exp(sink - mx)
    o = None
    for sc, v in zip(scs, vals):
        p = jnp.exp(sc - mx)
        den = den + jnp.sum(p, axis=-1, keepdims=True)
        pv = jnp.dot(p.astype(bf16), v, preferred_element_type=f32)
        o = pv if o is None else o + pv
    return o / den


def _attend_prompt(layer, qs, kvs, atts, tabm, sinks_ref, sc_scr, p_scr, *, nb, tt):
    s_len = WINDOW + tt
    lo = lax.broadcasted_iota(jnp.int32, (tt, LANES), 1) < HEAD_DIM

    for b in range(nb):
        kk16 = kvs.at[0][pl.ds(b, s_len, stride=nb), :].astype(bf16)
        for kvh in range(2):
            parts = []
            for i in range(2):
                parts += _split_heads(qs.at[2 * kvh + i][pl.ds(b, tt, stride=nb), :], lo, kvh)
            q4 = jnp.concatenate(parts, axis=0).astype(bf16)
            sc_scr[b, kvh] = lax.dot_general(q4, kk16, (((1,), (1,)), ((), ())), preferred_element_type=f32)

    for b in range(nb):
        for kvh in range(2):
            for g in range(4):
                rs = slice(g * tt, (g + 1) * tt)
                t = tabm[kvh, rs, :]
                sc = jnp.where(t > 0.5 * NEG, sc_scr[b, kvh, rs, :] * (HEAD_DIM ** -0.5) + t, NEG)
                sink = sinks_ref[layer, 4 * kvh + g]
                mx = jnp.maximum(jnp.max(sc, axis=-1, keepdims=True), sink)
                p = jnp.exp(sc - mx)
                den = jnp.sum(p, axis=-1, keepdims=True) + jnp.exp(sink - mx)
                p_scr[b, kvh, rs, :] = (p / den).astype(bf16)

    for b in range(nb):
        vv16 = kvs.at[1][pl.ds(b, s_len, stride=nb), :].astype(bf16)
        for kvh in range(2):
            o = jnp.dot(p_scr[b, kvh], vv16, preferred_element_type=f32)
            for i in range(2):
                slab = _merge_heads(o[2 * i * tt:(2 * i + 1) * tt], o[(2 * i + 1) * tt:(2 * i + 2) * tt], lo, kvh)
                atts.at[2 * kvh + i][pl.ds(b, tt, stride=nb), :] = slab


def _attend_sample(qs, kvn, atts, kc_ref, vc_ref, tab1_ref, tab2_ref, sink_ref, ko_ref, vo_ref, *, nb, tt):
    lo = lax.broadcasted_iota(jnp.int32, (SUBLANES * tt, LANES), 1) < HEAD_DIM
    for sg in range(nb // SUBLANES):
        b0 = SUBLANES * sg

        def rows(ref, b0=b0):
            return jnp.concatenate([ref[t * nb + b0:t * nb + b0 + SUBLANES, :] for t in range(tt)], axis=0)

        kc = kc_ref[b0:b0 + SUBLANES].reshape(SUBLANES * WINDOW, LANES).astype(bf16)
        vc = vc_ref[b0:b0 + SUBLANES].reshape(SUBLANES * WINDOW, LANES).astype(bf16)
        kn = rows(kvn.at[0]).astype(bf16)
        vn = rows(kvn.at[1]).astype(bf16)
        for kvh in range(2):
            parts = []
            for i in range(2):
                parts += _split_heads(rows(qs.at[2 * kvh + i]), lo, kvh)
            q4 = jnp.concatenate(parts, axis=0).astype(bf16)
            o = _softmax_pv(q4, [kc, kn], [vc, vn], [tab1_ref[kvh], tab2_ref[kvh]], sink_ref[kvh])
            r = SUBLANES * tt
            for i in range(2):
                slab = _merge_heads(o[2 * i * r:(2 * i + 1) * r], o[(2 * i + 1) * r:(2 * i + 2) * r], lo, kvh)
                for t in range(tt):
                    atts[2 * kvh + i, t * nb + b0:t * nb + b0 + SUBLANES, :] = slab[SUBLANES * t:SUBLANES * (t + 1)]
    for b in range(nb):
        ko_ref[b, 0:WINDOW - tt, :] = kc_ref[b, tt:WINDOW, :]
        vo_ref[b, 0:WINDOW - tt, :] = vc_ref[b, tt:WINDOW, :]
        ko_ref[b, WINDOW - tt:WINDOW, :] = kvn.at[0][pl.ds(b, tt, stride=nb), :]
        vo_ref[b, WINDOW - tt:WINDOW, :] = kvn.at[1][pl.ds(b, tt, stride=nb), :]


def _s5_block(ub, xs, hst, bbh_ref, cch_ref, ab_ref, *, nb, tt):
    ub16 = ub.astype(bf16)
    ys = []
    cw = HALF // 2
    for hf in range(2):
        xs[...] = jnp.dot(ub16[:, 256 * hf:256 * (hf + 1)], bbh_ref[hf], preferred_element_type=f32)
        for ck in range(2):
            re0, im0, st0 = cw * ck, HALF + cw * ck, HALF * hf + cw * ck
            ar = jnp.broadcast_to(ab_ref[2 * hf:2 * hf + 1, re0:re0 + cw], (SUBLANES, cw))
            ai = jnp.broadcast_to(ab_ref[2 * hf + 1:2 * hf + 2, re0:re0 + cw], (SUBLANES, cw))
            for bg in range(nb // SUBLANES):
                b0 = SUBLANES * bg

                hr = hst[0, b0:b0 + SUBLANES, st0:st0 + cw]
                hi = hst[1, b0:b0 + SUBLANES, st0:st0 + cw]
                for t in range(tt):
                    r0 = t * nb + b0
                    xr = xs[r0:r0 + SUBLANES, re0:re0 + cw]
                    xi = xs[r0:r0 + SUBLANES, im0:im0 + cw]
                    hr, hi = ar * hr - ai * hi + xr, ar * hi + ai * hr + xi
                    xs[r0:r0 + SUBLANES, re0:re0 + cw] = hr
                    xs[r0:r0 + SUBLANES, im0:im0 + cw] = hi
                hst[0, b0:b0 + SUBLANES, st0:st0 + cw] = hr
                hst[1, b0:b0 + SUBLANES, st0:st0 + cw] = hi
        ys.append(jnp.dot(xs[...].astype(bf16), cch_ref[hf], preferred_element_type=f32))
    return jnp.concatenate(ys, axis=1)


def _lru_block(xcs, la, lb, lst, wlru_ref, vec_ref, *, nb, tt):
    m = nb * tt
    conv = vec_ref[V_CONVB:V_CONVB + 1, :] + vec_ref[V_CONVW:V_CONVW + 1, :] * xcs[0:m, :]
    for j in range(1, CONV_WIDTH):
        conv = conv + vec_ref[V_CONVW + j:V_CONVW + j + 1, :] * xcs[j * nb:j * nb + m, :]
    c16 = conv.astype(bf16)

    def block_dot(g):
        return jnp.concatenate(
            [jnp.dot(c16[:, 256 * hf:256 * (hf + 1)], wlru_ref[g, hf], preferred_element_type=f32)
             for hf in range(2)], axis=1)

    r = jax.nn.sigmoid(block_dot(0) + vec_ref[V_BA:V_BA + 1, :])
    i = jax.nn.sigmoid(block_dot(1) + vec_ref[V_BX:V_BX + 1, :])
    log_a = LRU_C * r * jax.nn.log_sigmoid(vec_ref[V_LAM:V_LAM + 1, :])
    a = jnp.exp(log_a)
    la[...] = a
    lb[...] = conv * i * jnp.sqrt(-jnp.tanh(log_a) * (a * a + 1.0))
    for bg in range(nb // SUBLANES):
        b0 = SUBLANES * bg

        h = lst[b0:b0 + SUBLANES, :]
        for t in range(tt):
            r0 = t * nb + b0
            h = la[r0:r0 + SUBLANES, :] * h + lb[r0:r0 + SUBLANES, :]
            lb[r0:r0 + SUBLANES, :] = h
        lst[b0:b0 + SUBLANES, :] = h


def _mixer_tail(xb, win_ref, atts, xs, hst, xcs, la, lb, lst, bbh_ref, cch_ref, ab_ref, wglu_ref, wlru_ref,
                vec_ref, *, nb, tt):
    m = nb * tt

    def proj(cols):
        return jnp.dot(xb, win_ref[:, cols[0]:cols[1]], preferred_element_type=f32)

    att = jnp.concatenate([atts[j] for j in range(4)], axis=1)
    za = (att * jax.nn.silu(proj(C_GA))).astype(bf16)

    ub = proj(C_UB)
    y = _s5_block(ub, xs, hst, bbh_ref, cch_ref, ab_ref, nb=nb, tt=tt) + vec_ref[V_D:V_D + 1, :] * ub
    zz = jax.nn.gelu(y)
    zz = zz * jax.nn.sigmoid(jnp.dot(zz.astype(bf16), wglu_ref[...], preferred_element_type=f32))
    zb = (zz * jax.nn.silu(proj(C_GB))).astype(bf16)

    xcs[3 * nb:3 * nb + m, :] = proj(C_XC)
    _lru_block(xcs, la, lb, lst, wlru_ref, vec_ref, nb=nb, tt=tt)
    zc = (lb[...] * jax.nn.silu(proj(C_GC))).astype(bf16)
    return za, zb, zc


def _out_core(x, zs, win_ref, wbr_refs, wout_ref, ln_g, ln_b, alpha):
    xb = x.astype(bf16)
    merged = None
    for k in range(3):
        c0 = C_GM[0] + D_MODEL * k
        g = jax.nn.sigmoid(jnp.dot(xb, win_ref[:, c0:c0 + D_MODEL], preferred_element_type=f32))
        yk = jnp.dot(zs[k], wbr_refs[k][...], preferred_element_type=f32)
        merged = g * yk if merged is None else merged + g * yk
    out = jnp.dot(merged.astype(bf16), wout_ref[...], preferred_element_type=f32)
    h = alpha * x + out
    mu = jnp.mean(h, axis=-1, keepdims=True)
    d = h - mu
    var = jnp.mean(d * d, axis=-1, keepdims=True)
    return d * lax.rsqrt(var + LN_EPS) * ln_g + ln_b


def _prompt_mixer_kernel(*refs, layer, nb, tt, nts, natural_in):
    (x_ref, win_ref, tab_ref, sinks_ref, bbh_ref, cch_ref, ab_ref, wglu_ref, wlru_ref, vec_ref), refs = refs[:10], refs[10:]
    z_ref, refs = refs[0], refs[1:]
    if natural_in:
        xtb_ref, refs = refs[0], refs[1:]
    (ko_ref, vo_ref, sre_ref, sim_ref, lruo_ref, convo_ref,
     qs, kvs, atts, xs, hst, xcs, la, lb, lst, tabm, sc_scr, p_scr), refs = refs[:18], refs[18:]
    ts = pl.program_id(0)
    m = nb * tt
    w = WINDOW * nb
    last = ts == nts - 1

    @pl.when(ts == 0)
    def _zero_state():
        kvs[:, 0:w, :] = jnp.zeros((2, w, LANES), f32)
        hst[...] = jnp.zeros(hst.shape, f32)
        lst[...] = jnp.zeros(lst.shape, f32)
        xcs[0:3 * nb, :] = jnp.zeros((3 * nb, BRANCH), f32)

    if natural_in:
        (xsl,) = refs
        for b in range(nb):
            for k in range(N_SLABS):
                xsl.at[k][pl.ds(b, tt, stride=nb), :] = x_ref[b, :, LANES * k:LANES * (k + 1)]
        x = jnp.concatenate([xsl[k] for k in range(N_SLABS)], axis=1)
        xtb_ref[...] = x
    else:
        x = x_ref[...]
    xb = x.astype(bf16)

    def proj(cols):
        return jnp.dot(xb, win_ref[:, cols[0]:cols[1]], preferred_element_type=f32)

    q = proj(C_Q)
    for j in range(4):
        qs[j] = q[:, LANES * j:LANES * (j + 1)]
    kvs[0, w:w + m, :] = proj(C_K)
    kvs[1, w:w + m, :] = proj(C_V)
    kj = lax.broadcasted_iota(jnp.int32, (2, 4 * tt, WINDOW + tt), 2)
    tabm[...] = jnp.where(kj + ts * tt >= WINDOW, tab_ref[...], NEG)
    _attend_prompt(layer, qs, kvs, atts, tabm, sinks_ref, sc_scr, p_scr, nb=nb, tt=tt)
    kvs[:, 0:w, :] = kvs[:, m:m + w, :]

    za, zb, zc = _mixer_tail(xb, win_ref, atts, xs, hst, xcs, la, lb, lst, bbh_ref, cch_ref, ab_ref, wglu_ref,
                             wlru_ref, vec_ref, nb=nb, tt=tt)
    z_ref[:, 0:BRANCH] = za
    z_ref[:, BRANCH:2 * BRANCH] = zb
    z_ref[:, 2 * BRANCH:3 * BRANCH] = zc

    @pl.when(last)
    def _store_state():
        sre_ref[...] = hst[0]
        sim_ref[...] = hst[1]
        lruo_ref[...] = lst[...]
        for t in range(3):
            convo_ref[:, BRANCH * t:BRANCH * (t + 1)] = xcs[m + t * nb:m + (t + 1) * nb, :]
        for b in range(nb):
            ko_ref[b] = kvs.at[0][pl.ds(b, WINDOW, stride=nb), :]
            vo_ref[b] = kvs.at[1][pl.ds(b, WINDOW, stride=nb), :]

    xcs[0:3 * nb, :] = xcs[m:m + 3 * nb, :]


def _prompt_out_kernel(*refs, layer, alpha, nb, natural_out):
    if natural_out:
        x_ref, z_ref, win_ref, wa_ref, wb_ref, wc_ref, wout_ref, lng_ref, lnb_ref, y_ref, ysl = refs
    else:
        x_ref, z_ref, win_ref, wa_ref, wb_ref, wc_ref, wout_ref, lng_ref, lnb_ref, y_ref = refs
    zs = [z_ref[:, BRANCH * k:BRANCH * (k + 1)] for k in range(3)]
    y = _out_core(x_ref[...], zs, win_ref, (wa_ref, wb_ref, wc_ref), wout_ref,
                  lng_ref[layer:layer + 1, :], lnb_ref[layer:layer + 1, :], alpha)
    if natural_out:
        tt = y.shape[0] // nb
        for k in range(N_SLABS):
            ysl[k] = y[:, LANES * k:LANES * (k + 1)]
        for b in range(nb):
            for k in range(N_SLABS):
                y_ref[b, :, LANES * k:LANES * (k + 1)] = ysl.at[k][pl.ds(b, tt, stride=nb), :]
    else:
        y_ref[...] = y


def _sample_kernel(x_ref, win_ref, tab1_ref, tab2_ref, sink_ref, bbh_ref, cch_ref, ab_ref, wglu_ref, wlru_ref,
                   vec_ref, wa_ref, wb_ref, wc_ref, wout_ref, lng_ref, lnb_ref,
                   kc_ref, vc_ref, sre0_ref, sim0_ref, lru0_ref, conv0_ref,
                   y_ref, ko_ref, vo_ref, sre_ref, sim_ref, lruo_ref, convo_ref,
                   xcar, qs, kvn, atts, xs, hst, xcs, la, lb, lst, *, alpha, nb, tt):
    layer = pl.program_id(0)
    g = pl.program_id(1)
    m = nb * tt

    @pl.when(layer == 0)
    def _first_layer_input():
        xcar[g] = x_ref[...]

    x = xcar[g]
    xb = x.astype(bf16)
    hst[0] = sre0_ref[...]
    hst[1] = sim0_ref[...]
    lst[...] = lru0_ref[...]
    for t in range(3):
        xcs[t * nb:(t + 1) * nb, :] = conv0_ref[:, BRANCH * t:BRANCH * (t + 1)]

    def proj(cols):
        return jnp.dot(xb, win_ref[:, cols[0]:cols[1]], preferred_element_type=f32)

    q = proj(C_Q)
    for j in range(4):
        qs[j] = q[:, LANES * j:LANES * (j + 1)]
    kvn[0] = proj(C_K)
    kvn[1] = proj(C_V)
    _attend_sample(qs, kvn, atts, kc_ref, vc_ref, tab1_ref, tab2_ref, sink_ref, ko_ref, vo_ref, nb=nb, tt=tt)
    zs = _mixer_tail(xb, win_ref, atts, xs, hst, xcs, la, lb, lst, bbh_ref, cch_ref, ab_ref, wglu_ref,
                     wlru_ref, vec_ref, nb=nb, tt=tt)
    sre_ref[...] = hst[0]
    sim_ref[...] = hst[1]
    lruo_ref[...] = lst[...]
    for t in range(3):
        convo_ref[:, BRANCH * t:BRANCH * (t + 1)] = xcs[m + t * nb:m + (t + 1) * nb, :]

    y = _out_core(x, zs, win_ref, (wa_ref, wb_ref, wc_ref), wout_ref,
                  lng_ref[pl.ds(layer, 1), :], lnb_ref[pl.ds(layer, 1), :], alpha)
    xcar[g] = y
    y_ref[...] = y


def _layer_spec(shape, layer=None):
    if layer is None:
        idx = lambda l, *_: (l,) + (0,) * len(shape)
    else:
        idx = lambda *_: (layer,) + (0,) * len(shape)
    return pl.BlockSpec((None,) + tuple(shape), idx, pipeline_mode=pl.Buffered(1))


def _const_spec(shape):
    return pl.BlockSpec(tuple(shape), lambda *_: (0,) * len(shape), pipeline_mode=pl.Buffered(1))


def _mixer_weight_specs(layer):
    return [
        _layer_spec((2, 256, SSM_WIDTH), layer),
        _layer_spec((2, SSM_WIDTH, 256), layer),
        _layer_spec((4, HALF), layer),
        _layer_spec((BRANCH, BRANCH), layer),
        _layer_spec((2, 2, 256, 256), layer),
        _layer_spec((16, BRANCH), layer),
    ]


def _mixer_scratch(nb, tt, kv_rows):
    m = nb * tt
    return [
        pltpu.VMEM((4, m, LANES), f32),
        pltpu.VMEM((2, kv_rows, LANES), f32),
        pltpu.VMEM((4, m, LANES), f32),
        pltpu.VMEM((m, SSM_WIDTH), f32),
        pltpu.VMEM((2, nb, SSM_WIDTH), f32),
        pltpu.VMEM((m + 3 * nb, BRANCH), f32),
        pltpu.VMEM((m, BRANCH), f32),
        pltpu.VMEM((m, BRANCH), f32),
        pltpu.VMEM((nb, BRANCH), f32),
    ]


def _prompt_mixer_call(x, wp, tab, layer, *, nb, tt, natural_in):
    nts = (x.shape[1] if natural_in else x.shape[0] // nb) // tt
    m, w, s_len = nb * tt, WINDOW * nb, WINDOW + tt
    kern = functools.partial(_prompt_mixer_kernel, layer=layer, nb=nb, tt=tt, nts=nts, natural_in=natural_in)
    x_spec = (pl.BlockSpec((nb, tt, D_MODEL), lambda t: (0, t, 0)) if natural_in
              else pl.BlockSpec((m, D_MODEL), lambda t: (t, 0)))
    in_specs = [x_spec,
                pl.BlockSpec((None, D_MODEL, C_GM[0]), lambda t: (layer, 0, 0), pipeline_mode=pl.Buffered(1)),
                _const_spec(tab.shape), pl.BlockSpec(memory_space=pltpu.SMEM), *_mixer_weight_specs(layer)]
    whole = lambda shape: pl.BlockSpec(shape, lambda t: (0,) * len(shape))
    out_specs = [pl.BlockSpec((m, 3 * BRANCH), lambda t: (t, 0))]
    out_shape = [jax.ShapeDtypeStruct((nts * m, 3 * BRANCH), bf16)]
    if natural_in:
        out_specs.append(pl.BlockSpec((m, D_MODEL), lambda t: (t, 0)))
        out_shape.append(jax.ShapeDtypeStruct((nts * m, D_MODEL), f32))
    state_shapes = [(nb, WINDOW, LANES), (nb, WINDOW, LANES), (nb, SSM_WIDTH), (nb, SSM_WIDTH),
                    (nb, BRANCH), (nb, 3 * BRANCH)]
    out_specs += [whole(s) for s in state_shapes]
    out_shape += [jax.ShapeDtypeStruct(s, f32) for s in state_shapes]
    scratch = _mixer_scratch(nb, tt, w + m) + [
        pltpu.VMEM((2, 4 * tt, s_len), f32),
        pltpu.VMEM((nb, 2, 4 * tt, s_len), f32),
        pltpu.VMEM((nb, 2, 4 * tt, s_len), bf16),
    ]
    if natural_in:
        scratch.append(pltpu.VMEM((N_SLABS, m, LANES), f32))
    return pl.pallas_call(
        kern, grid=(nts,), in_specs=in_specs, out_specs=out_specs, out_shape=out_shape,
        scratch_shapes=scratch, name=f"mixer_prompt_{layer}",
        compiler_params=pltpu.CompilerParams(dimension_semantics=("arbitrary",), vmem_limit_bytes=VMEM_LIMIT),
    )(x, wp["win"], tab, wp["sinks"], wp["bbh"], wp["cch"], wp["ab"], wp["wglu"], wp["wlru"], wp["vec"])


def _prompt_out_call(x2d, z2d, wp, layer, *, nb, mb, alpha, natural_out):
    rows = x2d.shape[0]
    kern = functools.partial(_prompt_out_kernel, layer=layer, alpha=alpha, nb=nb, natural_out=natural_out)
    if natural_out:
        out_spec = pl.BlockSpec((nb, mb // nb, D_MODEL), lambda i: (0, i, 0))
        out_shape = jax.ShapeDtypeStruct((nb, rows // nb, D_MODEL), f32)
        scratch = [pltpu.VMEM((N_SLABS, mb, LANES), f32)]
    else:
        out_spec = pl.BlockSpec((mb, D_MODEL), lambda i: (i, 0))
        out_shape = jax.ShapeDtypeStruct((rows, D_MODEL), f32)
        scratch = []
    return pl.pallas_call(
        kern, grid=(rows // mb,),
        in_specs=[
            pl.BlockSpec((mb, D_MODEL), lambda i: (i, 0)),
            pl.BlockSpec((mb, 3 * BRANCH), lambda i: (i, 0)),
            _layer_spec((D_MODEL, IN_COLS), layer),
            _layer_spec((BRANCH, D_MODEL), layer), _layer_spec((BRANCH, D_MODEL), layer),
            _layer_spec((BRANCH, D_MODEL), layer),
            _layer_spec((D_MODEL, D_MODEL), layer),
            _const_spec(wp["ln_g"].shape), _const_spec(wp["ln_b"].shape),
        ],
        out_specs=out_spec, out_shape=out_shape, scratch_shapes=scratch, name=f"out_prompt_{layer}",
        compiler_params=pltpu.CompilerParams(dimension_semantics=("arbitrary",), vmem_limit_bytes=VMEM_LIMIT),
    )(x2d, z2d, wp["win"], wp["wbr_a"], wp["wbr_b"], wp["wbr_c"], wp["wout"], wp["ln_g"], wp["ln_b"])


def _sample_call(x2d, wp, tabs, state, *, depth, nbg, nb, tt, alpha):
    m = nb * tt
    kern = functools.partial(_sample_kernel, alpha=alpha, nb=nb, tt=tt)
    row_spec = pl.BlockSpec((m, D_MODEL), lambda l, g: (g, 0))
    st = lambda *shape: pl.BlockSpec((nb,) + shape, lambda l, g: (l * nbg + g,) + (0,) * len(shape))
    state_specs = [st(WINDOW, LANES), st(WINDOW, LANES), st(SSM_WIDTH), st(SSM_WIDTH), st(BRANCH), st(3 * BRANCH)]
    state_shapes = [(WINDOW, LANES), (WINDOW, LANES), (SSM_WIDTH,), (SSM_WIDTH,), (BRANCH,), (3 * BRANCH,)]
    in_specs = [row_spec, _layer_spec((D_MODEL, IN_COLS)), _const_spec(tabs[0].shape), _const_spec(tabs[1].shape),
                _layer_spec(wp["sinkcol"].shape[1:]), *_mixer_weight_specs(None),
                _layer_spec((BRANCH, D_MODEL)), _layer_spec((BRANCH, D_MODEL)), _layer_spec((BRANCH, D_MODEL)),
                _layer_spec((D_MODEL, D_MODEL)), _const_spec(wp["ln_g"].shape), _const_spec(wp["ln_b"].shape),
                *state_specs]
    out_specs = [pl.BlockSpec((m, D_MODEL), lambda l, g: (l * nbg + g, 0)), *state_specs]
    out_shape = [jax.ShapeDtypeStruct((depth * nbg * m, D_MODEL), f32)] + [
        jax.ShapeDtypeStruct((depth * nbg * nb,) + s, f32) for s in state_shapes]
    scratch = [pltpu.VMEM((nbg, m, D_MODEL), f32)] + _mixer_scratch(nb, tt, m)
    return pl.pallas_call(
        kern, grid=(depth, nbg), in_specs=in_specs, out_specs=out_specs, out_shape=out_shape,
        scratch_shapes=scratch, name="sample_layers",
        compiler_params=pltpu.CompilerParams(
            dimension_semantics=("arbitrary", "arbitrary"), vmem_limit_bytes=VMEM_LIMIT),
    )(x2d, wp["win"], tabs[0], tabs[1], wp["sinkcol"], wp["bbh"], wp["cch"], wp["ab"], wp["wglu"], wp["wlru"],
      wp["vec"], wp["wbr_a"], wp["wbr_b"], wp["wbr_c"], wp["wout"], wp["ln_g"], wp["ln_b"], *state)


def _t5_bucket(dist):
    max_exact = N_BUCKETS // 2
    d = jnp.maximum(dist, 0)
    large = max_exact + (jnp.log(jnp.maximum(d, 1).astype(f32) / max_exact)
                         / math.log(MAX_DISTANCE / max_exact) * (N_BUCKETS - max_exact)).astype(jnp.int32)
    large = jnp.minimum(large, N_BUCKETS - 1)
    return jnp.where(d < max_exact, d, large)


def _band_bias(rel_bias, tt):
    s_len = WINDOW + tt
    by_dist = rel_bias.astype(f32)[_t5_bucket(jnp.arange(WINDOW))].T
    row0 = jnp.concatenate([jnp.full((N_Q_HEADS, 1), NEG, f32), by_dist[:, ::-1],
                            jnp.full((N_Q_HEADS, tt), NEG, f32)], axis=1)
    flat = jnp.tile(row0, (1, tt))[:, :tt * s_len]
    return flat.reshape(N_Q_HEADS, tt, s_len)


def _prompt_table(rel_bias, tt):
    return _band_bias(rel_bias, tt).reshape(2, 4 * tt, WINDOW + tt)


def _sample_tables(rel_bias, tt):
    bs = _band_bias(rel_bias, tt).reshape(2, 4, tt, WINDOW + tt)
    same = jnp.eye(SUBLANES, dtype=bool)
    t1 = jnp.where(same[None, None, None, :, :, None], bs[:, :, :, None, None, :WINDOW], NEG)
    t2 = jnp.where(same[None, None, None, :, None, :], bs[:, :, :, None, WINDOW:, None], NEG)
    rows = 4 * tt * SUBLANES
    return t1.reshape(2, rows, SUBLANES * WINDOW), t2.reshape(2, rows, tt * SUBLANES)


def _weights(s_tt, w_in, sinks, w_branch_a, lam_re, lam_im, log_step, b_re, b_im, c_re, c_im, ssm_d, w_glu,
             w_branch_b, conv_w, conv_b, w_a, b_a, w_x, b_x, lam, w_branch_c, w_out, ln_g, ln_b):
    depth = w_in.shape[0]
    hg = SSM_GROUPS // 2
    lr = jnp.minimum(lam_re.astype(f32), -1e-4)
    li = lam_im.astype(f32)
    step = jnp.exp(log_step.astype(f32))[..., None]
    mag = jnp.exp(lr * step)
    ab_re = mag * jnp.cos(li * step)
    ab_im = mag * jnp.sin(li * step)
    den = lr * lr + li * li
    f_re = ((ab_re - 1.0) * lr + ab_im * li) / den
    f_im = (ab_im * lr - (ab_re - 1.0) * li) / den
    bre, bim = b_re.astype(f32), b_im.astype(f32)
    bb_re = f_re[..., None] * bre - f_im[..., None] * bim
    bb_im = f_re[..., None] * bim + f_im[..., None] * bre
    eye = jnp.eye(hg, dtype=f32)
    drv = jnp.stack([bb_re, bb_im], axis=1).reshape(depth, 2, 2, hg, SSM_STATE, SSM_GROUP)
    drv = drv.transpose(0, 2, 3, 5, 1, 4)[:, :, :, :, :, None, :] * eye[None, None, :, None, None, :, None]
    bbh = drv.reshape(depth, 2, hg * SSM_GROUP, SSM_WIDTH).astype(bf16)
    rdo = jnp.stack([c_re.astype(f32), -c_im.astype(f32)], axis=1).reshape(depth, 2, 2, hg, SSM_GROUP, SSM_STATE)
    rdo = rdo.transpose(0, 2, 1, 3, 5, 4)[:, :, :, :, :, None, :] * eye[None, None, None, :, None, :, None]
    cch = rdo.reshape(depth, 2, SSM_WIDTH, hg * SSM_GROUP).astype(bf16)
    ab = jnp.stack([ab_re.reshape(depth, 2, HALF), ab_im.reshape(depth, 2, HALF)], axis=2).reshape(depth, 4, HALF)
    eye4 = jnp.eye(LRU_BLOCKS // 2, dtype=f32)
    blk = jnp.stack([w_a.astype(f32), w_x.astype(f32)], axis=1).reshape(depth, 2, 2, LRU_BLOCKS // 2, 64, 64)
    wlru = (blk[:, :, :, :, :, None, :] * eye4[None, None, None, :, None, :, None]).reshape(depth, 2, 2, 256, 256)
    row = lambda v: v.astype(f32)[:, None, :]
    vec = jnp.concatenate([conv_w.astype(f32), row(conv_b), row(b_a), row(b_x), row(lam), row(ssm_d),
                           jnp.zeros((depth, 16 - V_D - 1, BRANCH), f32)], axis=1)
    sk = sinks.astype(f32).reshape(depth, 2, 4, 1)
    return {
        "win": w_in.astype(bf16),
        "sinks": sinks.astype(f32),
        "sinkcol": jnp.broadcast_to(sk, (depth, 2, 4, s_tt * SUBLANES)).reshape(depth, 2, 4 * s_tt * SUBLANES, 1),
        "bbh": bbh, "cch": cch, "ab": ab,
        "wglu": w_glu.astype(bf16),
        "wlru": wlru.astype(bf16),
        "vec": vec,
        "wbr_a": w_branch_a.astype(bf16), "wbr_b": w_branch_b.astype(bf16), "wbr_c": w_branch_c.astype(bf16),
        "wout": w_out.astype(bf16),
        "ln_g": ln_g.astype(f32), "ln_b": ln_b.astype(f32),
    }


def _group_rows(x, nbg, nb):
    b, t, c = x.shape
    return x.reshape(nbg, nb, t, c).transpose(0, 2, 1, 3).reshape(nbg * t * nb, c)


def _ungroup_rows(x2d, nbg, nb, t):
    c = x2d.shape[-1]
    return x2d.reshape(nbg, t, nb, c).transpose(0, 2, 1, 3).reshape(nbg * nb, t, c)


def _states_out(lead, ko, vo, sre, sim, lru, conv):
    return (ko.reshape(*lead, WINDOW, 2, HEAD_DIM), vo.reshape(*lead, WINDOW, 2, HEAD_DIM),
            sre.reshape(*lead, SSM_GROUPS, SSM_STATE), sim.reshape(*lead, SSM_GROUPS, SSM_STATE),
            lru.reshape(*lead, BRANCH), conv.reshape(*lead, CONV_WIDTH - 1, BRANCH))


P_TT = 64
S_NBG = 4


def kernel(x_prompt, x_sample, cache_k, cache_v, state_ssm_re, state_ssm_im, state_lru, state_conv, rel_bias, w_in, sinks, w_branch_a, ssm_lambda_re, ssm_lambda_im, ssm_log_step, ssm_b_re, ssm_b_im, ssm_c_re, ssm_c_im, ssm_d, ssm_w_glu, w_branch_b, conv_w, conv_b, lru_w_a, lru_b_a, lru_w_x, lru_b_x, lru_lambda, w_branch_c, w_out, ln_g, ln_b):
    depth = w_in.shape[0]
    alpha = (2 * depth) ** 0.25
    pb, pt, _ = x_prompt.shape
    sb, st, _ = x_sample.shape
    s_nb = sb // S_NBG

    wp = _weights(st, w_in, sinks, w_branch_a, ssm_lambda_re, ssm_lambda_im, ssm_log_step, ssm_b_re, ssm_b_im,
                  ssm_c_re, ssm_c_im, ssm_d, ssm_w_glu, w_branch_b, conv_w, conv_b, lru_w_a, lru_b_a, lru_w_x,
                  lru_b_x, lru_lambda, w_branch_c, w_out, ln_g, ln_b)
    p_tab = _prompt_table(rel_bias, P_TT)
    s_tabs = _sample_tables(rel_bias, st)

    st_p = []
    x = x_prompt
    for l in range(depth):
        outs = _prompt_mixer_call(x, wp, p_tab, l, nb=pb, tt=P_TT, natural_in=(l == 0))
        if l == 0:
            z, x, *sp = outs
        else:
            z, *sp = outs
        x = _prompt_out_call(x, z, wp, l, nb=pb, mb=pb * P_TT, alpha=alpha, natural_out=(l == depth - 1))
        st_p.append(_states_out((pb,), *sp))
    y_prompt = x
    stacked_p = [jnp.stack([s[k] for s in st_p]) for k in range(6)]

    n = depth * sb
    s_state = (cache_k.astype(f32).reshape(n, WINDOW, LANES), cache_v.astype(f32).reshape(n, WINDOW, LANES),
               state_ssm_re.astype(f32).reshape(n, SSM_WIDTH), state_ssm_im.astype(f32).reshape(n, SSM_WIDTH),
               state_lru.astype(f32).reshape(n, BRANCH), state_conv.astype(f32).reshape(n, 3 * BRANCH))
    ys, *ss = _sample_call(_group_rows(x_sample, S_NBG, s_nb), wp, s_tabs, s_state, depth=depth, nbg=S_NBG,
                           nb=s_nb, tt=st, alpha=alpha)
    y_sample = _ungroup_rows(ys[(depth - 1) * sb * st:], S_NBG, s_nb, st)
    stacked_s = _states_out((depth, sb), *ss)
    return (y_prompt, y_sample, *stacked_p, *stacked_s)
```

```python
import functools
import math

import jax
import jax.numpy as jnp
from jax import lax
from jax.experimental import pallas as pl
from jax.experimental.pallas import tpu as pltpu

f32 = jnp.float32
bf16 = jnp.bfloat16

D_MODEL = 1024
BRANCH = 512
HEAD_DIM = 64
N_Q_HEADS = 8
WINDOW = 128
N_BUCKETS = 32
MAX_DISTANCE = 128
SSM_GROUPS = 32
SSM_GROUP = 16
SSM_STATE = 64
LRU_BLOCKS = 8
CONV_WIDTH = 4
LRU_C = 8.0
LN_EPS = 1e-5
NEG = -1e30

LANES = 128
SUBLANES = 8
VMEM_LIMIT = 56 * 1024 * 1024

C_Q, C_K, C_V, C_GA, C_UB, C_GB, C_XC, C_GC, C_GM = (
    (0, 512), (512, 640), (640, 768), (768, 1280), (1280, 1792), (1792, 2304),
    (2304, 2816), (2816, 3328), (3328, 6400))
IN_COLS = C_GM[1]
HALF = SSM_GROUPS // 2 * SSM_STATE
SSM_WIDTH = 2 * HALF
N_SLABS = D_MODEL // LANES

V_CONVW, V_CONVB, V_BA, V_BX, V_LAM, V_D = 0, 4, 5, 6, 7, 8


def _split_heads(qa, lo, kvh):
    q_lo = jnp.where(lo, qa, 0.0)
    q_hi = jnp.where(lo, 0.0, qa)
    if kvh == 0:
        return [q_lo, pltpu.roll(q_hi, HEAD_DIM, axis=1)]
    return [pltpu.roll(q_lo, HEAD_DIM, axis=1), q_hi]


def _merge_heads(o0, o1, lo, kvh):
    if kvh == 0:
        return jnp.where(lo, o0, pltpu.roll(o1, HEAD_DIM, axis=1))
    return jnp.where(lo, pltpu.roll(o0, HEAD_DIM, axis=1), o1)


def _softmax_pv(q4, keys, vals, tabs, sink):
    scs = []
    for k, t in zip(keys, tabs):
        sc = lax.dot_general(q4, k, (((1,), (1,)), ((), ())), preferred_element_type=f32)
        scs.append(jnp.where(t > 0.5 * NEG, sc * (HEAD_DIM ** -0.5) + t, NEG))
    mx = sink
    for sc in scs:
        mx = jnp.maximum(mx, jnp.max(sc, axis=-1, keepdims=True))
    den = jnp.exp(sink - mx)
    o = None
    for sc, v in zip(scs, vals):
        p = jnp.exp(sc - mx)
        den = den + jnp.sum(p, axis=-1, keepdims=True)
        pv = jnp.dot(p.astype(bf16), v, preferred_element_type=f32)
        o = pv if o is None else o + pv
    return o / den


def _attend_prompt(layer, qs, kvs, atts, tabm, sinks_ref, sc_scr, p_scr, *, nb, tt):
    s_len = WINDOW + tt
    lo = lax.broadcasted_iota(jnp.int32, (tt, LANES), 1) < HEAD_DIM

    for b in range(nb):
        kk16 = kvs.at[0][pl.ds(b, s_len, stride=nb), :].astype(bf16)
        for kvh in range(2):
            parts = []
            for i in range(2):
                parts += _split_heads(qs.at[2 * kvh + i][pl.ds(b, tt, stride=nb), :], lo, kvh)
            q4 = jnp.concatenate(parts, axis=0).astype(bf16)
            sc_scr[b, kvh] = lax.dot_general(q4, kk16, (((1,), (1,)), ((), ())), preferred_element_type=f32)

    for b in range(nb):
        for kvh in range(2):
            for g in range(4):
                rs = slice(g * tt, (g + 1) * tt)
                t = tabm[kvh, rs, :]
                sc = jnp.where(t > 0.5 * NEG, sc_scr[b, kvh, rs, :] * (HEAD_DIM ** -0.5) + t, NEG)
                sink = sinks_ref[layer, 4 * kvh + g]
                mx = jnp.maximum(jnp.max(sc, axis=-1, keepdims=True), sink)
                p = jnp.exp(sc - mx)
                den = jnp.sum(p, axis=-1, keepdims=True) + jnp.exp(sink - mx)
                p_scr[b, kvh, rs, :] = (p / den).astype(bf16)

    for b in range(nb):
        vv16 = kvs.at[1][pl.ds(b, s_len, stride=nb), :].astype(bf16)
        for kvh in range(2):
            o = jnp.dot(p_scr[b, kvh], vv16, preferred_element_type=f32)
            for i in range(2):
                slab = _merge_heads(o[2 * i * tt:(2 * i + 1) * tt], o[(2 * i + 1) * tt:(2 * i + 2) * tt], lo, kvh)
                atts.at[2 * kvh + i][pl.ds(b, tt, stride=nb), :] = slab


def _attend_sample(qs, kvn, atts, kc_ref, vc_ref, tab1_ref, tab2_ref, sink_ref, ko_ref, vo_ref, *, nb, tt):
    lo = lax.broadcasted_iota(jnp.int32, (SUBLANES * tt, LANES), 1) < HEAD_DIM
    for sg in range(nb // SUBLANES):
        b0 = SUBLANES * sg

        def rows(ref, b0=b0):
            return jnp.concatenate([ref[t * nb + b0:t * nb + b0 + SUBLANES, :] for t in range(tt)], axis=0)

        kc = kc_ref[b0:b0 + SUBLANES].reshape(SUBLANES * WINDOW, LANES).astype(bf16)
        vc = vc_ref[b0:b0 + SUBLANES].reshape(SUBLANES * WINDOW, LANES).astype(bf16)
        kn = rows(kvn.at[0]).astype(bf16)
        vn = rows(kvn.at[1]).astype(bf16)
        for kvh in range(2):
            parts = []
            for i in range(2):
                parts += _split_heads(rows(qs.at[2 * kvh + i]), lo, kvh)
            q4 = jnp.concatenate(parts, axis=0).astype(bf16)
            o = _softmax_pv(q4, [kc, kn], [vc, vn], [tab1_ref[kvh], tab2_ref[kvh]], sink_ref[kvh])
            r = SUBLANES * tt
            for i in range(2):
                slab = _merge_heads(o[2 * i * r:(2 * i + 1) * r], o[(2 * i + 1) * r:(2 * i + 2) * r], lo, kvh)
                for t in range(tt):
                    atts[2 * kvh + i, t * nb + b0:t * nb + b0 + SUBLANES, :] = slab[SUBLANES * t:SUBLANES * (t + 1)]
    for b in range(nb):
        ko_ref[b, 0:WINDOW - tt, :] = kc_ref[b, tt:WINDOW, :]
        vo_ref[b, 0:WINDOW - tt, :] = vc_ref[b, tt:WINDOW, :]
        ko_ref[b, WINDOW - tt:WINDOW, :] = kvn.at[0][pl.ds(b, tt, stride=nb), :]
        vo_ref[b, WINDOW - tt:WINDOW, :] = kvn.at[1][pl.ds(b, tt, stride=nb), :]


def _s5_block(ub, xs, hst, bbh_ref, cch_ref, ab_ref, *, nb, tt):
    ub16 = ub.astype(bf16)
    ys = []
    cw = HALF // 2
    for hf in range(2):
        xs[...] = jnp.dot(ub16[:, 256 * hf:256 * (hf + 1)], bbh_ref[hf], preferred_element_type=f32)
        for ck in range(2):
            re0, im0, st0 = cw * ck, HALF + cw * ck, HALF * hf + cw * ck
            ar = jnp.broadcast_to(ab_ref[2 * hf:2 * hf + 1, re0:re0 + cw], (SUBLANES, cw))
            ai = jnp.broadcast_to(ab_ref[2 * hf + 1:2 * hf + 2, re0:re0 + cw], (SUBLANES, cw))
            for bg in range(nb // SUBLANES):
                b0 = SUBLANES * bg

                hr = hst[0, b0:b0 + SUBLANES, st0:st0 + cw]
                hi = hst[1, b0:b0 + SUBLANES, st0:st0 + cw]
                for t in range(tt):
                    r0 = t * nb + b0
                    xr = xs[r0:r0 + SUBLANES, re0:re0 + cw]
                    xi = xs[r0:r0 + SUBLANES, im0:im0 + cw]
                    hr, hi = ar * hr - ai * hi + xr, ar * hi + ai * hr + xi
                    xs[r0:r0 + SUBLANES, re0:re0 + cw] = hr
                    xs[r0:r0 + SUBLANES, im0:im0 + cw] = hi
                hst[0, b0:b0 + SUBLANES, st0:st0 + cw] = hr
                hst[1, b0:b0 + SUBLANES, st0:st0 + cw] = hi
        ys.append(jnp.dot(xs[...].astype(bf16), cch_ref[hf], preferred_element_type=f32))
    return jnp.concatenate(ys, axis=1)


def _lru_block(xcs, la, lb, lst, wlru_ref, vec_ref, *, nb, tt):
    m = nb * tt
    conv = vec_ref[V_CONVB:V_CONVB + 1, :] + vec_ref[V_CONVW:V_CONVW + 1, :] * xcs[0:m, :]
    for j in range(1, CONV_WIDTH):
        conv = conv + vec_ref[V_CONVW + j:V_CONVW + j + 1, :] * xcs[j * nb:j * nb + m, :]
    c16 = conv.astype(bf16)

    def block_dot(g):
        return jnp.concatenate(
            [jnp.dot(c16[:, 256 * hf:256 * (hf + 1)], wlru_ref[g, hf], preferred_element_type=f32)
             for hf in range(2)], axis=1)

    r = jax.nn.sigmoid(block_dot(0) + vec_ref[V_BA:V_BA + 1, :])
    i = jax.nn.sigmoid(block_dot(1) + vec_ref[V_BX:V_BX + 1, :])
    log_a = LRU_C * r * jax.nn.log_sigmoid(vec_ref[V_LAM:V_LAM + 1, :])
    a = jnp.exp(log_a)
    la[...] = a
    lb[...] = conv * i * jnp.sqrt(-jnp.tanh(log_a) * (a * a + 1.0))
    for bg in range(nb // SUBLANES):
        b0 = SUBLANES * bg

        h = lst[b0:b0 + SUBLANES, :]
        for t in range(tt):
            r0 = t * nb + b0
            h = la[r0:r0 + SUBLANES, :] * h + lb[r0:r0 + SUBLANES, :]
            lb[r0:r0 + SUBLANES, :] = h
        lst[b0:b0 + SUBLANES, :] = h


def _mixer_tail(xb, win_ref, atts, xs, hst, xcs, la, lb, lst, bbh_ref, cch_ref, ab_ref, wglu_ref, wlru_ref,
                vec_ref, *, nb, tt):
    m = nb * tt

    def proj(cols):
        return jnp.dot(xb, win_ref[:, cols[0]:cols[1]], preferred_element_type=f32)

    att = jnp.concatenate([atts[j] for j in range(4)], axis=1)
    za = (att * jax.nn.silu(proj(C_GA))).astype(bf16)

    ub = proj(C_UB)
    y = _s5_block(ub, xs, hst, bbh_ref, cch_ref, ab_ref, nb=nb, tt=tt) + vec_ref[V_D:V_D + 1, :] * ub
    zz = jax.nn.gelu(y)
    zz = zz * jax.nn.sigmoid(jnp.dot(zz.astype(bf16), wglu_ref[...], preferred_element_type=f32))
    zb = (zz * jax.nn.silu(proj(C_GB))).astype(bf16)

    xcs[3 * nb:3 * nb + m, :] = proj(C_XC)
    _lru_block(xcs, la, lb, lst, wlru_ref, vec_ref, nb=nb, tt=tt)
    zc = (lb[...] * jax.nn.silu(proj(C_GC))).astype(bf16)
    return za, zb, zc


def _out_core(x, zs, wgm_ref, wbr_refs, wout_ref, ln_g, ln_b, alpha):
    xb = x.astype(bf16)
    merged = None
    for k in range(3):
        g = jax.nn.sigmoid(jnp.dot(xb, wgm_ref[:, D_MODEL * k:D_MODEL * (k + 1)], preferred_element_type=f32))
        yk = jnp.dot(zs[k], wbr_refs[k][...], preferred_element_type=f32)
        merged = g * yk if merged is None else merged + g * yk
    out = jnp.dot(merged.astype(bf16), wout_ref[...], preferred_element_type=f32)
    h = alpha * x + out
    mu = jnp.mean(h, axis=-1, keepdims=True)
    d = h - mu
    var = jnp.mean(d * d, axis=-1, keepdims=True)
    return d * lax.rsqrt(var + LN_EPS) * ln_g + ln_b


def _prompt_mixer_kernel(*refs, layer, nb, tt, nts, natural_in):
    (x_ref, win_ref, tab_ref, sinks_ref, bbh_ref, cch_ref, ab_ref, wglu_ref, wlru_ref, vec_ref), refs = refs[:10], refs[10:]
    z_ref, refs = refs[0], refs[1:]
    if natural_in:
        xtb_ref, refs = refs[0], refs[1:]
    (ko_ref, vo_ref, sre_ref, sim_ref, lruo_ref, convo_ref,
     qs, kvs, atts, xs, hst, xcs, la, lb, lst, tabm, sc_scr, p_scr), refs = refs[:18], refs[18:]
    ts = pl.program_id(0)
    m = nb * tt
    w = WINDOW * nb
    last = ts == nts - 1

    @pl.when(ts == 0)
    def _zero_state():
        kvs[:, 0:w, :] = jnp.zeros((2, w, LANES), f32)
        hst[...] = jnp.zeros(hst.shape, f32)
        lst[...] = jnp.zeros(lst.shape, f32)
        xcs[0:3 * nb, :] = jnp.zeros((3 * nb, BRANCH), f32)

    if natural_in:
        (xsl,) = refs
        for b in range(nb):
            for k in range(N_SLABS):
                xsl.at[k][pl.ds(b, tt, stride=nb), :] = x_ref[b, :, LANES * k:LANES * (k + 1)]
        x = jnp.concatenate([xsl[k] for k in range(N_SLABS)], axis=1)
        xtb_ref[...] = x
    else:
        x = x_ref[...]
    xb = x.astype(bf16)

    def proj(cols):
        return jnp.dot(xb, win_ref[:, cols[0]:cols[1]], preferred_element_type=f32)

    q = proj(C_Q)
    for j in range(4):
        qs[j] = q[:, LANES * j:LANES * (j + 1)]
    kvs[0, w:w + m, :] = proj(C_K)
    kvs[1, w:w + m, :] = proj(C_V)
    kj = lax.broadcasted_iota(jnp.int32, (2, 4 * tt, WINDOW + tt), 2)
    tabm[...] = jnp.where(kj + ts * tt >= WINDOW, tab_ref[...], NEG)
    _attend_prompt(layer, qs, kvs, atts, tabm, sinks_ref, sc_scr, p_scr, nb=nb, tt=tt)
    kvs[:, 0:w, :] = kvs[:, m:m + w, :]

    za, zb, zc = _mixer_tail(xb, win_ref, atts, xs, hst, xcs, la, lb, lst, bbh_ref, cch_ref, ab_ref, wglu_ref,
                             wlru_ref, vec_ref, nb=nb, tt=tt)
    z_ref[:, 0:BRANCH] = za
    z_ref[:, BRANCH:2 * BRANCH] = zb
    z_ref[:, 2 * BRANCH:3 * BRANCH] = zc

    @pl.when(last)
    def _store_state():
        sre_ref[...] = hst[0]
        sim_ref[...] = hst[1]
        lruo_ref[...] = lst[...]
        for t in range(3):
            convo_ref[:, BRANCH * t:BRANCH * (t + 1)] = xcs[m + t * nb:m + (t + 1) * nb, :]
        for b in range(nb):
            ko_ref[b] = kvs.at[0][pl.ds(b, WINDOW, stride=nb), :]
            vo_ref[b] = kvs.at[1][pl.ds(b, WINDOW, stride=nb), :]

    xcs[0:3 * nb, :] = xcs[m:m + 3 * nb, :]


def _prompt_out_kernel(*refs, layer, alpha, nb, natural_out):
    if natural_out:
        x_ref, z_ref, wgm_ref, wa_ref, wb_ref, wc_ref, wout_ref, lng_ref, lnb_ref, y_ref, ysl = refs
    else:
        x_ref, z_ref, wgm_ref, wa_ref, wb_ref, wc_ref, wout_ref, lng_ref, lnb_ref, y_ref = refs
    zs = [z_ref[:, BRANCH * k:BRANCH * (k + 1)] for k in range(3)]
    y = _out_core(x_ref[...], zs, wgm_ref, (wa_ref, wb_ref, wc_ref), wout_ref,
                  lng_ref[layer:layer + 1, :], lnb_ref[layer:layer + 1, :], alpha)
    if natural_out:
        tt = y.shape[0] // nb
        for k in range(N_SLABS):
            ysl[k] = y[:, LANES * k:LANES * (k + 1)]
        for b in range(nb):
            for k in range(N_SLABS):
                y_ref[b, :, LANES * k:LANES * (k + 1)] = ysl.at[k][pl.ds(b, tt, stride=nb), :]
    else:
        y_ref[...] = y


def _sample_kernel(x_ref, win_ref, wgm_ref, tab1_ref, tab2_ref, sink_ref, bbh_ref, cch_ref, ab_ref, wglu_ref, wlru_ref,
                   vec_ref, wa_ref, wb_ref, wc_ref, wout_ref, lng_ref, lnb_ref,
                   kc_ref, vc_ref, sre0_ref, sim0_ref, lru0_ref, conv0_ref,
                   y_ref, ko_ref, vo_ref, sre_ref, sim_ref, lruo_ref, convo_ref,
                   xcar, qs, kvn, atts, xs, hst, xcs, la, lb, lst, *, alpha, nb, tt):
    layer = pl.program_id(0)
    g = pl.program_id(1)
    m = nb * tt

    @pl.when(layer == 0)
    def _first_layer_input():
        xcar[g] = x_ref[...]

    x = xcar[g]
    xb = x.astype(bf16)
    hst[0] = sre0_ref[...]
    hst[1] = sim0_ref[...]
    lst[...] = lru0_ref[...]
    for t in range(3):
        xcs[t * nb:(t + 1) * nb, :] = conv0_ref[:, BRANCH * t:BRANCH * (t + 1)]

    def proj(cols):
        return jnp.dot(xb, win_ref[:, cols[0]:cols[1]], preferred_element_type=f32)

    q = proj(C_Q)
    for j in range(4):
        qs[j] = q[:, LANES * j:LANES * (j + 1)]
    kvn[0] = proj(C_K)
    kvn[1] = proj(C_V)
    _attend_sample(qs, kvn, atts, kc_ref, vc_ref, tab1_ref, tab2_ref, sink_ref, ko_ref, vo_ref, nb=nb, tt=tt)
    zs = _mixer_tail(xb, win_ref, atts, xs, hst, xcs, la, lb, lst, bbh_ref, cch_ref, ab_ref, wglu_ref,
                     wlru_ref, vec_ref, nb=nb, tt=tt)
    sre_ref[...] = hst[0]
    sim_ref[...] = hst[1]
    lruo_ref[...] = lst[...]
    for t in range(3):
        convo_ref[:, BRANCH * t:BRANCH * (t + 1)] = xcs[m + t * nb:m + (t + 1) * nb, :]

    y = _out_core(x, zs, wgm_ref, (wa_ref, wb_ref, wc_ref), wout_ref,
                  lng_ref[pl.ds(layer, 1), :], lnb_ref[pl.ds(layer, 1), :], alpha)
    xcar[g] = y
    y_ref[...] = y


def _layer_spec(shape, layer=None):
    if layer is None:
        idx = lambda l, *_: (l,) + (0,) * len(shape)
    else:
        idx = lambda *_: (layer,) + (0,) * len(shape)
    return pl.BlockSpec((None,) + tuple(shape), idx, pipeline_mode=pl.Buffered(1))


def _const_spec(shape):
    return pl.BlockSpec(tuple(shape), lambda *_: (0,) * len(shape), pipeline_mode=pl.Buffered(1))


def _mixer_weight_specs(layer):
    return [
        _layer_spec((2, 256, SSM_WIDTH), layer),
        _layer_spec((2, SSM_WIDTH, 256), layer),
        _layer_spec((4, HALF), layer),
        _layer_spec((BRANCH, BRANCH), layer),
        _layer_spec((2, 2, 256, 256), layer),
        _layer_spec((16, BRANCH), layer),
    ]


def _mixer_scratch(nb, tt, kv_rows):
    m = nb * tt
    return [
        pltpu.VMEM((4, m, LANES), f32),
        pltpu.VMEM((2, kv_rows, LANES), f32),
        pltpu.VMEM((4, m, LANES), f32),
        pltpu.VMEM((m, SSM_WIDTH), f32),
        pltpu.VMEM((2, nb, SSM_WIDTH), f32),
        pltpu.VMEM((m + 3 * nb, BRANCH), f32),
        pltpu.VMEM((m, BRANCH), f32),
        pltpu.VMEM((m, BRANCH), f32),
        pltpu.VMEM((nb, BRANCH), f32),
    ]


def _prompt_mixer_call(x, wp, tab, layer, *, nb, tt, natural_in):
    nts = (x.shape[1] if natural_in else x.shape[0] // nb) // tt
    m, w, s_len = nb * tt, WINDOW * nb, WINDOW + tt
    kern = functools.partial(_prompt_mixer_kernel, layer=layer, nb=nb, tt=tt, nts=nts, natural_in=natural_in)
    x_spec = (pl.BlockSpec((nb, tt, D_MODEL), lambda t: (0, t, 0)) if natural_in
              else pl.BlockSpec((m, D_MODEL), lambda t: (t, 0)))
    in_specs = [x_spec,
                _layer_spec((D_MODEL, C_GM[0]), layer),
                _const_spec(tab.shape), pl.BlockSpec(memory_space=pltpu.SMEM), *_mixer_weight_specs(layer)]
    whole = lambda shape: pl.BlockSpec(shape, lambda t: (0,) * len(shape))
    out_specs = [pl.BlockSpec((m, 3 * BRANCH), lambda t: (t, 0))]
    out_shape = [jax.ShapeDtypeStruct((nts * m, 3 * BRANCH), bf16)]
    if natural_in:
        out_specs.append(pl.BlockSpec((m, D_MODEL), lambda t: (t, 0)))
        out_shape.append(jax.ShapeDtypeStruct((nts * m, D_MODEL), f32))
    state_shapes = [(nb, WINDOW, LANES), (nb, WINDOW, LANES), (nb, SSM_WIDTH), (nb, SSM_WIDTH),
                    (nb, BRANCH), (nb, 3 * BRANCH)]
    out_specs += [whole(s) for s in state_shapes]
    out_shape += [jax.ShapeDtypeStruct(s, f32) for s in state_shapes]
    scratch = _mixer_scratch(nb, tt, w + m) + [
        pltpu.VMEM((2, 4 * tt, s_len), f32),
        pltpu.VMEM((nb, 2, 4 * tt, s_len), f32),
        pltpu.VMEM((nb, 2, 4 * tt, s_len), bf16),
    ]
    if natural_in:
        scratch.append(pltpu.VMEM((N_SLABS, m, LANES), f32))
    return pl.pallas_call(
        kern, grid=(nts,), in_specs=in_specs, out_specs=out_specs, out_shape=out_shape,
        scratch_shapes=scratch, name=f"mixer_prompt_{layer}",
        compiler_params=pltpu.CompilerParams(dimension_semantics=("arbitrary",), vmem_limit_bytes=VMEM_LIMIT),
    )(x, wp["win_a"], tab, wp["sinks"], wp["bbh"], wp["cch"], wp["ab"], wp["wglu"], wp["wlru"], wp["vec"])


def _prompt_out_call(x2d, z2d, wp, layer, *, nb, mb, alpha, natural_out):
    rows = x2d.shape[0]
    kern = functools.partial(_prompt_out_kernel, layer=layer, alpha=alpha, nb=nb, natural_out=natural_out)
    if natural_out:
        out_spec = pl.BlockSpec((nb, mb // nb, D_MODEL), lambda i: (0, i, 0))
        out_shape = jax.ShapeDtypeStruct((nb, rows // nb, D_MODEL), f32)
        scratch = [pltpu.VMEM((N_SLABS, mb, LANES), f32)]
    else:
        out_spec = pl.BlockSpec((mb, D_MODEL), lambda i: (i, 0))
        out_shape = jax.ShapeDtypeStruct((rows, D_MODEL), f32)
        scratch = []
    return pl.pallas_call(
        kern, grid=(rows // mb,),
        in_specs=[
            pl.BlockSpec((mb, D_MODEL), lambda i: (i, 0)),
            pl.BlockSpec((mb, 3 * BRANCH), lambda i: (i, 0)),
            _layer_spec((D_MODEL, 3 * D_MODEL), layer),
            _layer_spec((BRANCH, D_MODEL), layer), _layer_spec((BRANCH, D_MODEL), layer),
            _layer_spec((BRANCH, D_MODEL), layer),
            _layer_spec((D_MODEL, D_MODEL), layer),
            _const_spec(wp["ln_g"].shape), _const_spec(wp["ln_b"].shape),
        ],
        out_specs=out_spec, out_shape=out_shape, scratch_shapes=scratch, name=f"out_prompt_{layer}",
        compiler_params=pltpu.CompilerParams(dimension_semantics=("arbitrary",), vmem_limit_bytes=VMEM_LIMIT),
    )(x2d, z2d, wp["win_g"], wp["wbr_a"], wp["wbr_b"], wp["wbr_c"], wp["wout"], wp["ln_g"], wp["ln_b"])


def _sample_call(x2d, wp, tabs, state, *, depth, nbg, nb, tt, alpha):
    m = nb * tt
    kern = functools.partial(_sample_kernel, alpha=alpha, nb=nb, tt=tt)
    row_spec = pl.BlockSpec((m, D_MODEL), lambda l, g: (g, 0))
    st = lambda *shape: pl.BlockSpec((nb,) + shape, lambda l, g: (l * nbg + g,) + (0,) * len(shape))
    state_specs = [st(WINDOW, LANES), st(WINDOW, LANES), st(SSM_WIDTH), st(SSM_WIDTH), st(BRANCH), st(3 * BRANCH)]
    state_shapes = [(WINDOW, LANES), (WINDOW, LANES), (SSM_WIDTH,), (SSM_WIDTH,), (BRANCH,), (3 * BRANCH,)]
    in_specs = [row_spec, _layer_spec((D_MODEL, C_GM[0])), _layer_spec((D_MODEL, 3 * D_MODEL)),
                _const_spec(tabs[0].shape), _const_spec(tabs[1].shape),
                _layer_spec(wp["sinkcol"].shape[1:]), *_mixer_weight_specs(None),
                _layer_spec((BRANCH, D_MODEL)), _layer_spec((BRANCH, D_MODEL)), _layer_spec((BRANCH, D_MODEL)),
                _layer_spec((D_MODEL, D_MODEL)), _const_spec(wp["ln_g"].shape), _const_spec(wp["ln_b"].shape),
                *state_specs]
    out_specs = [pl.BlockSpec((m, D_MODEL), lambda l, g: (l * nbg + g, 0)), *state_specs]
    out_shape = [jax.ShapeDtypeStruct((depth * nbg * m, D_MODEL), f32)] + [
        jax.ShapeDtypeStruct((depth * nbg * nb,) + s, f32) for s in state_shapes]
    scratch = [pltpu.VMEM((nbg, m, D_MODEL), f32)] + _mixer_scratch(nb, tt, m)
    return pl.pallas_call(
        kern, grid=(depth, nbg), in_specs=in_specs, out_specs=out_specs, out_shape=out_shape,
        scratch_shapes=scratch, name="sample_layers",
        compiler_params=pltpu.CompilerParams(
            dimension_semantics=("arbitrary", "arbitrary"), vmem_limit_bytes=VMEM_LIMIT),
    )(x2d, wp["win_a"], wp["win_g"], tabs[0], tabs[1], wp["sinkcol"], wp["bbh"], wp["cch"], wp["ab"], wp["wglu"], wp["wlru"],
      wp["vec"], wp["wbr_a"], wp["wbr_b"], wp["wbr_c"], wp["wout"], wp["ln_g"], wp["ln_b"], *state)


def _t5_bucket(dist):
    max_exact = N_BUCKETS // 2
    d = jnp.maximum(dist, 0)
    large = max_exact + (jnp.log(jnp.maximum(d, 1).astype(f32) / max_exact)
                         / math.log(MAX_DISTANCE / max_exact) * (N_BUCKETS - max_exact)).astype(jnp.int32)
    large = jnp.minimum(large, N_BUCKETS - 1)
    return jnp.where(d < max_exact, d, large)


def _band_bias(rel_bias, tt):
    s_len = WINDOW + tt
    by_dist = rel_bias.astype(f32)[_t5_bucket(jnp.arange(WINDOW))].T
    row0 = jnp.concatenate([jnp.full((N_Q_HEADS, 1), NEG, f32), by_dist[:, ::-1],
                            jnp.full((N_Q_HEADS, tt), NEG, f32)], axis=1)
    flat = jnp.tile(row0, (1, tt))[:, :tt * s_len]
    return flat.reshape(N_Q_HEADS, tt, s_len)


def _prompt_table(rel_bias, tt):
    return _band_bias(rel_bias, tt).reshape(2, 4 * tt, WINDOW + tt)


def _sample_tables(rel_bias, tt):
    bs = _band_bias(rel_bias, tt).reshape(2, 4, tt, WINDOW + tt)
    same = jnp.eye(SUBLANES, dtype=bool)
    t1 = jnp.where(same[None, None, None, :, :, None], bs[:, :, :, None, None, :WINDOW], NEG)
    t2 = jnp.where(same[None, None, None, :, None, :], bs[:, :, :, None, WINDOW:, None], NEG)
    rows = 4 * tt * SUBLANES
    return t1.reshape(2, rows, SUBLANES * WINDOW), t2.reshape(2, rows, tt * SUBLANES)


def _weights(s_tt, w_in, sinks, w_branch_a, lam_re, lam_im, log_step, b_re, b_im, c_re, c_im, ssm_d, w_glu,
             w_branch_b, conv_w, conv_b, w_a, b_a, w_x, b_x, lam, w_branch_c, w_out, ln_g, ln_b):
    depth = w_in.shape[0]
    hg = SSM_GROUPS // 2
    lr = jnp.minimum(lam_re.astype(f32), -1e-4)
    li = lam_im.astype(f32)
    step = jnp.exp(log_step.astype(f32))[..., None]
    mag = jnp.exp(lr * step)
    ab_re = mag * jnp.cos(li * step)
    ab_im = mag * jnp.sin(li * step)
    den = lr * lr + li * li
    f_re = ((ab_re - 1.0) * lr + ab_im * li) / den
    f_im = (ab_im * lr - (ab_re - 1.0) * li) / den
    bre, bim = b_re.astype(f32), b_im.astype(f32)
    bb_re = f_re[..., None] * bre - f_im[..., None] * bim
    bb_im = f_re[..., None] * bim + f_im[..., None] * bre
    eye = jnp.eye(hg, dtype=f32)
    drv = jnp.stack([bb_re, bb_im], axis=1).reshape(depth, 2, 2, hg, SSM_STATE, SSM_GROUP)
    drv = drv.transpose(0, 2, 3, 5, 1, 4)[:, :, :, :, :, None, :] * eye[None, None, :, None, None, :, None]
    bbh = drv.reshape(depth, 2, hg * SSM_GROUP, SSM_WIDTH).astype(bf16)
    rdo = jnp.stack([c_re.astype(f32), -c_im.astype(f32)], axis=1).reshape(depth, 2, 2, hg, SSM_GROUP, SSM_STATE)
    rdo = rdo.transpose(0, 2, 1, 3, 5, 4)[:, :, :, :, :, None, :] * eye[None, None, None, :, None, :, None]
    cch = rdo.reshape(depth, 2, SSM_WIDTH, hg * SSM_GROUP).astype(bf16)
    ab = jnp.stack([ab_re.reshape(depth, 2, HALF), ab_im.reshape(depth, 2, HALF)], axis=2).reshape(depth, 4, HALF)
    eye4 = jnp.eye(LRU_BLOCKS // 2, dtype=f32)
    blk = jnp.stack([w_a.astype(f32), w_x.astype(f32)], axis=1).reshape(depth, 2, 2, LRU_BLOCKS // 2, 64, 64)
    wlru = (blk[:, :, :, :, :, None, :] * eye4[None, None, None, :, None, :, None]).reshape(depth, 2, 2, 256, 256)
    row = lambda v: v.astype(f32)[:, None, :]
    vec = jnp.concatenate([conv_w.astype(f32), row(conv_b), row(b_a), row(b_x), row(lam), row(ssm_d),
                           jnp.zeros((depth, 16 - V_D - 1, BRANCH), f32)], axis=1)
    sk = sinks.astype(f32).reshape(depth, 2, 4, 1)
    return {
        "win_a": w_in[:, :, :C_GM[0]].astype(bf16),
        "win_g": w_in[:, :, C_GM[0]:].astype(bf16),
        "sinks": sinks.astype(f32),
        "sinkcol": jnp.broadcast_to(sk, (depth, 2, 4, s_tt * SUBLANES)).reshape(depth, 2, 4 * s_tt * SUBLANES, 1),
        "bbh": bbh, "cch": cch, "ab": ab,
        "wglu": w_glu.astype(bf16),
        "wlru": wlru.astype(bf16),
        "vec": vec,
        "wbr_a": w_branch_a.astype(bf16), "wbr_b": w_branch_b.astype(bf16), "wbr_c": w_branch_c.astype(bf16),
        "wout": w_out.astype(bf16),
        "ln_g": ln_g.astype(f32), "ln_b": ln_b.astype(f32),
    }


def _group_rows(x, nbg, nb):
    b, t, c = x.shape
    return x.reshape(nbg, nb, t, c).transpose(0, 2, 1, 3).reshape(nbg * t * nb, c)


def _ungroup_rows(x2d, nbg, nb, t):
    c = x2d.shape[-1]
    return x2d.reshape(nbg, t, nb, c).transpose(0, 2, 1, 3).reshape(nbg * nb, t, c)


def _states_out(lead, ko, vo, sre, sim, lru, conv):
    return (ko.reshape(*lead, WINDOW, 2, HEAD_DIM), vo.reshape(*lead, WINDOW, 2, HEAD_DIM),
            sre.reshape(*lead, SSM_GROUPS, SSM_STATE), sim.reshape(*lead, SSM_GROUPS, SSM_STATE),
            lru.reshape(*lead, BRANCH), conv.reshape(*lead, CONV_WIDTH - 1, BRANCH))


P_TT = 64
OUT_ROWS = 1024
S_NBG = 4


def kernel(x_prompt, x_sample, cache_k, cache_v, state_ssm_re, state_ssm_im, state_lru, state_conv, rel_bias, w_in, sinks, w_branch_a, ssm_lambda_re, ssm_lambda_im, ssm_log_step, ssm_b_re, ssm_b_im, ssm_c_re, ssm_c_im, ssm_d, ssm_w_glu, w_branch_b, conv_w, conv_b, lru_w_a, lru_b_a, lru_w_x, lru_b_x, lru_lambda, w_branch_c, w_out, ln_g, ln_b):
    depth = w_in.shape[0]
    alpha = (2 * depth) ** 0.25
    pb, pt, _ = x_prompt.shape
    sb, st, _ = x_sample.shape
    s_nb = sb // S_NBG

    wp = _weights(st, w_in, sinks, w_branch_a, ssm_lambda_re, ssm_lambda_im, ssm_log_step, ssm_b_re, ssm_b_im,
                  ssm_c_re, ssm_c_im, ssm_d, ssm_w_glu, w_branch_b, conv_w, conv_b, lru_w_a, lru_b_a, lru_w_x,
                  lru_b_x, lru_lambda, w_branch_c, w_out, ln_g, ln_b)
    p_tab = _prompt_table(rel_bias, P_TT)
    s_tabs = _sample_tables(rel_bias, st)

    st_p = []
    x = x_prompt
    for l in range(depth):
        outs = _prompt_mixer_call(x, wp, p_tab, l, nb=pb, tt=P_TT, natural_in=(l == 0))
        if l == 0:
            z, x, *sp = outs
        else:
            z, *sp = outs
        x = _prompt_out_call(x, z, wp, l, nb=pb, mb=OUT_ROWS, alpha=alpha, natural_out=(l == depth - 1))
        st_p.append(_states_out((pb,), *sp))
    y_prompt = x
    stacked_p = [jnp.stack([s[k] for s in st_p]) for k in range(6)]

    n = depth * sb
    s_state = (cache_k.astype(f32).reshape(n, WINDOW, LANES), cache_v.astype(f32).reshape(n, WINDOW, LANES),
               state_ssm_re.astype(f32).reshape(n, SSM_WIDTH), state_ssm_im.astype(f32).reshape(n, SSM_WIDTH),
               state_lru.astype(f32).reshape(n, BRANCH), state_conv.astype(f32).reshape(n, 3 * BRANCH))
    ys, *ss = _sample_call(_group_rows(x_sample, S_NBG, s_nb), wp, s_tabs, s_state, depth=depth, nbg=S_NBG,
                           nb=s_nb, tt=st, alpha=alpha)
    y_sample = _ungroup_rows(ys[(depth - 1) * sb * st:], S_NBG, s_nb, st)
    stacked_s = _states_out((depth, sb), *ss)
    return (y_prompt, y_sample, *stacked_p, *stacked_s)
```

```python
import functools
import math

import jax
import jax.numpy as jnp
from jax import lax
from jax.experimental import pallas as pl
from jax.experimental.pallas import tpu as pltpu

f32 = jnp.float32
bf16 = jnp.bfloat16

D_MODEL = 1024
BRANCH = 512
HEAD_DIM = 64
N_Q_HEADS = 8
WINDOW = 128
N_BUCKETS = 32
MAX_DISTANCE = 128
SSM_GROUPS = 32
SSM_GROUP = 16
SSM_STATE = 64
LRU_BLOCKS = 8
CONV_WIDTH = 4
LRU_C = 8.0
LN_EPS = 1e-5
NEG = -1e30

LANES = 128
SUBLANES = 8
VMEM_LIMIT = 56 * 1024 * 1024

C_Q, C_K, C_V, C_GA, C_UB, C_GB, C_XC, C_GC, C_GM = (
    (0, 512), (512, 640), (640, 768), (768, 1280), (1280, 1792), (1792, 2304),
    (2304, 2816), (2816, 3328), (3328, 6400))
IN_COLS = C_GM[1]
HALF = SSM_GROUPS // 2 * SSM_STATE
SSM_WIDTH = 2 * HALF
N_SLABS = D_MODEL // LANES

V_CONVW, V_CONVB, V_BA, V_BX, V_LAM, V_D = 0, 4, 5, 6, 7, 8


def _split_heads(qa, lo, kvh):
    q_lo = jnp.where(lo, qa, 0.0)
    q_hi = jnp.where(lo, 0.0, qa)
    if kvh == 0:
        return [q_lo, pltpu.roll(q_hi, HEAD_DIM, axis=1)]
    return [pltpu.roll(q_lo, HEAD_DIM, axis=1), q_hi]


def _merge_heads(o0, o1, lo, kvh):
    if kvh == 0:
        return jnp.where(lo, o0, pltpu.roll(o1, HEAD_DIM, axis=1))
    return jnp.where(lo, pltpu.roll(o0, HEAD_DIM, axis=1), o1)


def _softmax_pv(q4, keys, vals, tabs, sink):
    scs = []
    for k, t in zip(keys, tabs):
        sc = lax.dot_general(q4, k, (((1,), (1,)), ((), ())), preferred_element_type=f32)
        scs.append(jnp.where(t > 0.5 * NEG, sc * (HEAD_DIM ** -0.5) + t, NEG))
    mx = sink
    for sc in scs:
        mx = jnp.maximum(mx, jnp.max(sc, axis=-1, keepdims=True))
    den = jnp.exp(sink - mx)
    o = None
    for sc, v in zip(scs, vals):
        p = jnp.exp(sc - mx)
        den = den + jnp.sum(p, axis=-1, keepdims=True)
        pv = jnp.dot(p.astype(bf16), v, preferred_element_type=f32)
        o = pv if o is None else o + pv
    return o / den


def _attend_prompt(layer, qs, kvs, atts, tabm, sinks_ref, sc_scr, p_scr, *, nb, tt):
    s_len = WINDOW + tt
    lo = lax.broadcasted_iota(jnp.int32, (tt, LANES), 1) < HEAD_DIM

    for b in range(nb):
        kk16 = kvs.at[0][pl.ds(b, s_len, stride=nb), :].astype(bf16)
        for kvh in range(2):
            parts = []
            for i in range(2):
                parts += _split_heads(qs.at[2 * kvh + i][pl.ds(b, tt, stride=nb), :], lo, kvh)
            q4 = jnp.concatenate(parts, axis=0).astype(bf16)
            sc_scr[b, kvh] = lax.dot_general(q4, kk16, (((1,), (1,)), ((), ())), preferred_element_type=f32)

    for b in range(nb):
        for kvh in range(2):
            for g in range(4):
                rs = slice(g * tt, (g + 1) * tt)
                t = tabm[kvh, rs, :]
                sc = jnp.where(t > 0.5 * NEG, sc_scr[b, kvh, rs, :] + t, NEG)
                sink = sinks_ref[layer, 4 * kvh + g]
                mx = jnp.maximum(jnp.max(sc, axis=-1, keepdims=True), sink)
                p = jnp.exp(sc - mx)
                den = jnp.sum(p, axis=-1, keepdims=True) + jnp.exp(sink - mx)
                p_scr[b, kvh, rs, :] = (p / den).astype(bf16)

    for b in range(nb):
        vv16 = kvs.at[1][pl.ds(b, s_len, stride=nb), :].astype(bf16)
        for kvh in range(2):
            o = jnp.dot(p_scr[b, kvh], vv16, preferred_element_type=f32)
            for i in range(2):
                slab = _merge_heads(o[2 * i * tt:(2 * i + 1) * tt], o[(2 * i + 1) * tt:(2 * i + 2) * tt], lo, kvh)
                atts.at[2 * kvh + i][pl.ds(b, tt, stride=nb), :] = slab


def _attend_sample(qs, kvn, atts, kc_ref, vc_ref, tab1_ref, tab2_ref, sink_ref, ko_ref, vo_ref, *, nb, tt):
    lo = lax.broadcasted_iota(jnp.int32, (SUBLANES * tt, LANES), 1) < HEAD_DIM
    for sg in range(nb // SUBLANES):
        b0 = SUBLANES * sg

        def rows(ref, b0=b0):
            return jnp.concatenate([ref[t * nb + b0:t * nb + b0 + SUBLANES, :] for t in range(tt)], axis=0)

        kc = kc_ref[b0:b0 + SUBLANES].reshape(SUBLANES * WINDOW, LANES).astype(bf16)
        vc = vc_ref[b0:b0 + SUBLANES].reshape(SUBLANES * WINDOW, LANES).astype(bf16)
        kn = rows(kvn.at[0]).astype(bf16)
        vn = rows(kvn.at[1]).astype(bf16)
        for kvh in range(2):
            parts = []
            for i in range(2):
                parts += _split_heads(rows(qs.at[2 * kvh + i]), lo, kvh)
            q4 = jnp.concatenate(parts, axis=0).astype(bf16)
            o = _softmax_pv(q4, [kc, kn], [vc, vn], [tab1_ref[kvh], tab2_ref[kvh]], sink_ref[kvh])
            r = SUBLANES * tt
            for i in range(2):
                slab = _merge_heads(o[2 * i * r:(2 * i + 1) * r], o[(2 * i + 1) * r:(2 * i + 2) * r], lo, kvh)
                for t in range(tt):
                    atts[2 * kvh + i, t * nb + b0:t * nb + b0 + SUBLANES, :] = slab[SUBLANES * t:SUBLANES * (t + 1)]
    for b in range(nb):
        ko_ref[b, 0:WINDOW - tt, :] = kc_ref[b, tt:WINDOW, :]
        vo_ref[b, 0:WINDOW - tt, :] = vc_ref[b, tt:WINDOW, :]
        ko_ref[b, WINDOW - tt:WINDOW, :] = kvn.at[0][pl.ds(b, tt, stride=nb), :]
        vo_ref[b, WINDOW - tt:WINDOW, :] = kvn.at[1][pl.ds(b, tt, stride=nb), :]


def _s5_block(ub, xs, hst, bbh_ref, cch_ref, ab_ref, *, nb, tt):
    ub16 = ub.astype(bf16)
    ys = []
    cw = HALF // 2
    for hf in range(2):
        xs[...] = jnp.dot(ub16[:, 256 * hf:256 * (hf + 1)], bbh_ref[hf], preferred_element_type=f32)
        for ck in range(2):
            re0, im0, st0 = cw * ck, HALF + cw * ck, HALF * hf + cw * ck
            ar = jnp.broadcast_to(ab_ref[2 * hf:2 * hf + 1, re0:re0 + cw], (SUBLANES, cw))
            ai = jnp.broadcast_to(ab_ref[2 * hf + 1:2 * hf + 2, re0:re0 + cw], (SUBLANES, cw))
            for bg in range(nb // SUBLANES):
                b0 = SUBLANES * bg

                hr = hst[0, b0:b0 + SUBLANES, st0:st0 + cw]
                hi = hst[1, b0:b0 + SUBLANES, st0:st0 + cw]
                for t in range(tt):
                    r0 = t * nb + b0
                    xr = xs[r0:r0 + SUBLANES, re0:re0 + cw]
                    xi = xs[r0:r0 + SUBLANES, im0:im0 + cw]
                    hr, hi = ar * hr - ai * hi + xr, ar * hi + ai * hr + xi
                    xs[r0:r0 + SUBLANES, re0:re0 + cw] = hr
                    xs[r0:r0 + SUBLANES, im0:im0 + cw] = hi
                hst[0, b0:b0 + SUBLANES, st0:st0 + cw] = hr
                hst[1, b0:b0 + SUBLANES, st0:st0 + cw] = hi
        ys.append(jnp.dot(xs[...].astype(bf16), cch_ref[hf], preferred_element_type=f32))
    return jnp.concatenate(ys, axis=1)


def _lru_block(xcs, la, lb, lst, wlru_ref, vec_ref, *, nb, tt):
    m = nb * tt
    conv = vec_ref[V_CONVB:V_CONVB + 1, :] + vec_ref[V_CONVW:V_CONVW + 1, :] * xcs[0:m, :]
    for j in range(1, CONV_WIDTH):
        conv = conv + vec_ref[V_CONVW + j:V_CONVW + j + 1, :] * xcs[j * nb:j * nb + m, :]
    c16 = conv.astype(bf16)

    def block_dot(g):
        return jnp.concatenate(
            [jnp.dot(c16[:, 256 * hf:256 * (hf + 1)], wlru_ref[g, hf], preferred_element_type=f32)
             for hf in range(2)], axis=1)

    r = jax.nn.sigmoid(block_dot(0) + vec_ref[V_BA:V_BA + 1, :])
    i = jax.nn.sigmoid(block_dot(1) + vec_ref[V_BX:V_BX + 1, :])
    log_a = LRU_C * r * jax.nn.log_sigmoid(vec_ref[V_LAM:V_LAM + 1, :])
    a = jnp.exp(log_a)
    la[...] = a
    g2 = -jnp.tanh(log_a) * (a * a + 1.0)
    lb[...] = conv * i * jnp.where(g2 > 0.0, g2 * lax.rsqrt(g2), 0.0)
    for bg in range(nb // SUBLANES):
        b0 = SUBLANES * bg

        h = lst[b0:b0 + SUBLANES, :]
        for t in range(tt):
            r0 = t * nb + b0
            h = la[r0:r0 + SUBLANES, :] * h + lb[r0:r0 + SUBLANES, :]
            lb[r0:r0 + SUBLANES, :] = h
        lst[b0:b0 + SUBLANES, :] = h


def _mixer_tail(xb, win_ref, atts, xs, hst, xcs, la, lb, lst, bbh_ref, cch_ref, ab_ref, wglu_ref, wlru_ref,
                vec_ref, *, nb, tt):
    m = nb * tt

    def proj(cols):
        return jnp.dot(xb, win_ref[:, cols[0]:cols[1]], preferred_element_type=f32)

    att = jnp.concatenate([atts[j] for j in range(4)], axis=1)
    za = (att * jax.nn.silu(proj(C_GA))).astype(bf16)

    ub = proj(C_UB)
    y = _s5_block(ub, xs, hst, bbh_ref, cch_ref, ab_ref, nb=nb, tt=tt) + vec_ref[V_D:V_D + 1, :] * ub
    zz = jax.nn.gelu(y)
    zz = zz * jax.nn.sigmoid(jnp.dot(zz.astype(bf16), wglu_ref[...], preferred_element_type=f32))
    zb = (zz * jax.nn.silu(proj(C_GB))).astype(bf16)

    xcs[3 * nb:3 * nb + m, :] = proj(C_XC)
    _lru_block(xcs, la, lb, lst, wlru_ref, vec_ref, nb=nb, tt=tt)
    zc = (lb[...] * jax.nn.silu(proj(C_GC))).astype(bf16)
    return za, zb, zc


def _merge_gate(xb, win_ref, k):
    c0 = C_GM[0] + D_MODEL * k
    return jax.nn.sigmoid(jnp.dot(xb, win_ref[:, c0:c0 + D_MODEL], preferred_element_type=f32))


def _out_core(x, zs, gates, wbr_refs, wout_ref, ln_g, ln_b, alpha):
    merged = None
    for k in range(3):
        g = gates(k)
        yk = jnp.dot(zs[k], wbr_refs[k][...], preferred_element_type=f32)
        merged = g * yk if merged is None else merged + g * yk
    out = jnp.dot(merged.astype(bf16), wout_ref[...], preferred_element_type=f32)
    h = alpha * x + out
    mu = jnp.mean(h, axis=-1, keepdims=True)
    d = h - mu
    var = jnp.mean(d * d, axis=-1, keepdims=True)
    return d * lax.rsqrt(var + LN_EPS) * ln_g + ln_b


def _prompt_mixer_kernel(*refs, layer, nb, tt, nts, natural_in):
    (x_ref, win_ref, tab_ref, sinks_ref, bbh_ref, cch_ref, ab_ref, wglu_ref, wlru_ref, vec_ref), refs = refs[:10], refs[10:]
    (z_ref, gm_ref), refs = refs[:2], refs[2:]
    if natural_in:
        xtb_ref, refs = refs[0], refs[1:]
    (ko_ref, vo_ref, sre_ref, sim_ref, lruo_ref, convo_ref,
     qs, kvs, atts, xs, hst, xcs, la, lb, lst, tabm, sc_scr, p_scr), refs = refs[:18], refs[18:]
    ts = pl.program_id(0)
    m = nb * tt
    w = WINDOW * nb
    last = ts == nts - 1

    @pl.when(ts == 0)
    def _zero_state():
        kvs[:, 0:w, :] = jnp.zeros((2, w, LANES), f32)
        hst[...] = jnp.zeros(hst.shape, f32)
        lst[...] = jnp.zeros(lst.shape, f32)
        xcs[0:3 * nb, :] = jnp.zeros((3 * nb, BRANCH), f32)

    if natural_in:
        (xsl,) = refs
        for b in range(nb):
            for k in range(N_SLABS):
                xsl.at[k][pl.ds(b, tt, stride=nb), :] = x_ref[b, :, LANES * k:LANES * (k + 1)]
        x = jnp.concatenate([xsl[k] for k in range(N_SLABS)], axis=1)
        xtb_ref[...] = x
    else:
        x = x_ref[...]
    xb = x.astype(bf16)

    def proj(cols):
        return jnp.dot(xb, win_ref[:, cols[0]:cols[1]], preferred_element_type=f32)

    q = proj(C_Q)
    for j in range(4):
        qs[j] = q[:, LANES * j:LANES * (j + 1)] * (HEAD_DIM ** -0.5)
    kv = proj((C_K[0], C_V[1]))
    kvs[0, w:w + m, :] = kv[:, :LANES]
    kvs[1, w:w + m, :] = kv[:, LANES:]
    kj = lax.broadcasted_iota(jnp.int32, (2, 4 * tt, WINDOW + tt), 2)
    tabm[...] = jnp.where(kj + ts * tt >= WINDOW, tab_ref[...], NEG)
    _attend_prompt(layer, qs, kvs, atts, tabm, sinks_ref, sc_scr, p_scr, nb=nb, tt=tt)
    kvs[:, 0:w, :] = kvs[:, m:m + w, :]

    za, zb, zc = _mixer_tail(xb, win_ref, atts, xs, hst, xcs, la, lb, lst, bbh_ref, cch_ref, ab_ref, wglu_ref,
                             wlru_ref, vec_ref, nb=nb, tt=tt)
    z_ref[:, 0:BRANCH] = za
    z_ref[:, BRANCH:2 * BRANCH] = zb
    z_ref[:, 2 * BRANCH:3 * BRANCH] = zc
    for k in range(3):
        gm_ref[:, D_MODEL * k:D_MODEL * (k + 1)] = _merge_gate(xb, win_ref, k).astype(bf16)

    @pl.when(last)
    def _store_state():
        sre_ref[...] = hst[0]
        sim_ref[...] = hst[1]
        lruo_ref[...] = lst[...]
        for t in range(3):
            convo_ref[:, BRANCH * t:BRANCH * (t + 1)] = xcs[m + t * nb:m + (t + 1) * nb, :]
        for b in range(nb):
            ko_ref[b] = kvs.at[0][pl.ds(b, WINDOW, stride=nb), :]
            vo_ref[b] = kvs.at[1][pl.ds(b, WINDOW, stride=nb), :]

    xcs[0:3 * nb, :] = xcs[m:m + 3 * nb, :]


def _prompt_out_kernel(*refs, layer, alpha, nb, natural_out):
    if natural_out:
        x_ref, z_ref, gm_ref, wa_ref, wb_ref, wc_ref, wout_ref, lng_ref, lnb_ref, y_ref, ysl = refs
    else:
        x_ref, z_ref, gm_ref, wa_ref, wb_ref, wc_ref, wout_ref, lng_ref, lnb_ref, y_ref = refs
    zs = [z_ref[:, BRANCH * k:BRANCH * (k + 1)] for k in range(3)]
    gates = lambda k: gm_ref[:, D_MODEL * k:D_MODEL * (k + 1)].astype(f32)
    y = _out_core(x_ref[...], zs, gates, (wa_ref, wb_ref, wc_ref), wout_ref,
                  lng_ref[layer:layer + 1, :], lnb_ref[layer:layer + 1, :], alpha)
    if natural_out:
        tt = y.shape[0] // nb
        for k in range(N_SLABS):
            ysl[k] = y[:, LANES * k:LANES * (k + 1)]
        for b in range(nb):
            for k in range(N_SLABS):
                y_ref[b, :, LANES * k:LANES * (k + 1)] = ysl.at[k][pl.ds(b, tt, stride=nb), :]
    else:
        y_ref[...] = y


def _sample_kernel(x_ref, win_ref, tab1_ref, tab2_ref, sink_ref, bbh_ref, cch_ref, ab_ref, wglu_ref, wlru_ref,
                   vec_ref, wa_ref, wb_ref, wc_ref, wout_ref, lng_ref, lnb_ref,
                   kc_ref, vc_ref, sre0_ref, sim0_ref, lru0_ref, conv0_ref,
                   y_ref, ko_ref, vo_ref, sre_ref, sim_ref, lruo_ref, convo_ref,
                   xcar, qs, kvn, atts, xs, hst, xcs, la, lb, lst, *, alpha, nb, tt):
    layer = pl.program_id(0)
    g = pl.program_id(1)
    m = nb * tt

    @pl.when(layer == 0)
    def _first_layer_input():
        xcar[g] = x_ref[...]

    x = xcar[g]
    xb = x.astype(bf16)
    hst[0] = sre0_ref[...]
    hst[1] = sim0_ref[...]
    lst[...] = lru0_ref[...]
    for t in range(3):
        xcs[t * nb:(t + 1) * nb, :] = conv0_ref[:, BRANCH * t:BRANCH * (t + 1)]

    def proj(cols):
        return jnp.dot(xb, win_ref[:, cols[0]:cols[1]], preferred_element_type=f32)

    q = proj(C_Q)
    for j in range(4):
        qs[j] = q[:, LANES * j:LANES * (j + 1)]
    kv = proj((C_K[0], C_V[1]))
    kvn[0] = kv[:, :LANES]
    kvn[1] = kv[:, LANES:]
    _attend_sample(qs, kvn, atts, kc_ref, vc_ref, tab1_ref, tab2_ref, sink_ref, ko_ref, vo_ref, nb=nb, tt=tt)
    zs = _mixer_tail(xb, win_ref, atts, xs, hst, xcs, la, lb, lst, bbh_ref, cch_ref, ab_ref, wglu_ref,
                     wlru_ref, vec_ref, nb=nb, tt=tt)
    sre_ref[...] = hst[0]
    sim_ref[...] = hst[1]
    lruo_ref[...] = lst[...]
    for t in range(3):
        convo_ref[:, BRANCH * t:BRANCH * (t + 1)] = xcs[m + t * nb:m + (t + 1) * nb, :]

    y = _out_core(x, zs, functools.partial(_merge_gate, xb, win_ref), (wa_ref, wb_ref, wc_ref), wout_ref,
                  lng_ref[pl.ds(layer, 1), :], lnb_ref[pl.ds(layer, 1), :], alpha)
    xcar[g] = y
    y_ref[...] = y


def _layer_spec(shape, layer=None):
    if layer is None:
        idx = lambda l, *_: (l,) + (0,) * len(shape)
    else:
        idx = lambda *_: (layer,) + (0,) * len(shape)
    return pl.BlockSpec((None,) + tuple(shape), idx, pipeline_mode=pl.Buffered(1))


def _const_spec(shape):
    return pl.BlockSpec(tuple(shape), lambda *_: (0,) * len(shape), pipeline_mode=pl.Buffered(1))


def _mixer_weight_specs(layer):
    return [
        _layer_spec((2, 256, SSM_WIDTH), layer),
        _layer_spec((2, SSM_WIDTH, 256), layer),
        _layer_spec((4, HALF), layer),
        _layer_spec((BRANCH, BRANCH), layer),
        _layer_spec((2, 2, 256, 256), layer),
        _layer_spec((16, BRANCH), layer),
    ]


def _mixer_scratch(nb, tt, kv_rows):
    m = nb * tt
    return [
        pltpu.VMEM((4, m, LANES), f32),
        pltpu.VMEM((2, kv_rows, LANES), f32),
        pltpu.VMEM((4, m, LANES), f32),
        pltpu.VMEM((m, SSM_WIDTH), f32),
        pltpu.VMEM((2, nb, SSM_WIDTH), f32),
        pltpu.VMEM((m + 3 * nb, BRANCH), f32),
        pltpu.VMEM((m, BRANCH), f32),
        pltpu.VMEM((m, BRANCH), f32),
        pltpu.VMEM((nb, BRANCH), f32),
    ]


def _prompt_mixer_call(x, wp, tab, layer, *, nb, tt, natural_in):
    nts = (x.shape[1] if natural_in else x.shape[0] // nb) // tt
    m, w, s_len = nb * tt, WINDOW * nb, WINDOW + tt
    kern = functools.partial(_prompt_mixer_kernel, layer=layer, nb=nb, tt=tt, nts=nts, natural_in=natural_in)
    x_spec = (pl.BlockSpec((nb, tt, D_MODEL), lambda t: (0, t, 0)) if natural_in
              else pl.BlockSpec((m, D_MODEL), lambda t: (t, 0)))
    in_specs = [x_spec,
                _layer_spec((D_MODEL, IN_COLS), layer),
                _const_spec(tab.shape), pl.BlockSpec(memory_space=pltpu.SMEM), *_mixer_weight_specs(layer)]
    whole = lambda shape: pl.BlockSpec(shape, lambda t: (0,) * len(shape))
    out_specs = [pl.BlockSpec((m, 3 * BRANCH), lambda t: (t, 0)), pl.BlockSpec((m, 3 * D_MODEL), lambda t: (t, 0))]
    out_shape = [jax.ShapeDtypeStruct((nts * m, 3 * BRANCH), bf16),
                 jax.ShapeDtypeStruct((nts * m, 3 * D_MODEL), bf16)]
    if natural_in:
        out_specs.append(pl.BlockSpec((m, D_MODEL), lambda t: (t, 0)))
        out_shape.append(jax.ShapeDtypeStruct((nts * m, D_MODEL), f32))
    state_shapes = [(nb, WINDOW, LANES), (nb, WINDOW, LANES), (nb, SSM_WIDTH), (nb, SSM_WIDTH),
                    (nb, BRANCH), (nb, 3 * BRANCH)]
    out_specs += [whole(s) for s in state_shapes]
    out_shape += [jax.ShapeDtypeStruct(s, f32) for s in state_shapes]
    scratch = _mixer_scratch(nb, tt, w + m) + [
        pltpu.VMEM((2, 4 * tt, s_len), f32),
        pltpu.VMEM((nb, 2, 4 * tt, s_len), f32),
        pltpu.VMEM((nb, 2, 4 * tt, s_len), bf16),
    ]
    if natural_in:
        scratch.append(pltpu.VMEM((N_SLABS, m, LANES), f32))
    return pl.pallas_call(
        kern, grid=(nts,), in_specs=in_specs, out_specs=out_specs, out_shape=out_shape,
        scratch_shapes=scratch, name=f"mixer_prompt_{layer}",
        compiler_params=pltpu.CompilerParams(dimension_semantics=("arbitrary",), vmem_limit_bytes=VMEM_LIMIT),
    )(x, wp["win"], tab, wp["sinks"], wp["bbh"], wp["cch"], wp["ab"], wp["wglu"], wp["wlru"], wp["vec"])


def _prompt_out_call(x2d, z2d, gm2d, wp, layer, *, nb, mb, alpha, natural_out):
    rows = x2d.shape[0]
    kern = functools.partial(_prompt_out_kernel, layer=layer, alpha=alpha, nb=nb, natural_out=natural_out)
    if natural_out:
        out_spec = pl.BlockSpec((nb, mb // nb, D_MODEL), lambda i: (0, i, 0))
        out_shape = jax.ShapeDtypeStruct((nb, rows // nb, D_MODEL), f32)
        scratch = [pltpu.VMEM((N_SLABS, mb, LANES), f32)]
    else:
        out_spec = pl.BlockSpec((mb, D_MODEL), lambda i: (i, 0))
        out_shape = jax.ShapeDtypeStruct((rows, D_MODEL), f32)
        scratch = []
    return pl.pallas_call(
        kern, grid=(rows // mb,),
        in_specs=[
            pl.BlockSpec((mb, D_MODEL), lambda i: (i, 0)),
            pl.BlockSpec((mb, 3 * BRANCH), lambda i: (i, 0)),
            pl.BlockSpec((mb, 3 * D_MODEL), lambda i: (i, 0)),
            _layer_spec((BRANCH, D_MODEL), layer), _layer_spec((BRANCH, D_MODEL), layer),
            _layer_spec((BRANCH, D_MODEL), layer),
            _layer_spec((D_MODEL, D_MODEL), layer),
            _const_spec(wp["ln_g"].shape), _const_spec(wp["ln_b"].shape),
        ],
        out_specs=out_spec, out_shape=out_shape, scratch_shapes=scratch, name=f"out_prompt_{layer}",
        compiler_params=pltpu.CompilerParams(dimension_semantics=("arbitrary",), vmem_limit_bytes=VMEM_LIMIT),
    )(x2d, z2d, gm2d, wp["wbr_a"], wp["wbr_b"], wp["wbr_c"], wp["wout"], wp["ln_g"], wp["ln_b"])


def _sample_call(x2d, wp, tabs, state, *, depth, nbg, nb, tt, alpha):
    m = nb * tt
    kern = functools.partial(_sample_kernel, alpha=alpha, nb=nb, tt=tt)
    row_spec = pl.BlockSpec((m, D_MODEL), lambda l, g: (g, 0))
    st = lambda *shape: pl.BlockSpec((nb,) + shape, lambda l, g: (l * nbg + g,) + (0,) * len(shape))
    state_specs = [st(WINDOW, LANES), st(WINDOW, LANES), st(SSM_WIDTH), st(SSM_WIDTH), st(BRANCH), st(3 * BRANCH)]
    state_shapes = [(WINDOW, LANES), (WINDOW, LANES), (SSM_WIDTH,), (SSM_WIDTH,), (BRANCH,), (3 * BRANCH,)]
    in_specs = [row_spec, _layer_spec((D_MODEL, IN_COLS)), _const_spec(tabs[0].shape), _const_spec(tabs[1].shape),
                _layer_spec(wp["sinkcol"].shape[1:]), *_mixer_weight_specs(None),
                _layer_spec((BRANCH, D_MODEL)), _layer_spec((BRANCH, D_MODEL)), _layer_spec((BRANCH, D_MODEL)),
                _layer_spec((D_MODEL, D_MODEL)), _const_spec(wp["ln_g"].shape), _const_spec(wp["ln_b"].shape),
                *state_specs]
    out_specs = [pl.BlockSpec((m, D_MODEL), lambda l, g: (l * nbg + g, 0)), *state_specs]
    out_shape = [jax.ShapeDtypeStruct((depth * nbg * m, D_MODEL), f32)] + [
        jax.ShapeDtypeStruct((depth * nbg * nb,) + s, f32) for s in state_shapes]
    scratch = [pltpu.VMEM((nbg, m, D_MODEL), f32)] + _mixer_scratch(nb, tt, m)
    return pl.pallas_call(
        kern, grid=(depth, nbg), in_specs=in_specs, out_specs=out_specs, out_shape=out_shape,
        scratch_shapes=scratch, name="sample_layers",
        compiler_params=pltpu.CompilerParams(
            dimension_semantics=("arbitrary", "arbitrary"), vmem_limit_bytes=VMEM_LIMIT),
    )(x2d, wp["win"], tabs[0], tabs[1], wp["sinkcol"], wp["bbh"], wp["cch"], wp["ab"], wp["wglu"], wp["wlru"],
      wp["vec"], wp["wbr_a"], wp["wbr_b"], wp["wbr_c"], wp["wout"], wp["ln_g"], wp["ln_b"], *state)


def _t5_bucket(dist):
    max_exact = N_BUCKETS // 2
    d = jnp.maximum(dist, 0)
    large = max_exact + (jnp.log(jnp.maximum(d, 1).astype(f32) / max_exact)
                         / math.log(MAX_DISTANCE / max_exact) * (N_BUCKETS - max_exact)).astype(jnp.int32)
    large = jnp.minimum(large, N_BUCKETS - 1)
    return jnp.where(d < max_exact, d, large)


def _band_bias(rel_bias, tt):
    s_len = WINDOW + tt
    by_dist = rel_bias.astype(f32)[_t5_bucket(jnp.arange(WINDOW))].T
    row0 = jnp.concatenate([jnp.full((N_Q_HEADS, 1), NEG, f32), by_dist[:, ::-1],
                            jnp.full((N_Q_HEADS, tt), NEG, f32)], axis=1)
    flat = jnp.tile(row0, (1, tt))[:, :tt * s_len]
    return flat.reshape(N_Q_HEADS, tt, s_len)


def _prompt_table(rel_bias, tt):
    return _band_bias(rel_bias, tt).reshape(2, 4 * tt, WINDOW + tt)


def _sample_tables(rel_bias, tt):
    bs = _band_bias(rel_bias, tt).reshape(2, 4, tt, WINDOW + tt)
    same = jnp.eye(SUBLANES, dtype=bool)
    t1 = jnp.where(same[None, None, None, :, :, None], bs[:, :, :, None, None, :WINDOW], NEG)
    t2 = jnp.where(same[None, None, None, :, None, :], bs[:, :, :, None, WINDOW:, None], NEG)
    rows = 4 * tt * SUBLANES
    return t1.reshape(2, rows, SUBLANES * WINDOW), t2.reshape(2, rows, tt * SUBLANES)


def _weights(s_tt, w_in, sinks, w_branch_a, lam_re, lam_im, log_step, b_re, b_im, c_re, c_im, ssm_d, w_glu,
             w_branch_b, conv_w, conv_b, w_a, b_a, w_x, b_x, lam, w_branch_c, w_out, ln_g, ln_b):
    depth = w_in.shape[0]
    hg = SSM_GROUPS // 2
    lr = jnp.minimum(lam_re.astype(f32), -1e-4)
    li = lam_im.astype(f32)
    step = jnp.exp(log_step.astype(f32))[..., None]
    mag = jnp.exp(lr * step)
    ab_re = mag * jnp.cos(li * step)
    ab_im = mag * jnp.sin(li * step)
    den = lr * lr + li * li
    f_re = ((ab_re - 1.0) * lr + ab_im * li) / den
    f_im = (ab_im * lr - (ab_re - 1.0) * li) / den
    bre, bim = b_re.astype(f32), b_im.astype(f32)
    bb_re = f_re[..., None] * bre - f_im[..., None] * bim
    bb_im = f_re[..., None] * bim + f_im[..., None] * bre
    eye = jnp.eye(hg, dtype=f32)
    drv = jnp.stack([bb_re, bb_im], axis=1).reshape(depth, 2, 2, hg, SSM_STATE, SSM_GROUP)
    drv = drv.transpose(0, 2, 3, 5, 1, 4)[:, :, :, :, :, None, :] * eye[None, None, :, None, None, :, None]
    bbh = drv.reshape(depth, 2, hg * SSM_GROUP, SSM_WIDTH).astype(bf16)
    rdo = jnp.stack([c_re.astype(f32), -c_im.astype(f32)], axis=1).reshape(depth, 2, 2, hg, SSM_GROUP, SSM_STATE)
    rdo = rdo.transpose(0, 2, 1, 3, 5, 4)[:, :, :, :, :, None, :] * eye[None, None, None, :, None, :, None]
    cch = rdo.reshape(depth, 2, SSM_WIDTH, hg * SSM_GROUP).astype(bf16)
    ab = jnp.stack([ab_re.reshape(depth, 2, HALF), ab_im.reshape(depth, 2, HALF)], axis=2).reshape(depth, 4, HALF)
    eye4 = jnp.eye(LRU_BLOCKS // 2, dtype=f32)
    blk = jnp.stack([w_a.astype(f32), w_x.astype(f32)], axis=1).reshape(depth, 2, 2, LRU_BLOCKS // 2, 64, 64)
    wlru = (blk[:, :, :, :, :, None, :] * eye4[None, None, None, :, None, :, None]).reshape(depth, 2, 2, 256, 256)
    row = lambda v: v.astype(f32)[:, None, :]
    vec = jnp.concatenate([conv_w.astype(f32), row(conv_b), row(b_a), row(b_x), row(lam), row(ssm_d),
                           jnp.zeros((depth, 16 - V_D - 1, BRANCH), f32)], axis=1)
    sk = sinks.astype(f32).reshape(depth, 2, 4, 1)
    return {
        "win": w_in.astype(bf16),
        "sinks": sinks.astype(f32),
        "sinkcol": jnp.broadcast_to(sk, (depth, 2, 4, s_tt * SUBLANES)).reshape(depth, 2, 4 * s_tt * SUBLANES, 1),
        "bbh": bbh, "cch": cch, "ab": ab,
        "wglu": w_glu.astype(bf16),
        "wlru": wlru.astype(bf16),
        "vec": vec,
        "wbr_a": w_branch_a.astype(bf16), "wbr_b": w_branch_b.astype(bf16), "wbr_c": w_branch_c.astype(bf16),
        "wout": w_out.astype(bf16),
        "ln_g": ln_g.astype(f32), "ln_b": ln_b.astype(f32),
    }


def _group_rows(x, nbg, nb):
    b, t, c = x.shape
    return x.reshape(nbg, nb, t, c).transpose(0, 2, 1, 3).reshape(nbg * t * nb, c)


def _ungroup_rows(x2d, nbg, nb, t):
    c = x2d.shape[-1]
    return x2d.reshape(nbg, t, nb, c).transpose(0, 2, 1, 3).reshape(nbg * nb, t, c)


def _states_out(lead, ko, vo, sre, sim, lru, conv):
    return (ko.reshape(*lead, WINDOW, 2, HEAD_DIM), vo.reshape(*lead, WINDOW, 2, HEAD_DIM),
            sre.reshape(*lead, SSM_GROUPS, SSM_STATE), sim.reshape(*lead, SSM_GROUPS, SSM_STATE),
            lru.reshape(*lead, BRANCH), conv.reshape(*lead, CONV_WIDTH - 1, BRANCH))


P_TT = 64
OUT_ROWS = 512
S_NBG = 4


def kernel(x_prompt, x_sample, cache_k, cache_v, state_ssm_re, state_ssm_im, state_lru, state_conv, rel_bias, w_in, sinks, w_branch_a, ssm_lambda_re, ssm_lambda_im, ssm_log_step, ssm_b_re, ssm_b_im, ssm_c_re, ssm_c_im, ssm_d, ssm_w_glu, w_branch_b, conv_w, conv_b, lru_w_a, lru_b_a, lru_w_x, lru_b_x, lru_lambda, w_branch_c, w_out, ln_g, ln_b):
    depth = w_in.shape[0]
    alpha = (2 * depth) ** 0.25
    pb, pt, _ = x_prompt.shape
    sb, st, _ = x_sample.shape
    s_nb = sb // S_NBG

    wp = _weights(st, w_in, sinks, w_branch_a, ssm_lambda_re, ssm_lambda_im, ssm_log_step, ssm_b_re, ssm_b_im,
                  ssm_c_re, ssm_c_im, ssm_d, ssm_w_glu, w_branch_b, conv_w, conv_b, lru_w_a, lru_b_a, lru_w_x,
                  lru_b_x, lru_lambda, w_branch_c, w_out, ln_g, ln_b)
    p_tab = _prompt_table(rel_bias, P_TT)
    s_tabs = _sample_tables(rel_bias, st)

    st_p = []
    x = x_prompt
    for l in range(depth):
        outs = _prompt_mixer_call(x, wp, p_tab, l, nb=pb, tt=P_TT, natural_in=(l == 0))
        if l == 0:
            z, gm, x, *sp = outs
        else:
            z, gm, *sp = outs
        x = _prompt_out_call(x, z, gm, wp, l, nb=pb, mb=OUT_ROWS, alpha=alpha, natural_out=(l == depth - 1))
        st_p.append(_states_out((pb,), *sp))
    y_prompt = x
    stacked_p = [jnp.stack([s[k] for s in st_p]) for k in range(6)]

    n = depth * sb
    s_state = (cache_k.astype(f32).reshape(n, WINDOW, LANES), cache_v.astype(f32).reshape(n, WINDOW, LANES),
               state_ssm_re.astype(f32).reshape(n, SSM_WIDTH), state_ssm_im.astype(f32).reshape(n, SSM_WIDTH),
               state_lru.astype(f32).reshape(n, BRANCH), state_conv.astype(f32).reshape(n, 3 * BRANCH))
    ys, *ss = _sample_call(_group_rows(x_sample, S_NBG, s_nb), wp, s_tabs, s_state, depth=depth, nbg=S_NBG,
                           nb=s_nb, tt=st, alpha=alpha)
    y_sample = _ungroup_rows(ys[(depth - 1) * sb * st:], S_NBG, s_nb, st)
    stacked_s = _states_out((depth, sb), *ss)
    return (y_prompt, y_sample, *stacked_p, *stacked_s)
```

```python
import functools
import math

import jax
import jax.numpy as jnp
from jax import lax
from jax.experimental import pallas as pl
from jax.experimental.pallas import tpu as pltpu

f32 = jnp.float32
bf16 = jnp.bfloat16

D_MODEL = 1024
BRANCH = 512
HEAD_DIM = 64
N_Q_HEADS = 8
WINDOW = 128
N_BUCKETS = 32
MAX_DISTANCE = 128
SSM_GROUPS = 32
SSM_GROUP = 16
SSM_STATE = 64
LRU_BLOCKS = 8
CONV_WIDTH = 4
LRU_C = 8.0
LN_EPS = 1e-5
NEG = -1e30

LANES = 128
SUBLANES = 8
VMEM_LIMIT = 60 * 1024 * 1024

C_Q, C_K, C_V, C_GA, C_UB, C_GB, C_XC, C_GC, C_GM = (
    (0, 512), (512, 640), (640, 768), (768, 1280), (1280, 1792), (1792, 2304),
    (2304, 2816), (2816, 3328), (3328, 6400))
IN_COLS = C_GM[1]
HALF = SSM_GROUPS // 2 * SSM_STATE
SSM_WIDTH = 2 * HALF
N_SLABS = D_MODEL // LANES

V_CONVW, V_CONVB, V_BA, V_BX, V_LAM, V_D = 0, 4, 5, 6, 7, 8


def _split_heads(qa, lo, kvh):
    q_lo = jnp.where(lo, qa, 0.0)
    q_hi = jnp.where(lo, 0.0, qa)
    if kvh == 0:
        return [q_lo, pltpu.roll(q_hi, HEAD_DIM, axis=1)]
    return [pltpu.roll(q_lo, HEAD_DIM, axis=1), q_hi]


def _merge_heads(o0, o1, lo, kvh):
    if kvh == 0:
        return jnp.where(lo, o0, pltpu.roll(o1, HEAD_DIM, axis=1))
    return jnp.where(lo, pltpu.roll(o0, HEAD_DIM, axis=1), o1)


def _softmax_pv(q4, keys, vals, tabs, sink):
    scs = []
    for k, t in zip(keys, tabs):
        sc = lax.dot_general(q4, k, (((1,), (1,)), ((), ())), preferred_element_type=f32)
        scs.append(jnp.where(t > 0.5 * NEG, sc * (HEAD_DIM ** -0.5) + t, NEG))
    mx = sink
    for sc in scs:
        mx = jnp.maximum(mx, jnp.max(sc, axis=-1, keepdims=True))
    den = jnp.exp(sink - mx)
    o = None
    for sc, v in zip(scs, vals):
        p = jnp.exp(sc - mx)
        den = den + jnp.sum(p, axis=-1, keepdims=True)
        pv = jnp.dot(p.astype(bf16), v, preferred_element_type=f32)
        o = pv if o is None else o + pv
    return o / den


class _Filler:
    def __init__(self, thunks):
        self._thunks = list(thunks)
        self._done = 0

    def tick(self, i, n):
        due = -((i + 1) * len(self._thunks) // -n)
        while self._done < due:
            self._thunks[self._done]()
            self._done += 1

    def drain(self):
        self.tick(0, 1)


def _scores_prompt(qs, kvs, sc_scr, *, nb, tt):
    s_len = WINDOW + tt
    lo = lax.broadcasted_iota(jnp.int32, (tt, LANES), 1) < HEAD_DIM
    for b in range(nb):
        kk16 = kvs.at[0][pl.ds(b, s_len, stride=nb), :].astype(bf16)
        for kvh in range(2):
            parts = []
            for i in range(2):
                parts += _split_heads(qs.at[2 * kvh + i][pl.ds(b, tt, stride=nb), :], lo, kvh)
            q4 = jnp.concatenate(parts, axis=0).astype(bf16)
            sc_scr[b, kvh] = lax.dot_general(q4, kk16, (((1,), (1,)), ((), ())), preferred_element_type=f32)


def _softmax_prompt(layer, tabm, sinks_ref, sc_scr, p_scr, filler, *, nb, tt):
    n, it = nb * 2 * 4, 0
    for b in range(nb):
        for kvh in range(2):
            for g in range(4):
                rs = slice(g * tt, (g + 1) * tt)
                t = tabm[kvh, rs, :]
                sc = jnp.where(t > 0.5 * NEG, sc_scr[b, kvh, rs, :] + t, NEG)
                sink = sinks_ref[layer, 4 * kvh + g]
                mx = jnp.maximum(jnp.max(sc, axis=-1, keepdims=True), sink)
                p = jnp.exp(sc - mx)
                den = jnp.sum(p, axis=-1, keepdims=True) + jnp.exp(sink - mx)
                p_scr[b, kvh, rs, :] = (p / den).astype(bf16)
                filler.tick(it, n)
                it += 1


def _pv_thunks(kvs, atts, p_scr, *, nb, tt):
    s_len = WINDOW + tt
    lo = lax.broadcasted_iota(jnp.int32, (tt, LANES), 1) < HEAD_DIM

    def piece(b):
        vv16 = kvs.at[1][pl.ds(b, s_len, stride=nb), :].astype(bf16)
        for kvh in range(2):
            o = jnp.dot(p_scr[b, kvh], vv16, preferred_element_type=f32)
            for i in range(2):
                slab = _merge_heads(o[2 * i * tt:(2 * i + 1) * tt], o[(2 * i + 1) * tt:(2 * i + 2) * tt], lo, kvh)
                atts.at[2 * kvh + i][pl.ds(b, tt, stride=nb), :] = slab

    return [functools.partial(piece, b) for b in range(nb)]


def _attend_sample(qs, kvn, atts, kc_ref, vc_ref, tab1_ref, tab2_ref, sink_ref, ko_ref, vo_ref, *, nb, tt):
    lo = lax.broadcasted_iota(jnp.int32, (SUBLANES * tt, LANES), 1) < HEAD_DIM
    for sg in range(nb // SUBLANES):
        b0 = SUBLANES * sg

        def rows(ref, b0=b0):
            return jnp.concatenate([ref[t * nb + b0:t * nb + b0 + SUBLANES, :] for t in range(tt)], axis=0)

        kc = kc_ref[b0:b0 + SUBLANES].reshape(SUBLANES * WINDOW, LANES).astype(bf16)
        vc = vc_ref[b0:b0 + SUBLANES].reshape(SUBLANES * WINDOW, LANES).astype(bf16)
        kn = rows(kvn.at[0]).astype(bf16)
        vn = rows(kvn.at[1]).astype(bf16)
        for kvh in range(2):
            parts = []
            for i in range(2):
                parts += _split_heads(rows(qs.at[2 * kvh + i]), lo, kvh)
            q4 = jnp.concatenate(parts, axis=0).astype(bf16)
            o = _softmax_pv(q4, [kc, kn], [vc, vn], [tab1_ref[kvh], tab2_ref[kvh]], sink_ref[kvh])
            r = SUBLANES * tt
            for i in range(2):
                slab = _merge_heads(o[2 * i * r:(2 * i + 1) * r], o[(2 * i + 1) * r:(2 * i + 2) * r], lo, kvh)
                for t in range(tt):
                    atts[2 * kvh + i, t * nb + b0:t * nb + b0 + SUBLANES, :] = slab[SUBLANES * t:SUBLANES * (t + 1)]
    for b in range(nb):
        ko_ref[b, 0:WINDOW - tt, :] = kc_ref[b, tt:WINDOW, :]
        vo_ref[b, 0:WINDOW - tt, :] = vc_ref[b, tt:WINDOW, :]
        ko_ref[b, WINDOW - tt:WINDOW, :] = kvn.at[0][pl.ds(b, tt, stride=nb), :]
        vo_ref[b, WINDOW - tt:WINDOW, :] = kvn.at[1][pl.ds(b, tt, stride=nb), :]


def _s5_block(ub, xs, hst, bbh_ref, cch_ref, ab_ref, *, nb, tt):
    ub16 = ub.astype(bf16)
    ys = []
    for hf in range(2):
        _s5_drive(ub16, xs, bbh_ref, hf)
        _s5_scan(xs, hst, ab_ref, hf, nb=nb, tt=tt)
        ys.append(_s5_readout(xs, cch_ref, hf))
    return jnp.concatenate(ys, axis=1)


def _s5_drive(ub16, xs, bbh_ref, hf):
    xs[...] = jnp.dot(ub16[:, 256 * hf:256 * (hf + 1)], bbh_ref[hf], preferred_element_type=f32)


def _s5_readout(xs, cch_ref, hf):
    return jnp.dot(xs[...].astype(bf16), cch_ref[hf], preferred_element_type=f32)


def _s5_scan(xs, hst, ab_ref, hf, *, nb, tt, filler=None):
    cw = HALF // 2
    n, it = 2 * (nb // SUBLANES) * tt, 0
    for ck in range(2):
        re0, im0, st0 = cw * ck, HALF + cw * ck, HALF * hf + cw * ck
        ar = jnp.broadcast_to(ab_ref[2 * hf:2 * hf + 1, re0:re0 + cw], (SUBLANES, cw))
        ai = jnp.broadcast_to(ab_ref[2 * hf + 1:2 * hf + 2, re0:re0 + cw], (SUBLANES, cw))
        for bg in range(nb // SUBLANES):
            b0 = SUBLANES * bg
            hr = hst[0, b0:b0 + SUBLANES, st0:st0 + cw]
            hi = hst[1, b0:b0 + SUBLANES, st0:st0 + cw]
            for t in range(tt):
                r0 = t * nb + b0
                xr = xs[r0:r0 + SUBLANES, re0:re0 + cw]
                xi = xs[r0:r0 + SUBLANES, im0:im0 + cw]
                hr, hi = ar * hr - ai * hi + xr, ar * hi + ai * hr + xi
                xs[r0:r0 + SUBLANES, re0:re0 + cw] = hr
                xs[r0:r0 + SUBLANES, im0:im0 + cw] = hi
                if filler is not None:
                    filler.tick(it, n)
                it += 1
            hst[0, b0:b0 + SUBLANES, st0:st0 + cw] = hr
            hst[1, b0:b0 + SUBLANES, st0:st0 + cw] = hi


def _lru_block(xcs, la, lb, lst, wlru_ref, vec_ref, *, nb, tt):
    _lru_prep(xcs, la, lb, wlru_ref, vec_ref, nb=nb, tt=tt)
    _lru_scan(la, lb, lst, nb=nb, tt=tt)


def _lru_prep(xcs, la, lb, wlru_ref, vec_ref, *, nb, tt):
    m = nb * tt
    conv = vec_ref[V_CONVB:V_CONVB + 1, :] + vec_ref[V_CONVW:V_CONVW + 1, :] * xcs[0:m, :]
    for j in range(1, CONV_WIDTH):
        conv = conv + vec_ref[V_CONVW + j:V_CONVW + j + 1, :] * xcs[j * nb:j * nb + m, :]
    c16 = conv.astype(bf16)

    def block_dot(g):
        return jnp.concatenate(
            [jnp.dot(c16[:, 256 * hf:256 * (hf + 1)], wlru_ref[g, hf], preferred_element_type=f32)
             for hf in range(2)], axis=1)

    r = jax.nn.sigmoid(block_dot(0) + vec_ref[V_BA:V_BA + 1, :])
    i = jax.nn.sigmoid(block_dot(1) + vec_ref[V_BX:V_BX + 1, :])
    log_a = LRU_C * r * jax.nn.log_sigmoid(vec_ref[V_LAM:V_LAM + 1, :])
    a = jnp.exp(log_a)
    la[...] = a
    g2 = -jnp.tanh(log_a) * (a * a + 1.0)
    lb[...] = conv * i * jnp.where(g2 > 0.0, g2 * lax.rsqrt(g2), 0.0)


def _lru_scan(la, lb, lst, *, nb, tt, filler=None):
    n, it = (nb // SUBLANES) * tt, 0
    for bg in range(nb // SUBLANES):
        b0 = SUBLANES * bg
        h = lst[b0:b0 + SUBLANES, :]
        for t in range(tt):
            r0 = t * nb + b0
            h = la[r0:r0 + SUBLANES, :] * h + lb[r0:r0 + SUBLANES, :]
            lb[r0:r0 + SUBLANES, :] = h
            if filler is not None:
                filler.tick(it, n)
            it += 1
        lst[b0:b0 + SUBLANES, :] = h


def _mixer_tail(xb, win_ref, atts, xs, hst, xcs, la, lb, lst, bbh_ref, cch_ref, ab_ref, wglu_ref, wlru_ref,
                vec_ref, *, nb, tt):
    m = nb * tt

    def proj(cols):
        return jnp.dot(xb, win_ref[:, cols[0]:cols[1]], preferred_element_type=f32)

    att = jnp.concatenate([atts[j] for j in range(4)], axis=1)
    za = (att * jax.nn.silu(proj(C_GA))).astype(bf16)

    ub = proj(C_UB)
    y = _s5_block(ub, xs, hst, bbh_ref, cch_ref, ab_ref, nb=nb, tt=tt) + vec_ref[V_D:V_D + 1, :] * ub
    zz = jax.nn.gelu(y)
    zz = zz * jax.nn.sigmoid(jnp.dot(zz.astype(bf16), wglu_ref[...], preferred_element_type=f32))
    zb = (zz * jax.nn.silu(proj(C_GB))).astype(bf16)

    xcs[3 * nb:3 * nb + m, :] = proj(C_XC)
    _lru_block(xcs, la, lb, lst, wlru_ref, vec_ref, nb=nb, tt=tt)
    zc = (lb[...] * jax.nn.silu(proj(C_GC))).astype(bf16)
    return za, zb, zc


def _merge_gate(xb, win_ref, k):
    c0 = C_GM[0] + D_MODEL * k
    return jax.nn.sigmoid(jnp.dot(xb, win_ref[:, c0:c0 + D_MODEL], preferred_element_type=f32))


def _out_core(x, zs, gates, wbr_refs, wout_ref, ln_g, ln_b, alpha):
    merged = None
    for k in range(3):
        g = gates(k)
        yk = jnp.dot(zs[k], wbr_refs[k][...], preferred_element_type=f32)
        merged = g * yk if merged is None else merged + g * yk
    out = jnp.dot(merged.astype(bf16), wout_ref[...], preferred_element_type=f32)
    h = alpha * x + out
    mu = jnp.mean(h, axis=-1, keepdims=True)
    d = h - mu
    var = jnp.mean(d * d, axis=-1, keepdims=True)
    return d * lax.rsqrt(var + LN_EPS) * ln_g + ln_b


def _prompt_mixer_kernel(*refs, layer, nb, tt, nts, natural_in):
    (x_ref, win_ref, tab_ref, sinks_ref, bbh_ref, cch_ref, ab_ref, wglu_ref, wlru_ref, vec_ref), refs = refs[:10], refs[10:]
    (z_ref, gm_ref), refs = refs[:2], refs[2:]
    if natural_in:
        xtb_ref, refs = refs[0], refs[1:]
    (ko_ref, vo_ref, sre_ref, sim_ref, lruo_ref, convo_ref,
     qs, kvs, atts, xs, hst, xcs, la, lb, lst, tabm, sc_scr, p_scr, gact), refs = refs[:19], refs[19:]
    ts = pl.program_id(0)
    m = nb * tt
    w = WINDOW * nb
    last = ts == nts - 1

    @pl.when(ts == 0)
    def _zero_state():
        kvs[:, 0:w, :] = jnp.zeros((2, w, LANES), f32)
        hst[...] = jnp.zeros(hst.shape, f32)
        lst[...] = jnp.zeros(lst.shape, f32)
        xcs[0:3 * nb, :] = jnp.zeros((3 * nb, BRANCH), f32)

    if natural_in:
        (xsl,) = refs
        for b in range(nb):
            for k in range(N_SLABS):
                xsl.at[k][pl.ds(b, tt, stride=nb), :] = x_ref[b, :, LANES * k:LANES * (k + 1)]
        x = jnp.concatenate([xsl[k] for k in range(N_SLABS)], axis=1)
        xtb_ref[...] = x
    else:
        x = x_ref[...]
    xb = x.astype(bf16)

    def proj(cols):
        return jnp.dot(xb, win_ref[:, cols[0]:cols[1]], preferred_element_type=f32)

    q = proj(C_Q)
    for j in range(4):
        qs[j] = q[:, LANES * j:LANES * (j + 1)] * (HEAD_DIM ** -0.5)
    kv = proj((C_K[0], C_V[1]))
    kvs[0, w:w + m, :] = kv[:, :LANES]
    kvs[1, w:w + m, :] = kv[:, LANES:]
    kj = lax.broadcasted_iota(jnp.int32, (2, 4 * tt, WINDOW + tt), 2)
    tabm[...] = jnp.where(kj + ts * tt >= WINDOW, tab_ref[...], NEG)
    _scores_prompt(qs, kvs, sc_scr, nb=nb, tt=tt)
    ub = proj(C_UB)
    ub16 = ub.astype(bf16)
    _s5_drive(ub16, xs, bbh_ref, 0)

    def gate_piece(j):
        c0 = C_GM[0] + BRANCH * j
        g = jax.nn.sigmoid(jnp.dot(xb, win_ref[:, c0:c0 + BRANCH], preferred_element_type=f32))
        gm_ref[:, BRANCH * j:BRANCH * (j + 1)] = g.astype(bf16)

    fill = _Filler([functools.partial(gate_piece, j) for j in range(3 * D_MODEL // BRANCH)])
    _softmax_prompt(layer, tabm, sinks_ref, sc_scr, p_scr, fill, nb=nb, tt=tt)
    fill.drain()

    def branch_a():
        att = jnp.concatenate([atts[j] for j in range(4)], axis=1)
        z_ref[:, 0:BRANCH] = (att * jax.nn.silu(proj(C_GA))).astype(bf16)

    def conv_in():
        xcs[3 * nb:3 * nb + m, :] = proj(C_XC)

    fill = _Filler(_pv_thunks(kvs, atts, p_scr, nb=nb, tt=tt) + [branch_a, conv_in])
    _s5_scan(xs, hst, ab_ref, 0, nb=nb, tt=tt, filler=fill)
    fill.drain()
    kvs[:, 0:w, :] = kvs[:, m:m + w, :]

    y0 = _s5_readout(xs, cch_ref, 0)
    _lru_prep(xcs, la, lb, wlru_ref, vec_ref, nb=nb, tt=tt)
    _s5_drive(ub16, xs, bbh_ref, 1)

    def gate_act(k, cols):
        gact[k] = jax.nn.silu(proj(cols))

    fill = _Filler([functools.partial(gate_act, 0, C_GB), functools.partial(gate_act, 1, C_GC)])
    _s5_scan(xs, hst, ab_ref, 1, nb=nb, tt=tt, filler=fill)
    fill.drain()

    def branch_b():
        y = jnp.concatenate([y0, _s5_readout(xs, cch_ref, 1)], axis=1) + vec_ref[V_D:V_D + 1, :] * ub
        zz = jax.nn.gelu(y)
        zz = zz * jax.nn.sigmoid(jnp.dot(zz.astype(bf16), wglu_ref[...], preferred_element_type=f32))
        z_ref[:, BRANCH:2 * BRANCH] = (zz * gact[0]).astype(bf16)

    fill = _Filler([branch_b])
    _lru_scan(la, lb, lst, nb=nb, tt=tt, filler=fill)
    fill.drain()
    z_ref[:, 2 * BRANCH:3 * BRANCH] = (lb[...] * gact[1]).astype(bf16)

    @pl.when(last)
    def _store_state():
        sre_ref[...] = hst[0]
        sim_ref[...] = hst[1]
        lruo_ref[...] = lst[...]
        for t in range(3):
            convo_ref[:, BRANCH * t:BRANCH * (t + 1)] = xcs[m + t * nb:m + (t + 1) * nb, :]
        for b in range(nb):
            ko_ref[b] = kvs.at[0][pl.ds(b, WINDOW, stride=nb), :]
            vo_ref[b] = kvs.at[1][pl.ds(b, WINDOW, stride=nb), :]

    xcs[0:3 * nb, :] = xcs[m:m + 3 * nb, :]


def _prompt_out_kernel(*refs, layer, alpha, nb, natural_out):
    if natural_out:
        x_ref, z_ref, gm_ref, wa_ref, wb_ref, wc_ref, wout_ref, lng_ref, lnb_ref, y_ref, ysl = refs
    else:
        x_ref, z_ref, gm_ref, wa_ref, wb_ref, wc_ref, wout_ref, lng_ref, lnb_ref, y_ref = refs
    zs = [z_ref[:, BRANCH * k:BRANCH * (k + 1)] for k in range(3)]
    gates = lambda k: gm_ref[:, D_MODEL * k:D_MODEL * (k + 1)].astype(f32)
    y = _out_core(x_ref[...], zs, gates, (wa_ref, wb_ref, wc_ref), wout_ref,
                  lng_ref[layer:layer + 1, :], lnb_ref[layer:layer + 1, :], alpha)
    if natural_out:
        tt = y.shape[0] // nb
        for k in range(N_SLABS):
            ysl[k] = y[:, LANES * k:LANES * (k + 1)]
        for b in range(nb):
            for k in range(N_SLABS):
                y_ref[b, :, LANES * k:LANES * (k + 1)] = ysl.at[k][pl.ds(b, tt, stride=nb), :]
    else:
        y_ref[...] = y


def _sample_kernel(x_ref, win_ref, tab1_ref, tab2_ref, sink_ref, bbh_ref, cch_ref, ab_ref, wglu_ref, wlru_ref,
                   vec_ref, wa_ref, wb_ref, wc_ref, wout_ref, lng_ref, lnb_ref,
                   kc_ref, vc_ref, sre0_ref, sim0_ref, lru0_ref, conv0_ref,
                   y_ref, ko_ref, vo_ref, sre_ref, sim_ref, lruo_ref, convo_ref,
                   xcar, qs, kvn, atts, xs, hst, xcs, la, lb, lst, *, alpha, nb, tt):
    layer = pl.program_id(0)
    g = pl.program_id(1)
    m = nb * tt

    @pl.when(layer == 0)
    def _first_layer_input():
        xcar[g] = x_ref[...]

    x = xcar[g]
    xb = x.astype(bf16)
    hst[0] = sre0_ref[...]
    hst[1] = sim0_ref[...]
    lst[...] = lru0_ref[...]
    for t in range(3):
        xcs[t * nb:(t + 1) * nb, :] = conv0_ref[:, BRANCH * t:BRANCH * (t + 1)]

    def proj(cols):
        return jnp.dot(xb, win_ref[:, cols[0]:cols[1]], preferred_element_type=f32)

    q = proj(C_Q)
    for j in range(4):
        qs[j] = q[:, LANES * j:LANES * (j + 1)]
    kv = proj((C_K[0], C_V[1]))
    kvn[0] = kv[:, :LANES]
    kvn[1] = kv[:, LANES:]
    _attend_sample(qs, kvn, atts, kc_ref, vc_ref, tab1_ref, tab2_ref, sink_ref, ko_ref, vo_ref, nb=nb, tt=tt)
    zs = _mixer_tail(xb, win_ref, atts, xs, hst, xcs, la, lb, lst, bbh_ref, cch_ref, ab_ref, wglu_ref,
                     wlru_ref, vec_ref, nb=nb, tt=tt)
    sre_ref[...] = hst[0]
    sim_ref[...] = hst[1]
    lruo_ref[...] = lst[...]
    for t in range(3):
        convo_ref[:, BRANCH * t:BRANCH * (t + 1)] = xcs[m + t * nb:m + (t + 1) * nb, :]

    y = _out_core(x, zs, functools.partial(_merge_gate, xb, win_ref), (wa_ref, wb_ref, wc_ref), wout_ref,
                  lng_ref[pl.ds(layer, 1), :], lnb_ref[pl.ds(layer, 1), :], alpha)
    xcar[g] = y
    y_ref[...] = y


def _layer_spec(shape, layer=None):
    if layer is None:
        idx = lambda l, *_: (l,) + (0,) * len(shape)
    else:
        idx = lambda *_: (layer,) + (0,) * len(shape)
    return pl.BlockSpec((None,) + tuple(shape), idx, pipeline_mode=pl.Buffered(1))


def _const_spec(shape):
    return pl.BlockSpec(tuple(shape), lambda *_: (0,) * len(shape), pipeline_mode=pl.Buffered(1))


def _mixer_weight_specs(layer):
    return [
        _layer_spec((2, 256, SSM_WIDTH), layer),
        _layer_spec((2, SSM_WIDTH, 256), layer),
        _layer_spec((4, HALF), layer),
        _layer_spec((BRANCH, BRANCH), layer),
        _layer_spec((2, 2, 256, 256), layer),
        _layer_spec((16, BRANCH), layer),
    ]


def _mixer_scratch(nb, tt, kv_rows):
    m = nb * tt
    return [
        pltpu.VMEM((4, m, LANES), f32),
        pltpu.VMEM((2, kv_rows, LANES), f32),
        pltpu.VMEM((4, m, LANES), f32),
        pltpu.VMEM((m, SSM_WIDTH), f32),
        pltpu.VMEM((2, nb, SSM_WIDTH), f32),
        pltpu.VMEM((m + 3 * nb, BRANCH), f32),
        pltpu.VMEM((m, BRANCH), f32),
        pltpu.VMEM((m, BRANCH), f32),
        pltpu.VMEM((nb, BRANCH), f32),
    ]


def _prompt_mixer_call(x, wp, tab, layer, *, nb, tt, natural_in):
    nts = (x.shape[1] if natural_in else x.shape[0] // nb) // tt
    m, w, s_len = nb * tt, WINDOW * nb, WINDOW + tt
    kern = functools.partial(_prompt_mixer_kernel, layer=layer, nb=nb, tt=tt, nts=nts, natural_in=natural_in)
    x_spec = (pl.BlockSpec((nb, tt, D_MODEL), lambda t: (0, t, 0)) if natural_in
              else pl.BlockSpec((m, D_MODEL), lambda t: (t, 0)))
    in_specs = [x_spec,
                _layer_spec((D_MODEL, IN_COLS), layer),
                _const_spec(tab.shape), pl.BlockSpec(memory_space=pltpu.SMEM), *_mixer_weight_specs(layer)]
    whole = lambda shape: pl.BlockSpec(shape, lambda t: (0,) * len(shape))
    out_specs = [pl.BlockSpec((m, 3 * BRANCH), lambda t: (t, 0)), pl.BlockSpec((m, 3 * D_MODEL), lambda t: (t, 0))]
    out_shape = [jax.ShapeDtypeStruct((nts * m, 3 * BRANCH), bf16),
                 jax.ShapeDtypeStruct((nts * m, 3 * D_MODEL), bf16)]
    if natural_in:
        out_specs.append(pl.BlockSpec((m, D_MODEL), lambda t: (t, 0)))
        out_shape.append(jax.ShapeDtypeStruct((nts * m, D_MODEL), f32))
    state_shapes = [(nb, WINDOW, LANES), (nb, WINDOW, LANES), (nb, SSM_WIDTH), (nb, SSM_WIDTH),
                    (nb, BRANCH), (nb, 3 * BRANCH)]
    out_specs += [whole(s) for s in state_shapes]
    out_shape += [jax.ShapeDtypeStruct(s, f32) for s in state_shapes]
    scratch = _mixer_scratch(nb, tt, w + m) + [
        pltpu.VMEM((2, 4 * tt, s_len), f32),
        pltpu.VMEM((nb, 2, 4 * tt, s_len), f32),
        pltpu.VMEM((nb, 2, 4 * tt, s_len), bf16),
        pltpu.VMEM((2, m, BRANCH), f32),
    ]
    if natural_in:
        scratch.append(pltpu.VMEM((N_SLABS, m, LANES), f32))
    return pl.pallas_call(
        kern, grid=(nts,), in_specs=in_specs, out_specs=out_specs, out_shape=out_shape,
        scratch_shapes=scratch, name=f"mixer_prompt_{layer}",
        compiler_params=pltpu.CompilerParams(dimension_semantics=("arbitrary",), vmem_limit_bytes=VMEM_LIMIT),
    )(x, wp["win"], tab, wp["sinks"], wp["bbh"], wp["cch"], wp["ab"], wp["wglu"], wp["wlru"], wp["vec"])


def _prompt_out_call(x2d, z2d, gm2d, wp, layer, *, nb, mb, alpha, natural_out):
    rows = x2d.shape[0]
    kern = functools.partial(_prompt_out_kernel, layer=layer, alpha=alpha, nb=nb, natural_out=natural_out)
    if natural_out:
        out_spec = pl.BlockSpec((nb, mb // nb, D_MODEL), lambda i: (0, i, 0))
        out_shape = jax.ShapeDtypeStruct((nb, rows // nb, D_MODEL), f32)
        scratch = [pltpu.VMEM((N_SLABS, mb, LANES), f32)]
    else:
        out_spec = pl.BlockSpec((mb, D_MODEL), lambda i: (i, 0))
        out_shape = jax.ShapeDtypeStruct((rows, D_MODEL), f32)
        scratch = []
    return pl.pallas_call(
        kern, grid=(rows // mb,),
        in_specs=[
            pl.BlockSpec((mb, D_MODEL), lambda i: (i, 0)),
            pl.BlockSpec((mb, 3 * BRANCH), lambda i: (i, 0)),
            pl.BlockSpec((mb, 3 * D_MODEL), lambda i: (i, 0)),
            _layer_spec((BRANCH, D_MODEL), layer), _layer_spec((BRANCH, D_MODEL), layer),
            _layer_spec((BRANCH, D_MODEL), layer),
            _layer_spec((D_MODEL, D_MODEL), layer),
            _const_spec(wp["ln_g"].shape), _const_spec(wp["ln_b"].shape),
        ],
        out_specs=out_spec, out_shape=out_shape, scratch_shapes=scratch, name=f"out_prompt_{layer}",
        compiler_params=pltpu.CompilerParams(dimension_semantics=("arbitrary",), vmem_limit_bytes=VMEM_LIMIT),
    )(x2d, z2d, gm2d, wp["wbr_a"], wp["wbr_b"], wp["wbr_c"], wp["wout"], wp["ln_g"], wp["ln_b"])


def _sample_call(x2d, wp, tabs, state, *, depth, nbg, nb, tt, alpha):
    m = nb * tt
    kern = functools.partial(_sample_kernel, alpha=alpha, nb=nb, tt=tt)
    row_spec = pl.BlockSpec((m, D_MODEL), lambda l, g: (g, 0))
    st = lambda *shape: pl.BlockSpec((nb,) + shape, lambda l, g: (l * nbg + g,) + (0,) * len(shape))
    state_specs = [st(WINDOW, LANES), st(WINDOW, LANES), st(SSM_WIDTH), st(SSM_WIDTH), st(BRANCH), st(3 * BRANCH)]
    state_shapes = [(WINDOW, LANES), (WINDOW, LANES), (SSM_WIDTH,), (SSM_WIDTH,), (BRANCH,), (3 * BRANCH,)]
    in_specs = [row_spec, _layer_spec((D_MODEL, IN_COLS)), _const_spec(tabs[0].shape), _const_spec(tabs[1].shape),
                _layer_spec(wp["sinkcol"].shape[1:]), *_mixer_weight_specs(None),
                _layer_spec((BRANCH, D_MODEL)), _layer_spec((BRANCH, D_MODEL)), _layer_spec((BRANCH, D_MODEL)),
                _layer_spec((D_MODEL, D_MODEL)), _const_spec(wp["ln_g"].shape), _const_spec(wp["ln_b"].shape),
                *state_specs]
    out_specs = [pl.BlockSpec((m, D_MODEL), lambda l, g: (l * nbg + g, 0)), *state_specs]
    out_shape = [jax.ShapeDtypeStruct((depth * nbg * m, D_MODEL), f32)] + [
        jax.ShapeDtypeStruct((depth * nbg * nb,) + s, f32) for s in state_shapes]
    scratch = [pltpu.VMEM((nbg, m, D_MODEL), f32)] + _mixer_scratch(nb, tt, m)
    return pl.pallas_call(
        kern, grid=(depth, nbg), in_specs=in_specs, out_specs=out_specs, out_shape=out_shape,
        scratch_shapes=scratch, name="sample_layers",
        compiler_params=pltpu.CompilerParams(
            dimension_semantics=("arbitrary", "arbitrary"), vmem_limit_bytes=VMEM_LIMIT),
    )(x2d, wp["win"], tabs[0], tabs[1], wp["sinkcol"], wp["bbh"], wp["cch"], wp["ab"], wp["wglu"], wp["wlru"],
      wp["vec"], wp["wbr_a"], wp["wbr_b"], wp["wbr_c"], wp["wout"], wp["ln_g"], wp["ln_b"], *state)


def _t5_bucket(dist):
    max_exact = N_BUCKETS // 2
    d = jnp.maximum(dist, 0)
    large = max_exact + (jnp.log(jnp.maximum(d, 1).astype(f32) / max_exact)
                         / math.log(MAX_DISTANCE / max_exact) * (N_BUCKETS - max_exact)).astype(jnp.int32)
    large = jnp.minimum(large, N_BUCKETS - 1)
    return jnp.where(d < max_exact, d, large)


def _band_bias(rel_bias, tt):
    s_len = WINDOW + tt
    by_dist = rel_bias.astype(f32)[_t5_bucket(jnp.arange(WINDOW))].T
    row0 = jnp.concatenate([jnp.full((N_Q_HEADS, 1), NEG, f32), by_dist[:, ::-1],
                            jnp.full((N_Q_HEADS, tt), NEG, f32)], axis=1)
    flat = jnp.tile(row0, (1, tt))[:, :tt * s_len]
    return flat.reshape(N_Q_HEADS, tt, s_len)


def _prompt_table(rel_bias, tt):
    return _band_bias(rel_bias, tt).reshape(2, 4 * tt, WINDOW + tt)


def _sample_tables(rel_bias, tt):
    bs = _band_bias(rel_bias, tt).reshape(2, 4, tt, WINDOW + tt)
    same = jnp.eye(SUBLANES, dtype=bool)
    t1 = jnp.where(same[None, None, None, :, :, None], bs[:, :, :, None, None, :WINDOW], NEG)
    t2 = jnp.where(same[None, None, None, :, None, :], bs[:, :, :, None, WINDOW:, None], NEG)
    rows = 4 * tt * SUBLANES
    return t1.reshape(2, rows, SUBLANES * WINDOW), t2.reshape(2, rows, tt * SUBLANES)


def _weights(s_tt, w_in, sinks, w_branch_a, lam_re, lam_im, log_step, b_re, b_im, c_re, c_im, ssm_d, w_glu,
             w_branch_b, conv_w, conv_b, w_a, b_a, w_x, b_x, lam, w_branch_c, w_out, ln_g, ln_b):
    depth = w_in.shape[0]
    hg = SSM_GROUPS // 2
    lr = jnp.minimum(lam_re.astype(f32), -1e-4)
    li = lam_im.astype(f32)
    step = jnp.exp(log_step.astype(f32))[..., None]
    mag = jnp.exp(lr * step)
    ab_re = mag * jnp.cos(li * step)
    ab_im = mag * jnp.sin(li * step)
    den = lr * lr + li * li
    f_re = ((ab_re - 1.0) * lr + ab_im * li) / den
    f_im = (ab_im * lr - (ab_re - 1.0) * li) / den
    bre, bim = b_re.astype(f32), b_im.astype(f32)
    bb_re = f_re[..., None] * bre - f_im[..., None] * bim
    bb_im = f_re[..., None] * bim + f_im[..., None] * bre
    eye = jnp.eye(hg, dtype=f32)
    drv = jnp.stack([bb_re, bb_im], axis=1).reshape(depth, 2, 2, hg, SSM_STATE, SSM_GROUP)
    drv = drv.transpose(0, 2, 3, 5, 1, 4)[:, :, :, :, :, None, :] * eye[None, None, :, None, None, :, None]
    bbh = drv.reshape(depth, 2, hg * SSM_GROUP, SSM_WIDTH).astype(bf16)
    rdo = jnp.stack([c_re.astype(f32), -c_im.astype(f32)], axis=1).reshape(depth, 2, 2, hg, SSM_GROUP, SSM_STATE)
    rdo = rdo.transpose(0, 2, 1, 3, 5, 4)[:, :, :, :, :, None, :] * eye[None, None, None, :, None, :, None]
    cch = rdo.reshape(depth, 2, SSM_WIDTH, hg * SSM_GROUP).astype(bf16)
    ab = jnp.stack([ab_re.reshape(depth, 2, HALF), ab_im.reshape(depth, 2, HALF)], axis=2).reshape(depth, 4, HALF)
    eye4 = jnp.eye(LRU_BLOCKS // 2, dtype=f32)
    blk = jnp.stack([w_a.astype(f32), w_x.astype(f32)], axis=1).reshape(depth, 2, 2, LRU_BLOCKS // 2, 64, 64)
    wlru = (blk[:, :, :, :, :, None, :] * eye4[None, None, None, :, None, :, None]).reshape(depth, 2, 2, 256, 256)
    row = lambda v: v.astype(f32)[:, None, :]
    vec = jnp.concatenate([conv_w.astype(f32), row(conv_b), row(b_a), row(b_x), row(lam), row(ssm_d),
                           jnp.zeros((depth, 16 - V_D - 1, BRANCH), f32)], axis=1)
    sk = sinks.astype(f32).reshape(depth, 2, 4, 1)
    return {
        "win": w_in.astype(bf16),
        "sinks": sinks.astype(f32),
        "sinkcol": jnp.broadcast_to(sk, (depth, 2, 4, s_tt * SUBLANES)).reshape(depth, 2, 4 * s_tt * SUBLANES, 1),
        "bbh": bbh, "cch": cch, "ab": ab,
        "wglu": w_glu.astype(bf16),
        "wlru": wlru.astype(bf16),
        "vec": vec,
        "wbr_a": w_branch_a.astype(bf16), "wbr_b": w_branch_b.astype(bf16), "wbr_c": w_branch_c.astype(bf16),
        "wout": w_out.astype(bf16),
        "ln_g": ln_g.astype(f32), "ln_b": ln_b.astype(f32),
    }


def _group_rows(x, nbg, nb):
    b, t, c = x.shape
    return x.reshape(nbg, nb, t, c).transpose(0, 2, 1, 3).reshape(nbg * t * nb, c)


def _ungroup_rows(x2d, nbg, nb, t):
    c = x2d.shape[-1]
    return x2d.reshape(nbg, t, nb, c).transpose(0, 2, 1, 3).reshape(nbg * nb, t, c)


def _states_out(lead, ko, vo, sre, sim, lru, conv):
    return (ko.reshape(*lead, WINDOW, 2, HEAD_DIM), vo.reshape(*lead, WINDOW, 2, HEAD_DIM),
            sre.reshape(*lead, SSM_GROUPS, SSM_STATE), sim.reshape(*lead, SSM_GROUPS, SSM_STATE),
            lru.reshape(*lead, BRANCH), conv.reshape(*lead, CONV_WIDTH - 1, BRANCH))


P_TT = 64
OUT_ROWS = 512
S_NBG = 4


def kernel(x_prompt, x_sample, cache_k, cache_v, state_ssm_re, state_ssm_im, state_lru, state_conv, rel_bias, w_in, sinks, w_branch_a, ssm_lambda_re, ssm_lambda_im, ssm_log_step, ssm_b_re, ssm_b_im, ssm_c_re, ssm_c_im, ssm_d, ssm_w_glu, w_branch_b, conv_w, conv_b, lru_w_a, lru_b_a, lru_w_x, lru_b_x, lru_lambda, w_branch_c, w_out, ln_g, ln_b):
    depth = w_in.shape[0]
    alpha = (2 * depth) ** 0.25
    pb, pt, _ = x_prompt.shape
    sb, st, _ = x_sample.shape
    s_nb = sb // S_NBG

    wp = _weights(st, w_in, sinks, w_branch_a, ssm_lambda_re, ssm_lambda_im, ssm_log_step, ssm_b_re, ssm_b_im,
                  ssm_c_re, ssm_c_im, ssm_d, ssm_w_glu, w_branch_b, conv_w, conv_b, lru_w_a, lru_b_a, lru_w_x,
                  lru_b_x, lru_lambda, w_branch_c, w_out, ln_g, ln_b)
    p_tab = _prompt_table(rel_bias, P_TT)
    s_tabs = _sample_tables(rel_bias, st)

    st_p = []
    x = x_prompt
    for l in range(depth):
        outs = _prompt_mixer_call(x, wp, p_tab, l, nb=pb, tt=P_TT, natural_in=(l == 0))
        if l == 0:
            z, gm, x, *sp = outs
        else:
            z, gm, *sp = outs
        x = _prompt_out_call(x, z, gm, wp, l, nb=pb, mb=OUT_ROWS, alpha=alpha, natural_out=(l == depth - 1))
        st_p.append(_states_out((pb,), *sp))
    y_prompt = x
    stacked_p = [jnp.stack([s[k] for s in st_p]) for k in range(6)]

    n = depth * sb
    s_state = (cache_k.astype(f32).reshape(n, WINDOW, LANES), cache_v.astype(f32).reshape(n, WINDOW, LANES),
               state_ssm_re.astype(f32).reshape(n, SSM_WIDTH), state_ssm_im.astype(f32).reshape(n, SSM_WIDTH),
               state_lru.astype(f32).reshape(n, BRANCH), state_conv.astype(f32).reshape(n, 3 * BRANCH))
    ys, *ss = _sample_call(_group_rows(x_sample, S_NBG, s_nb), wp, s_tabs, s_state, depth=depth, nbg=S_NBG,
                           nb=s_nb, tt=st, alpha=alpha)
    y_sample = _ungroup_rows(ys[(depth - 1) * sb * st:], S_NBG, s_nb, st)
    stacked_s = _states_out((depth, sb), *ss)
    return (y_prompt, y_sample, *stacked_p, *stacked_s)
```

```python
import functools
import math

import jax
import jax.numpy as jnp
from jax import lax
from jax.experimental import pallas as pl
from jax.experimental.pallas import tpu as pltpu

f32 = jnp.float32
bf16 = jnp.bfloat16

D_MODEL = 1024
BRANCH = 512
HEAD_DIM = 64
N_Q_HEADS = 8
WINDOW = 128
N_BUCKETS = 32
MAX_DISTANCE = 128
SSM_GROUPS = 32
SSM_GROUP = 16
SSM_STATE = 64
LRU_BLOCKS = 8
CONV_WIDTH = 4
LRU_C = 8.0
LN_EPS = 1e-5
NEG = -1e30

LANES = 128
SUBLANES = 8
VMEM_LIMIT = 60 * 1024 * 1024

C_Q, C_K, C_V, C_GA, C_UB, C_GB, C_XC, C_GC, C_GM = (
    (0, 512), (512, 640), (640, 768), (768, 1280), (1280, 1792), (1792, 2304),
    (2304, 2816), (2816, 3328), (3328, 6400))
IN_COLS = C_GM[1]
HALF = SSM_GROUPS // 2 * SSM_STATE
SSM_WIDTH = 2 * HALF
N_SLABS = D_MODEL // LANES

V_CONVW, V_CONVB, V_BA, V_BX, V_LAM, V_D = 0, 4, 5, 6, 7, 8


def _split_heads(qa, lo, kvh):
    q_lo = jnp.where(lo, qa, 0.0)
    q_hi = jnp.where(lo, 0.0, qa)
    if kvh == 0:
        return [q_lo, pltpu.roll(q_hi, HEAD_DIM, axis=1)]
    return [pltpu.roll(q_lo, HEAD_DIM, axis=1), q_hi]


def _merge_heads(o0, o1, lo, kvh):
    if kvh == 0:
        return jnp.where(lo, o0, pltpu.roll(o1, HEAD_DIM, axis=1))
    return jnp.where(lo, pltpu.roll(o0, HEAD_DIM, axis=1), o1)


def _softmax_pv(q4, keys, vals, tabs, sink):
    scs = []
    for k, t in zip(keys, tabs):
        sc = lax.dot_general(q4, k, (((1,), (1,)), ((), ())), preferred_element_type=f32)
        scs.append(jnp.where(t > 0.5 * NEG, sc * (HEAD_DIM ** -0.5) + t, NEG))
    mx = sink
    for sc in scs:
        mx = jnp.maximum(mx, jnp.max(sc, axis=-1, keepdims=True))
    den = jnp.exp(sink - mx)
    o = None
    for sc, v in zip(scs, vals):
        p = jnp.exp(sc - mx)
        den = den + jnp.sum(p, axis=-1, keepdims=True)
        pv = jnp.dot(p.astype(bf16), v, preferred_element_type=f32)
        o = pv if o is None else o + pv
    return o / den


class _Filler:
    def __init__(self, thunks):
        self._thunks = list(thunks)
        self._done = 0

    def tick(self, i, n):
        due = -((i + 1) * len(self._thunks) // -n)
        while self._done < due:
            self._thunks[self._done]()
            self._done += 1

    def drain(self):
        self.tick(0, 1)


def _scores_prompt(qs, kvs, sc_scr, *, nb, tt):
    s_len = WINDOW + tt
    lo = lax.broadcasted_iota(jnp.int32, (tt, LANES), 1) < HEAD_DIM
    for b in range(nb):
        kk16 = kvs.at[0][pl.ds(b, s_len, stride=nb), :].astype(bf16)
        for kvh in range(2):
            parts = []
            for i in range(2):
                parts += _split_heads(qs.at[2 * kvh + i][pl.ds(b, tt, stride=nb), :], lo, kvh)
            q4 = jnp.concatenate(parts, axis=0).astype(bf16)
            sc_scr[b, kvh] = lax.dot_general(q4, kk16, (((1,), (1,)), ((), ())), preferred_element_type=f32)


def _softmax_prompt(layer, tabm, sinks_ref, sc_scr, p_scr, filler, *, nb, tt):
    n, it = nb * 2 * 4, 0
    for b in range(nb):
        for kvh in range(2):
            for g in range(4):
                rs = slice(g * tt, (g + 1) * tt)
                t = tabm[kvh, rs, :]
                sc = jnp.where(t > 0.5 * NEG, sc_scr[b, kvh, rs, :] + t, NEG)
                sink = sinks_ref[layer, 4 * kvh + g]
                mx = jnp.maximum(jnp.max(sc, axis=-1, keepdims=True), sink)
                p = jnp.exp(sc - mx)
                den = jnp.sum(p, axis=-1, keepdims=True) + jnp.exp(sink - mx)
                p_scr[b, kvh, rs, :] = (p / den).astype(bf16)
                filler.tick(it, n)
                it += 1


def _pv_thunks(kvs, atts, p_scr, *, nb, tt):
    s_len = WINDOW + tt
    lo = lax.broadcasted_iota(jnp.int32, (tt, LANES), 1) < HEAD_DIM

    def piece(b):
        vv16 = kvs.at[1][pl.ds(b, s_len, stride=nb), :].astype(bf16)
        for kvh in range(2):
            o = jnp.dot(p_scr[b, kvh], vv16, preferred_element_type=f32)
            for i in range(2):
                slab = _merge_heads(o[2 * i * tt:(2 * i + 1) * tt], o[(2 * i + 1) * tt:(2 * i + 2) * tt], lo, kvh)
                atts.at[2 * kvh + i][pl.ds(b, tt, stride=nb), :] = slab

    return [functools.partial(piece, b) for b in range(nb)]


def _attend_sample(qs, kvn, atts, kc_ref, vc_ref, tab1_ref, tab2_ref, sink_ref, ko_ref, vo_ref, *, nb, tt):
    lo = lax.broadcasted_iota(jnp.int32, (SUBLANES * tt, LANES), 1) < HEAD_DIM
    for sg in range(nb // SUBLANES):
        b0 = SUBLANES * sg

        def rows(ref, b0=b0):
            return jnp.concatenate([ref[t * nb + b0:t * nb + b0 + SUBLANES, :] for t in range(tt)], axis=0)

        kc = kc_ref[b0:b0 + SUBLANES].reshape(SUBLANES * WINDOW, LANES).astype(bf16)
        vc = vc_ref[b0:b0 + SUBLANES].reshape(SUBLANES * WINDOW, LANES).astype(bf16)
        kn = rows(kvn.at[0]).astype(bf16)
        vn = rows(kvn.at[1]).astype(bf16)
        for kvh in range(2):
            parts = []
            for i in range(2):
                parts += _split_heads(rows(qs.at[2 * kvh + i]), lo, kvh)
            q4 = jnp.concatenate(parts, axis=0).astype(bf16)
            o = _softmax_pv(q4, [kc, kn], [vc, vn], [tab1_ref[kvh], tab2_ref[kvh]], sink_ref[kvh])
            r = SUBLANES * tt
            for i in range(2):
                slab = _merge_heads(o[2 * i * r:(2 * i + 1) * r], o[(2 * i + 1) * r:(2 * i + 2) * r], lo, kvh)
                for t in range(tt):
                    atts[2 * kvh + i, t * nb + b0:t * nb + b0 + SUBLANES, :] = slab[SUBLANES * t:SUBLANES * (t + 1)]
    for b in range(nb):
        ko_ref[b, 0:WINDOW - tt, :] = kc_ref[b, tt:WINDOW, :]
        vo_ref[b, 0:WINDOW - tt, :] = vc_ref[b, tt:WINDOW, :]
        ko_ref[b, WINDOW - tt:WINDOW, :] = kvn.at[0][pl.ds(b, tt, stride=nb), :]
        vo_ref[b, WINDOW - tt:WINDOW, :] = kvn.at[1][pl.ds(b, tt, stride=nb), :]


def _s5_block(ub, xs, hst, bbh_ref, cch_ref, ab_ref, *, nb, tt):
    ub16 = ub.astype(bf16)
    ys = []
    for hf in range(2):
        _s5_drive(ub16, xs, bbh_ref, hf)
        _s5_scan(xs, hst, ab_ref, hf, nb=nb, tt=tt)
        ys.append(_s5_readout(xs, cch_ref, hf))
    return jnp.concatenate(ys, axis=1)


def _s5_drive(ub16, xs, bbh_ref, hf):
    xs[...] = jnp.dot(ub16[:, 256 * hf:256 * (hf + 1)], bbh_ref[hf], preferred_element_type=f32)


def _s5_readout(xs, cch_ref, hf):
    return jnp.dot(xs[...].astype(bf16), cch_ref[hf], preferred_element_type=f32)


def _s5_scan(xs, hst, ab_ref, hf, *, nb, tt, filler=None):
    nck = 2
    cw = HALF // nck
    n, it = (nb // SUBLANES) * tt, 0
    coef = []
    for ck in range(nck):
        ar = jnp.broadcast_to(ab_ref[2 * hf:2 * hf + 1, cw * ck:cw * (ck + 1)], (SUBLANES, cw))
        ai = jnp.broadcast_to(ab_ref[2 * hf + 1:2 * hf + 2, cw * ck:cw * (ck + 1)], (SUBLANES, cw))
        coef.append((ar, ai))
    for bg in range(nb // SUBLANES):
        b0 = SUBLANES * bg
        rows = slice(b0, b0 + SUBLANES)
        st = [(hst[0, rows, HALF * hf + cw * ck:HALF * hf + cw * (ck + 1)],
               hst[1, rows, HALF * hf + cw * ck:HALF * hf + cw * (ck + 1)]) for ck in range(nck)]
        for t in range(tt):
            r0 = t * nb + b0
            for ck in range(nck):
                re0, im0 = cw * ck, HALF + cw * ck
                (ar, ai), (hr, hi) = coef[ck], st[ck]
                xr = xs[r0:r0 + SUBLANES, re0:re0 + cw]
                xi = xs[r0:r0 + SUBLANES, im0:im0 + cw]
                hr, hi = ar * hr - ai * hi + xr, ar * hi + ai * hr + xi
                xs[r0:r0 + SUBLANES, re0:re0 + cw] = hr
                xs[r0:r0 + SUBLANES, im0:im0 + cw] = hi
                st[ck] = (hr, hi)
            if filler is not None:
                filler.tick(it, n)
            it += 1
        for ck in range(nck):
            hst[0, rows, HALF * hf + cw * ck:HALF * hf + cw * (ck + 1)] = st[ck][0]
            hst[1, rows, HALF * hf + cw * ck:HALF * hf + cw * (ck + 1)] = st[ck][1]


def _lru_block(xcs, la, lb, lst, wlru_ref, vec_ref, *, nb, tt):
    _lru_prep(xcs, la, lb, wlru_ref, vec_ref, nb=nb, tt=tt)
    _lru_scan(la, lb, lst, nb=nb, tt=tt)


def _lru_prep(xcs, la, lb, wlru_ref, vec_ref, *, nb, tt):
    m = nb * tt
    conv = vec_ref[V_CONVB:V_CONVB + 1, :] + vec_ref[V_CONVW:V_CONVW + 1, :] * xcs[0:m, :]
    for j in range(1, CONV_WIDTH):
        conv = conv + vec_ref[V_CONVW + j:V_CONVW + j + 1, :] * xcs[j * nb:j * nb + m, :]
    c16 = conv.astype(bf16)

    def block_dot(g):
        return jnp.concatenate(
            [jnp.dot(c16[:, 256 * hf:256 * (hf + 1)], wlru_ref[g, hf], preferred_element_type=f32)
             for hf in range(2)], axis=1)

    r = jax.nn.sigmoid(block_dot(0) + vec_ref[V_BA:V_BA + 1, :])
    i = jax.nn.sigmoid(block_dot(1) + vec_ref[V_BX:V_BX + 1, :])
    log_a = LRU_C * r * jax.nn.log_sigmoid(vec_ref[V_LAM:V_LAM + 1, :])
    a = jnp.exp(log_a)
    la[...] = a
    g2 = -jnp.tanh(log_a) * (a * a + 1.0)
    lb[...] = conv * i * jnp.where(g2 > 0.0, g2 * lax.rsqrt(g2), 0.0)


def _lru_scan(la, lb, lst, *, nb, tt, filler=None):
    n, it = (nb // SUBLANES) * tt, 0
    for bg in range(nb // SUBLANES):
        b0 = SUBLANES * bg
        h = lst[b0:b0 + SUBLANES, :]
        for t in range(tt):
            r0 = t * nb + b0
            h = la[r0:r0 + SUBLANES, :] * h + lb[r0:r0 + SUBLANES, :]
            lb[r0:r0 + SUBLANES, :] = h
            if filler is not None:
                filler.tick(it, n)
            it += 1
        lst[b0:b0 + SUBLANES, :] = h


def _mixer_tail(xb, win_ref, atts, xs, hst, xcs, la, lb, lst, bbh_ref, cch_ref, ab_ref, wglu_ref, wlru_ref,
                vec_ref, *, nb, tt):
    m = nb * tt

    def proj(cols):
        return jnp.dot(xb, win_ref[:, cols[0]:cols[1]], preferred_element_type=f32)

    att = jnp.concatenate([atts[j] for j in range(4)], axis=1)
    za = (att * jax.nn.silu(proj(C_GA))).astype(bf16)

    ub = proj(C_UB)
    y = _s5_block(ub, xs, hst, bbh_ref, cch_ref, ab_ref, nb=nb, tt=tt) + vec_ref[V_D:V_D + 1, :] * ub
    zz = jax.nn.gelu(y)
    zz = zz * jax.nn.sigmoid(jnp.dot(zz.astype(bf16), wglu_ref[...], preferred_element_type=f32))
    zb = (zz * jax.nn.silu(proj(C_GB))).astype(bf16)

    xcs[3 * nb:3 * nb + m, :] = proj(C_XC)
    _lru_block(xcs, la, lb, lst, wlru_ref, vec_ref, nb=nb, tt=tt)
    zc = (lb[...] * jax.nn.silu(proj(C_GC))).astype(bf16)
    return za, zb, zc


def _merge_gate(xb, win_ref, k):
    c0 = C_GM[0] + D_MODEL * k
    return jax.nn.sigmoid(jnp.dot(xb, win_ref[:, c0:c0 + D_MODEL], preferred_element_type=f32))


def _out_core(x, zs, gates, wbr_refs, wout_ref, ln_g, ln_b, alpha):
    merged = None
    for k in range(3):
        g = gates(k)
        yk = jnp.dot(zs[k], wbr_refs[k][...], preferred_element_type=f32)
        merged = g * yk if merged is None else merged + g * yk
    out = jnp.dot(merged.astype(bf16), wout_ref[...], preferred_element_type=f32)
    h = alpha * x + out
    mu = jnp.mean(h, axis=-1, keepdims=True)
    d = h - mu
    var = jnp.mean(d * d, axis=-1, keepdims=True)
    return d * lax.rsqrt(var + LN_EPS) * ln_g + ln_b


def _prompt_mixer_kernel(*refs, layer, nb, tt, nts, natural_in):
    (x_ref, win_ref, tab_ref, sinks_ref, bbh_ref, cch_ref, ab_ref, wglu_ref, wlru_ref, vec_ref), refs = refs[:10], refs[10:]
    (z_ref, gm_ref), refs = refs[:2], refs[2:]
    if natural_in:
        xtb_ref, refs = refs[0], refs[1:]
    (ko_ref, vo_ref, sre_ref, sim_ref, lruo_ref, convo_ref,
     qs, kvs, atts, xs, hst, xcs, la, lb, lst, tabm, sc_scr, p_scr, gact), refs = refs[:19], refs[19:]
    ts = pl.program_id(0)
    m = nb * tt
    w = WINDOW * nb
    last = ts == nts - 1

    @pl.when(ts == 0)
    def _zero_state():
        kvs[:, 0:w, :] = jnp.zeros((2, w, LANES), f32)
        hst[...] = jnp.zeros(hst.shape, f32)
        lst[...] = jnp.zeros(lst.shape, f32)
        xcs[0:3 * nb, :] = jnp.zeros((3 * nb, BRANCH), f32)

    if natural_in:
        (xsl,) = refs
        for b in range(nb):
            for k in range(N_SLABS):
                xsl.at[k][pl.ds(b, tt, stride=nb), :] = x_ref[b, :, LANES * k:LANES * (k + 1)]
        x = jnp.concatenate([xsl[k] for k in range(N_SLABS)], axis=1)
        xtb_ref[...] = x
    else:
        x = x_ref[...]
    xb = x.astype(bf16)

    def proj(cols):
        return jnp.dot(xb, win_ref[:, cols[0]:cols[1]], preferred_element_type=f32)

    q = proj(C_Q)
    for j in range(4):
        qs[j] = q[:, LANES * j:LANES * (j + 1)] * (HEAD_DIM ** -0.5)
    kv = proj((C_K[0], C_V[1]))
    kvs[0, w:w + m, :] = kv[:, :LANES]
    kvs[1, w:w + m, :] = kv[:, LANES:]
    kj = lax.broadcasted_iota(jnp.int32, (2, 4 * tt, WINDOW + tt), 2)
    tabm[...] = jnp.where(kj + ts * tt >= WINDOW, tab_ref[...], NEG)
    _scores_prompt(qs, kvs, sc_scr, nb=nb, tt=tt)
    ub = proj(C_UB)
    ub16 = ub.astype(bf16)
    _s5_drive(ub16, xs, bbh_ref, 0)

    def gate_piece(j):
        c0 = C_GM[0] + BRANCH * j
        g = jax.nn.sigmoid(jnp.dot(xb, win_ref[:, c0:c0 + BRANCH], preferred_element_type=f32))
        gm_ref[:, BRANCH * j:BRANCH * (j + 1)] = g.astype(bf16)

    gate_pieces = [functools.partial(gate_piece, j) for j in range(3 * D_MODEL // BRANCH)]
    fill = _Filler(gate_pieces)
    _softmax_prompt(layer, tabm, sinks_ref, sc_scr, p_scr, fill, nb=nb, tt=tt)
    fill.drain()

    def branch_a():
        att = jnp.concatenate([atts[j] for j in range(4)], axis=1)
        z_ref[:, 0:BRANCH] = (att * jax.nn.silu(proj(C_GA))).astype(bf16)

    def conv_in():
        xcs[3 * nb:3 * nb + m, :] = proj(C_XC)

    fill = _Filler(_pv_thunks(kvs, atts, p_scr, nb=nb, tt=tt) + [branch_a, conv_in])
    _s5_scan(xs, hst, ab_ref, 0, nb=nb, tt=tt, filler=fill)
    fill.drain()
    kvs[:, 0:w, :] = kvs[:, m:m + w, :]

    y0 = _s5_readout(xs, cch_ref, 0)
    _lru_prep(xcs, la, lb, wlru_ref, vec_ref, nb=nb, tt=tt)
    _s5_drive(ub16, xs, bbh_ref, 1)

    def gate_act(k, cols):
        gact[k] = jax.nn.silu(proj(cols))

    fill = _Filler([functools.partial(gate_act, 0, C_GB), functools.partial(gate_act, 1, C_GC)])
    _s5_scan(xs, hst, ab_ref, 1, nb=nb, tt=tt, filler=fill)
    fill.drain()

    def branch_b():
        y = jnp.concatenate([y0, _s5_readout(xs, cch_ref, 1)], axis=1) + vec_ref[V_D:V_D + 1, :] * ub
        zz = jax.nn.gelu(y)
        zz = zz * jax.nn.sigmoid(jnp.dot(zz.astype(bf16), wglu_ref[...], preferred_element_type=f32))
        z_ref[:, BRANCH:2 * BRANCH] = (zz * gact[0]).astype(bf16)

    fill = _Filler([branch_b])
    _lru_scan(la, lb, lst, nb=nb, tt=tt, filler=fill)
    fill.drain()
    z_ref[:, 2 * BRANCH:3 * BRANCH] = (lb[...] * gact[1]).astype(bf16)

    @pl.when(last)
    def _store_state():
        sre_ref[...] = hst[0]
        sim_ref[...] = hst[1]
        lruo_ref[...] = lst[...]
        for t in range(3):
            convo_ref[:, BRANCH * t:BRANCH * (t + 1)] = xcs[m + t * nb:m + (t + 1) * nb, :]
        for b in range(nb):
            ko_ref[b] = kvs.at[0][pl.ds(b, WINDOW, stride=nb), :]
            vo_ref[b] = kvs.at[1][pl.ds(b, WINDOW, stride=nb), :]

    xcs[0:3 * nb, :] = xcs[m:m + 3 * nb, :]


def _prompt_out_kernel(*refs, layer, alpha, nb, natural_out):
    if natural_out:
        x_ref, z_ref, gm_ref, wa_ref, wb_ref, wc_ref, wout_ref, lng_ref, lnb_ref, y_ref, ysl = refs
    else:
        x_ref, z_ref, gm_ref, wa_ref, wb_ref, wc_ref, wout_ref, lng_ref, lnb_ref, y_ref = refs
    zs = [z_ref[:, BRANCH * k:BRANCH * (k + 1)] for k in range(3)]
    gates = lambda k: gm_ref[:, D_MODEL * k:D_MODEL * (k + 1)].astype(f32)
    y = _out_core(x_ref[...], zs, gates, (wa_ref, wb_ref, wc_ref), wout_ref,
                  lng_ref[layer:layer + 1, :], lnb_ref[layer:layer + 1, :], alpha)
    if natural_out:
        tt = y.shape[0] // nb
        for k in range(N_SLABS):
            ysl[k] = y[:, LANES * k:LANES * (k + 1)]
        for b in range(nb):
            for k in range(N_SLABS):
                y_ref[b, :, LANES * k:LANES * (k + 1)] = ysl.at[k][pl.ds(b, tt, stride=nb), :]
    else:
        y_ref[...] = y


def _sample_kernel(x_ref, win_ref, tab1_ref, tab2_ref, sink_ref, bbh_ref, cch_ref, ab_ref, wglu_ref, wlru_ref,
                   vec_ref, wa_ref, wb_ref, wc_ref, wout_ref, lng_ref, lnb_ref,
                   kc_ref, vc_ref, sre0_ref, sim0_ref, lru0_ref, conv0_ref,
                   y_ref, ko_ref, vo_ref, sre_ref, sim_ref, lruo_ref, convo_ref,
                   xcar, qs, kvn, atts, xs, hst, xcs, la, lb, lst, *, alpha, nb, tt):
    layer = pl.program_id(0)
    g = pl.program_id(1)
    m = nb * tt

    @pl.when(layer == 0)
    def _first_layer_input():
        xcar[g] = x_ref[...]

    x = xcar[g]
    xb = x.astype(bf16)
    hst[0] = sre0_ref[...]
    hst[1] = sim0_ref[...]
    lst[...] = lru0_ref[...]
    for t in range(3):
        xcs[t * nb:(t + 1) * nb, :] = conv0_ref[:, BRANCH * t:BRANCH * (t + 1)]

    def proj(cols):
        return jnp.dot(xb, win_ref[:, cols[0]:cols[1]], preferred_element_type=f32)

    q = proj(C_Q)
    for j in range(4):
        qs[j] = q[:, LANES * j:LANES * (j + 1)]
    kv = proj((C_K[0], C_V[1]))
    kvn[0] = kv[:, :LANES]
    kvn[1] = kv[:, LANES:]
    _attend_sample(qs, kvn, atts, kc_ref, vc_ref, tab1_ref, tab2_ref, sink_ref, ko_ref, vo_ref, nb=nb, tt=tt)
    zs = _mixer_tail(xb, win_ref, atts, xs, hst, xcs, la, lb, lst, bbh_ref, cch_ref, ab_ref, wglu_ref,
                     wlru_ref, vec_ref, nb=nb, tt=tt)
    sre_ref[...] = hst[0]
    sim_ref[...] = hst[1]
    lruo_ref[...] = lst[...]
    for t in range(3):
        convo_ref[:, BRANCH * t:BRANCH * (t + 1)] = xcs[m + t * nb:m + (t + 1) * nb, :]

    y = _out_core(x, zs, functools.partial(_merge_gate, xb, win_ref), (wa_ref, wb_ref, wc_ref), wout_ref,
                  lng_ref[pl.ds(layer, 1), :], lnb_ref[pl.ds(layer, 1), :], alpha)
    xcar[g] = y
    y_ref[...] = y


def _layer_spec(shape, layer=None):
    if layer is None:
        idx = lambda l, *_: (l,) + (0,) * len(shape)
    else:
        idx = lambda *_: (layer,) + (0,) * len(shape)
    return pl.BlockSpec((None,) + tuple(shape), idx, pipeline_mode=pl.Buffered(1))


def _const_spec(shape):
    return pl.BlockSpec(tuple(shape), lambda *_: (0,) * len(shape), pipeline_mode=pl.Buffered(1))


def _mixer_weight_specs(layer):
    return [
        _layer_spec((2, 256, SSM_WIDTH), layer),
        _layer_spec((2, SSM_WIDTH, 256), layer),
        _layer_spec((4, HALF), layer),
        _layer_spec((BRANCH, BRANCH), layer),
        _layer_spec((2, 2, 256, 256), layer),
        _layer_spec((16, BRANCH), layer),
    ]


def _mixer_scratch(nb, tt, kv_rows):
    m = nb * tt
    return [
        pltpu.VMEM((4, m, LANES), f32),
        pltpu.VMEM((2, kv_rows, LANES), f32),
        pltpu.VMEM((4, m, LANES), f32),
        pltpu.VMEM((m, SSM_WIDTH), f32),
        pltpu.VMEM((2, nb, SSM_WIDTH), f32),
        pltpu.VMEM((m + 3 * nb, BRANCH), f32),
        pltpu.VMEM((m, BRANCH), f32),
        pltpu.VMEM((m, BRANCH), f32),
        pltpu.VMEM((nb, BRANCH), f32),
    ]


def _prompt_mixer_call(x, wp, tab, layer, *, nb, tt, natural_in):
    nts = (x.shape[1] if natural_in else x.shape[0] // nb) // tt
    m, w, s_len = nb * tt, WINDOW * nb, WINDOW + tt
    kern = functools.partial(_prompt_mixer_kernel, layer=layer, nb=nb, tt=tt, nts=nts, natural_in=natural_in)
    x_spec = (pl.BlockSpec((nb, tt, D_MODEL), lambda t: (0, t, 0)) if natural_in
              else pl.BlockSpec((m, D_MODEL), lambda t: (t, 0)))
    in_specs = [x_spec,
                _layer_spec((D_MODEL, IN_COLS), layer),
                _const_spec(tab.shape), pl.BlockSpec(memory_space=pltpu.SMEM), *_mixer_weight_specs(layer)]
    whole = lambda shape: pl.BlockSpec(shape, lambda t: (0,) * len(shape))
    out_specs = [pl.BlockSpec((m, 3 * BRANCH), lambda t: (t, 0)), pl.BlockSpec((m, 3 * D_MODEL), lambda t: (t, 0))]
    out_shape = [jax.ShapeDtypeStruct((nts * m, 3 * BRANCH), bf16),
                 jax.ShapeDtypeStruct((nts * m, 3 * D_MODEL), bf16)]
    if natural_in:
        out_specs.append(pl.BlockSpec((m, D_MODEL), lambda t: (t, 0)))
        out_shape.append(jax.ShapeDtypeStruct((nts * m, D_MODEL), f32))
    state_shapes = [(nb, WINDOW, LANES), (nb, WINDOW, LANES), (nb, SSM_WIDTH), (nb, SSM_WIDTH),
                    (nb, BRANCH), (nb, 3 * BRANCH)]
    out_specs += [whole(s) for s in state_shapes]
    out_shape += [jax.ShapeDtypeStruct(s, f32) for s in state_shapes]
    scratch = _mixer_scratch(nb, tt, w + m) + [
        pltpu.VMEM((2, 4 * tt, s_len), f32),
        pltpu.VMEM((nb, 2, 4 * tt, s_len), f32),
        pltpu.VMEM((nb, 2, 4 * tt, s_len), bf16),
        pltpu.VMEM((2, m, BRANCH), f32),
    ]
    if natural_in:
        scratch.append(pltpu.VMEM((N_SLABS, m, LANES), f32))
    return pl.pallas_call(
        kern, grid=(nts,), in_specs=in_specs, out_specs=out_specs, out_shape=out_shape,
        scratch_shapes=scratch, name=f"mixer_prompt_{layer}",
        compiler_params=pltpu.CompilerParams(dimension_semantics=("arbitrary",), vmem_limit_bytes=VMEM_LIMIT),
    )(x, wp["win"], tab, wp["sinks"], wp["bbh"], wp["cch"], wp["ab"], wp["wglu"], wp["wlru"], wp["vec"])


def _prompt_out_call(x2d, z2d, gm2d, wp, layer, *, nb, mb, alpha, natural_out):
    rows = x2d.shape[0]
    kern = functools.partial(_prompt_out_kernel, layer=layer, alpha=alpha, nb=nb, natural_out=natural_out)
    if natural_out:
        out_spec = pl.BlockSpec((nb, mb // nb, D_MODEL), lambda i: (0, i, 0))
        out_shape = jax.ShapeDtypeStruct((nb, rows // nb, D_MODEL), f32)
        scratch = [pltpu.VMEM((N_SLABS, mb, LANES), f32)]
    else:
        out_spec = pl.BlockSpec((mb, D_MODEL), lambda i: (i, 0))
        out_shape = jax.ShapeDtypeStruct((rows, D_MODEL), f32)
        scratch = []
    return pl.pallas_call(
        kern, grid=(rows // mb,),
        in_specs=[
            pl.BlockSpec((mb, D_MODEL), lambda i: (i, 0)),
            pl.BlockSpec((mb, 3 * BRANCH), lambda i: (i, 0)),
            pl.BlockSpec((mb, 3 * D_MODEL), lambda i: (i, 0)),
            _layer_spec((BRANCH, D_MODEL), layer), _layer_spec((BRANCH, D_MODEL), layer),
            _layer_spec((BRANCH, D_MODEL), layer),
            _layer_spec((D_MODEL, D_MODEL), layer),
            _const_spec(wp["ln_g"].shape), _const_spec(wp["ln_b"].shape),
        ],
        out_specs=out_spec, out_shape=out_shape, scratch_shapes=scratch, name=f"out_prompt_{layer}",
        compiler_params=pltpu.CompilerParams(dimension_semantics=("arbitrary",), vmem_limit_bytes=VMEM_LIMIT),
    )(x2d, z2d, gm2d, wp["wbr_a"], wp["wbr_b"], wp["wbr_c"], wp["wout"], wp["ln_g"], wp["ln_b"])


def _sample_call(x2d, wp, tabs, state, *, depth, nbg, nb, tt, alpha):
    m = nb * tt
    kern = functools.partial(_sample_kernel, alpha=alpha, nb=nb, tt=tt)
    row_spec = pl.BlockSpec((m, D_MODEL), lambda l, g: (g, 0))
    st = lambda *shape: pl.BlockSpec((nb,) + shape, lambda l, g: (l * nbg + g,) + (0,) * len(shape))
    state_specs = [st(WINDOW, LANES), st(WINDOW, LANES), st(SSM_WIDTH), st(SSM_WIDTH), st(BRANCH), st(3 * BRANCH)]
    state_shapes = [(WINDOW, LANES), (WINDOW, LANES), (SSM_WIDTH,), (SSM_WIDTH,), (BRANCH,), (3 * BRANCH,)]
    in_specs = [row_spec, _layer_spec((D_MODEL, IN_COLS)), _const_spec(tabs[0].shape), _const_spec(tabs[1].shape),
                _layer_spec(wp["sinkcol"].shape[1:]), *_mixer_weight_specs(None),
                _layer_spec((BRANCH, D_MODEL)), _layer_spec((BRANCH, D_MODEL)), _layer_spec((BRANCH, D_MODEL)),
                _layer_spec((D_MODEL, D_MODEL)), _const_spec(wp["ln_g"].shape), _const_spec(wp["ln_b"].shape),
                *state_specs]
    out_specs = [pl.BlockSpec((m, D_MODEL), lambda l, g: (l * nbg + g, 0)), *state_specs]
    out_shape = [jax.ShapeDtypeStruct((depth * nbg * m, D_MODEL), f32)] + [
        jax.ShapeDtypeStruct((depth * nbg * nb,) + s, f32) for s in state_shapes]
    scratch = [pltpu.VMEM((nbg, m, D_MODEL), f32)] + _mixer_scratch(nb, tt, m)
    return pl.pallas_call(
        kern, grid=(depth, nbg), in_specs=in_specs, out_specs=out_specs, out_shape=out_shape,
        scratch_shapes=scratch, name="sample_layers",
        compiler_params=pltpu.CompilerParams(
            dimension_semantics=("arbitrary", "arbitrary"), vmem_limit_bytes=VMEM_LIMIT),
    )(x2d, wp["win"], tabs[0], tabs[1], wp["sinkcol"], wp["bbh"], wp["cch"], wp["ab"], wp["wglu"], wp["wlru"],
      wp["vec"], wp["wbr_a"], wp["wbr_b"], wp["wbr_c"], wp["wout"], wp["ln_g"], wp["ln_b"], *state)


def _t5_bucket(dist):
    max_exact = N_BUCKETS // 2
    d = jnp.maximum(dist, 0)
    large = max_exact + (jnp.log(jnp.maximum(d, 1).astype(f32) / max_exact)
                         / math.log(MAX_DISTANCE / max_exact) * (N_BUCKETS - max_exact)).astype(jnp.int32)
    large = jnp.minimum(large, N_BUCKETS - 1)
    return jnp.where(d < max_exact, d, large)


def _band_bias(rel_bias, tt):
    s_len = WINDOW + tt
    by_dist = rel_bias.astype(f32)[_t5_bucket(jnp.arange(WINDOW))].T
    row0 = jnp.concatenate([jnp.full((N_Q_HEADS, 1), NEG, f32), by_dist[:, ::-1],
                            jnp.full((N_Q_HEADS, tt), NEG, f32)], axis=1)
    flat = jnp.tile(row0, (1, tt))[:, :tt * s_len]
    return flat.reshape(N_Q_HEADS, tt, s_len)


def _prompt_table(rel_bias, tt):
    return _band_bias(rel_bias, tt).reshape(2, 4 * tt, WINDOW + tt)


def _sample_tables(rel_bias, tt):
    bs = _band_bias(rel_bias, tt).reshape(2, 4, tt, WINDOW + tt)
    same = jnp.eye(SUBLANES, dtype=bool)
    t1 = jnp.where(same[None, None, None, :, :, None], bs[:, :, :, None, None, :WINDOW], NEG)
    t2 = jnp.where(same[None, None, None, :, None, :], bs[:, :, :, None, WINDOW:, None], NEG)
    rows = 4 * tt * SUBLANES
    return t1.reshape(2, rows, SUBLANES * WINDOW), t2.reshape(2, rows, tt * SUBLANES)


def _weights(s_tt, w_in, sinks, w_branch_a, lam_re, lam_im, log_step, b_re, b_im, c_re, c_im, ssm_d, w_glu,
             w_branch_b, conv_w, conv_b, w_a, b_a, w_x, b_x, lam, w_branch_c, w_out, ln_g, ln_b):
    depth = w_in.shape[0]
    hg = SSM_GROUPS // 2
    lr = jnp.minimum(lam_re.astype(f32), -1e-4)
    li = lam_im.astype(f32)
    step = jnp.exp(log_step.astype(f32))[..., None]
    mag = jnp.exp(lr * step)
    ab_re = mag * jnp.cos(li * step)
    ab_im = mag * jnp.sin(li * step)
    den = lr * lr + li * li
    f_re = ((ab_re - 1.0) * lr + ab_im * li) / den
    f_im = (ab_im * lr - (ab_re - 1.0) * li) / den
    bre, bim = b_re.astype(f32), b_im.astype(f32)
    bb_re = f_re[..., None] * bre - f_im[..., None] * bim
    bb_im = f_re[..., None] * bim + f_im[..., None] * bre

    def expand(blocks, n):
        c = blocks.shape[-1]
        per = LANES // c
        owner = per * jnp.arange(n // per)[:, None] + jnp.arange(LANES)[None, :] // c
        mask = owner[None] == jnp.arange(n)[:, None, None]
        tiled = jnp.tile(blocks, (1,) * (blocks.ndim - 1) + (per,))[..., None, :]
        return tiled, mask

    drv = jnp.stack([bb_re, bb_im], axis=1).reshape(depth, 2, 2, hg, SSM_STATE, SSM_GROUP)
    tiled, mask = expand(drv.transpose(0, 2, 3, 5, 1, 4), hg)
    bbh = jnp.where(mask[None, None, :, None, None, :, :], tiled, 0.0).astype(bf16)
    bbh = bbh.reshape(depth, 2, hg * SSM_GROUP, SSM_WIDTH)
    rdo = jnp.stack([c_re.astype(f32), -c_im.astype(f32)], axis=1).reshape(depth, 2, 2, hg, SSM_GROUP, SSM_STATE)
    tiled, mask = expand(rdo.transpose(0, 2, 1, 3, 5, 4), hg)
    cch = jnp.where(mask[None, None, None, :, None, :, :], tiled, 0.0).astype(bf16)
    cch = cch.reshape(depth, 2, SSM_WIDTH, hg * SSM_GROUP)
    ab = jnp.stack([ab_re.reshape(depth, 2, HALF), ab_im.reshape(depth, 2, HALF)], axis=2).reshape(depth, 4, HALF)
    nblk = LRU_BLOCKS // 2
    blk = jnp.stack([w_a.astype(f32), w_x.astype(f32)], axis=1).reshape(depth, 2, 2, nblk, 64, 64)
    tiled, mask = expand(blk, nblk)
    wlru = jnp.where(mask[None, None, None, :, None, :, :], tiled, 0.0).reshape(depth, 2, 2, 256, 256)
    row = lambda v: v.astype(f32)[:, None, :]
    vec = jnp.concatenate([conv_w.astype(f32), row(conv_b), row(b_a), row(b_x), row(lam), row(ssm_d),
                           jnp.zeros((depth, 16 - V_D - 1, BRANCH), f32)], axis=1)
    sk = sinks.astype(f32).reshape(depth, 2, 4, 1)
    return {
        "win": w_in.astype(bf16),
        "sinks": sinks.astype(f32),
        "sinkcol": jnp.broadcast_to(sk, (depth, 2, 4, s_tt * SUBLANES)).reshape(depth, 2, 4 * s_tt * SUBLANES, 1),
        "bbh": bbh, "cch": cch, "ab": ab,
        "wglu": w_glu.astype(bf16),
        "wlru": wlru.astype(bf16),
        "vec": vec,
        "wbr_a": w_branch_a.astype(bf16), "wbr_b": w_branch_b.astype(bf16), "wbr_c": w_branch_c.astype(bf16),
        "wout": w_out.astype(bf16),
        "ln_g": ln_g.astype(f32), "ln_b": ln_b.astype(f32),
    }


def _group_rows(x, nbg, nb):
    b, t, c = x.shape
    return x.reshape(nbg, nb, t, c).transpose(0, 2, 1, 3).reshape(nbg * t * nb, c)


def _ungroup_rows(x2d, nbg, nb, t):
    c = x2d.shape[-1]
    return x2d.reshape(nbg, t, nb, c).transpose(0, 2, 1, 3).reshape(nbg * nb, t, c)


def _states_out(lead, ko, vo, sre, sim, lru, conv):
    return (ko.reshape(*lead, WINDOW, 2, HEAD_DIM), vo.reshape(*lead, WINDOW, 2, HEAD_DIM),
            sre.reshape(*lead, SSM_GROUPS, SSM_STATE), sim.reshape(*lead, SSM_GROUPS, SSM_STATE),
            lru.reshape(*lead, BRANCH), conv.reshape(*lead, CONV_WIDTH - 1, BRANCH))


P_TT = 64
OUT_ROWS = 512
S_NBG = 4


def kernel(x_prompt, x_sample, cache_k, cache_v, state_ssm_re, state_ssm_im, state_lru, state_conv, rel_bias, w_in, sinks, w_branch_a, ssm_lambda_re, ssm_lambda_im, ssm_log_step, ssm_b_re, ssm_b_im, ssm_c_re, ssm_c_im, ssm_d, ssm_w_glu, w_branch_b, conv_w, conv_b, lru_w_a, lru_b_a, lru_w_x, lru_b_x, lru_lambda, w_branch_c, w_out, ln_g, ln_b):
    depth = w_in.shape[0]
    alpha = (2 * depth) ** 0.25
    pb, pt, _ = x_prompt.shape
    sb, st, _ = x_sample.shape
    s_nb = sb // S_NBG

    wp = _weights(st, w_in, sinks, w_branch_a, ssm_lambda_re, ssm_lambda_im, ssm_log_step, ssm_b_re, ssm_b_im,
                  ssm_c_re, ssm_c_im, ssm_d, ssm_w_glu, w_branch_b, conv_w, conv_b, lru_w_a, lru_b_a, lru_w_x,
                  lru_b_x, lru_lambda, w_branch_c, w_out, ln_g, ln_b)
    p_tab = _prompt_table(rel_bias, P_TT)
    s_tabs = _sample_tables(rel_bias, st)

    st_p = []
    x = x_prompt
    for l in range(depth):
        outs = _prompt_mixer_call(x, wp, p_tab, l, nb=pb, tt=P_TT, natural_in=(l == 0))
        if l == 0:
            z, gm, x, *sp = outs
        else:
            z, gm, *sp = outs
        x = _prompt_out_call(x, z, gm, wp, l, nb=pb, mb=OUT_ROWS, alpha=alpha, natural_out=(l == depth - 1))
        st_p.append(_states_out((pb,), *sp))
    y_prompt = x
    stacked_p = [jnp.stack([s[k] for s in st_p]) for k in range(6)]

    n = depth * sb
    s_state = (cache_k.astype(f32).reshape(n, WINDOW, LANES), cache_v.astype(f32).reshape(n, WINDOW, LANES),
               state_ssm_re.astype(f32).reshape(n, SSM_WIDTH), state_ssm_im.astype(f32).reshape(n, SSM_WIDTH),
               state_lru.astype(f32).reshape(n, BRANCH), state_conv.astype(f32).reshape(n, 3 * BRANCH))
    ys, *ss = _sample_call(_group_rows(x_sample, S_NBG, s_nb), wp, s_tabs, s_state, depth=depth, nbg=S_NBG,
                           nb=s_nb, tt=st, alpha=alpha)
    y_sample = _ungroup_rows(ys[(depth - 1) * sb * st:], S_NBG, s_nb, st)
    stacked_s = _states_out((depth, sb), *ss)
    return (y_prompt, y_sample, *stacked_p, *stacked_s)
```

```python
import functools
import math

import jax
import jax.numpy as jnp
from jax import lax
from jax.experimental import pallas as pl
from jax.experimental.pallas import tpu as pltpu

f32 = jnp.float32
bf16 = jnp.bfloat16

D_MODEL = 1024
BRANCH = 512
HEAD_DIM = 64
N_Q_HEADS = 8
WINDOW = 128
N_BUCKETS = 32
MAX_DISTANCE = 128
SSM_GROUPS = 32
SSM_GROUP = 16
SSM_STATE = 64
LRU_BLOCKS = 8
CONV_WIDTH = 4
LRU_C = 8.0
LN_EPS = 1e-5
NEG = -1e30

LANES = 128
SUBLANES = 8
VMEM_LIMIT = 60 * 1024 * 1024

C_Q, C_K, C_V, C_GA, C_UB, C_GB, C_XC, C_GC, C_GM = (
    (0, 512), (512, 640), (640, 768), (768, 1280), (1280, 1792), (1792, 2304),
    (2304, 2816), (2816, 3328), (3328, 6400))
IN_COLS = C_GM[1]
HALF = SSM_GROUPS // 2 * SSM_STATE
SSM_WIDTH = 2 * HALF
N_SLABS = D_MODEL // LANES

V_CONVW, V_CONVB, V_BA, V_BX, V_LAM, V_D = 0, 4, 5, 6, 7, 8


def _split_heads(qa, lo, kvh):
    q_lo = jnp.where(lo, qa, 0.0)
    q_hi = jnp.where(lo, 0.0, qa)
    if kvh == 0:
        return [q_lo, pltpu.roll(q_hi, HEAD_DIM, axis=1)]
    return [pltpu.roll(q_lo, HEAD_DIM, axis=1), q_hi]


def _merge_heads(o0, o1, lo, kvh):
    if kvh == 0:
        return jnp.where(lo, o0, pltpu.roll(o1, HEAD_DIM, axis=1))
    return jnp.where(lo, pltpu.roll(o0, HEAD_DIM, axis=1), o1)


class _Filler:
    def __init__(self, thunks):
        self._thunks = list(thunks)
        self._done = 0

    def tick(self, i, n):
        due = -((i + 1) * len(self._thunks) // -n)
        while self._done < due:
            self._thunks[self._done]()
            self._done += 1

    def drain(self):
        self.tick(0, 1)


def _scores_prompt(qs, kvs, sc_scr, *, nb, tt):
    s_len = WINDOW + tt
    lo = lax.broadcasted_iota(jnp.int32, (tt, LANES), 1) < HEAD_DIM
    for b in range(nb):
        kk16 = kvs.at[0][pl.ds(b, s_len, stride=nb), :].astype(bf16)
        for kvh in range(2):
            parts = []
            for i in range(2):
                parts += _split_heads(qs.at[2 * kvh + i][pl.ds(b, tt, stride=nb), :], lo, kvh)
            q4 = jnp.concatenate(parts, axis=0).astype(bf16)
            sc_scr[b, kvh] = lax.dot_general(q4, kk16, (((1,), (1,)), ((), ())), preferred_element_type=f32)


def _softmax_prompt(layer, tabm, sinks_ref, sc_scr, p_scr, filler, *, nb, tt):
    n, it = nb * 2 * 4, 0
    for b in range(nb):
        for kvh in range(2):
            for g in range(4):
                rs = slice(g * tt, (g + 1) * tt)
                t = tabm[kvh, rs, :]
                sc = jnp.where(t > 0.5 * NEG, sc_scr[b, kvh, rs, :] + t, NEG)
                sink = sinks_ref[layer, 4 * kvh + g]
                mx = jnp.maximum(jnp.max(sc, axis=-1, keepdims=True), sink)
                p = jnp.exp(sc - mx)
                den = jnp.sum(p, axis=-1, keepdims=True) + jnp.exp(sink - mx)
                p_scr[b, kvh, rs, :] = (p / den).astype(bf16)
                filler.tick(it, n)
                it += 1


def _pv_thunks(kvs, atts, p_scr, *, nb, tt):
    s_len = WINDOW + tt
    lo = lax.broadcasted_iota(jnp.int32, (tt, LANES), 1) < HEAD_DIM

    def piece(b):
        vv16 = kvs.at[1][pl.ds(b, s_len, stride=nb), :].astype(bf16)
        for kvh in range(2):
            o = jnp.dot(p_scr[b, kvh], vv16, preferred_element_type=f32)
            for i in range(2):
                slab = _merge_heads(o[2 * i * tt:(2 * i + 1) * tt], o[(2 * i + 1) * tt:(2 * i + 2) * tt], lo, kvh)
                atts.at[2 * kvh + i][pl.ds(b, tt, stride=nb), :] = slab

    return [functools.partial(piece, b) for b in range(nb)]


def _attend_sample(layer, qs, kvn, atts, kc_ref, vc_ref, tab1_ref, tab2_ref, sinks_ref, ko_ref, vo_ref,
                   sc1, sc2, p1, p2, *, nb, tt):
    r = SUBLANES * tt
    lo = lax.broadcasted_iota(jnp.int32, (r, LANES), 1) < HEAD_DIM
    nsg = nb // SUBLANES

    def rows(ref, b0):
        return jnp.concatenate([ref[t * nb + b0:t * nb + b0 + SUBLANES, :] for t in range(tt)], axis=0)

    nt = (((1,), (1,)), ((), ()))
    for sg in range(nsg):
        b0 = SUBLANES * sg
        kc = kc_ref[b0:b0 + SUBLANES].reshape(SUBLANES * WINDOW, LANES).astype(bf16)
        kn = rows(kvn.at[0], b0).astype(bf16)
        for kvh in range(2):
            parts = []
            for i in range(2):
                parts += _split_heads(rows(qs.at[2 * kvh + i], b0), lo, kvh)
            q4 = jnp.concatenate(parts, axis=0).astype(bf16)
            sc1[sg, kvh] = lax.dot_general(q4, kc, nt, preferred_element_type=f32)
            sc2[sg, kvh] = lax.dot_general(q4, kn, nt, preferred_element_type=f32)

    for sg in range(nsg):
        for kvh in range(2):
            for g in range(4):
                rs = slice(g * r, (g + 1) * r)
                t1, t2 = tab1_ref[kvh, rs, :], tab2_ref[kvh, rs, :]
                s1 = jnp.where(t1 > 0.5 * NEG, sc1[sg, kvh, rs, :] + t1, NEG)
                s2 = jnp.where(t2 > 0.5 * NEG, sc2[sg, kvh, rs, :] + t2, NEG)
                sink = sinks_ref[layer, 4 * kvh + g]
                mx = jnp.maximum(jnp.maximum(jnp.max(s1, axis=-1, keepdims=True),
                                             jnp.max(s2, axis=-1, keepdims=True)), sink)
                e1, e2 = jnp.exp(s1 - mx), jnp.exp(s2 - mx)
                den = (jnp.sum(e1, axis=-1, keepdims=True) + jnp.sum(e2, axis=-1, keepdims=True)
                       + jnp.exp(sink - mx))
                p1[sg, kvh, rs, :] = (e1 / den).astype(bf16)
                p2[sg, kvh, rs, :] = (e2 / den).astype(bf16)

    for sg in range(nsg):
        b0 = SUBLANES * sg
        vc = vc_ref[b0:b0 + SUBLANES].reshape(SUBLANES * WINDOW, LANES).astype(bf16)
        vn = rows(kvn.at[1], b0).astype(bf16)
        for kvh in range(2):
            o = (jnp.dot(p1[sg, kvh], vc, preferred_element_type=f32)
                 + jnp.dot(p2[sg, kvh], vn, preferred_element_type=f32))
            for i in range(2):
                slab = _merge_heads(o[2 * i * r:(2 * i + 1) * r], o[(2 * i + 1) * r:(2 * i + 2) * r], lo, kvh)
                for t in range(tt):
                    atts[2 * kvh + i, t * nb + b0:t * nb + b0 + SUBLANES, :] = slab[SUBLANES * t:SUBLANES * (t + 1)]
    for b in range(nb):
        ko_ref[b, 0:WINDOW - tt, :] = kc_ref[b, tt:WINDOW, :]
        vo_ref[b, 0:WINDOW - tt, :] = vc_ref[b, tt:WINDOW, :]
        ko_ref[b, WINDOW - tt:WINDOW, :] = kvn.at[0][pl.ds(b, tt, stride=nb), :]
        vo_ref[b, WINDOW - tt:WINDOW, :] = kvn.at[1][pl.ds(b, tt, stride=nb), :]


def _s5_block(ub, xs, hst, bbh_ref, cch_ref, ab_ref, *, nb, tt):
    ub16 = ub.astype(bf16)
    ys = []
    for hf in range(2):
        _s5_drive(ub16, xs, bbh_ref, hf)
        _s5_scan(xs, hst, ab_ref, hf, nb=nb, tt=tt)
        ys.append(_s5_readout(xs, cch_ref, hf))
    return jnp.concatenate(ys, axis=1)


def _s5_drive(ub16, xs, bbh_ref, hf):
    xs[...] = jnp.dot(ub16[:, 256 * hf:256 * (hf + 1)], bbh_ref[hf], preferred_element_type=f32)


def _s5_readout(xs, cch_ref, hf):
    return jnp.dot(xs[...].astype(bf16), cch_ref[hf], preferred_element_type=f32)


def _s5_scan(xs, hst, ab_ref, hf, *, nb, tt, filler=None):
    nck = 2
    cw = HALF // nck
    n, it = (nb // SUBLANES) * tt, 0
    coef = []
    for ck in range(nck):
        ar = jnp.broadcast_to(ab_ref[2 * hf:2 * hf + 1, cw * ck:cw * (ck + 1)], (SUBLANES, cw))
        ai = jnp.broadcast_to(ab_ref[2 * hf + 1:2 * hf + 2, cw * ck:cw * (ck + 1)], (SUBLANES, cw))
        coef.append((ar, ai))
    for bg in range(nb // SUBLANES):
        b0 = SUBLANES * bg
        rows = slice(b0, b0 + SUBLANES)
        st = [(hst[0, rows, HALF * hf + cw * ck:HALF * hf + cw * (ck + 1)],
               hst[1, rows, HALF * hf + cw * ck:HALF * hf + cw * (ck + 1)]) for ck in range(nck)]
        for t in range(tt):
            r0 = t * nb + b0
            for ck in range(nck):
                re0, im0 = cw * ck, HALF + cw * ck
                (ar, ai), (hr, hi) = coef[ck], st[ck]
                xr = xs[r0:r0 + SUBLANES, re0:re0 + cw]
                xi = xs[r0:r0 + SUBLANES, im0:im0 + cw]
                hr, hi = ar * hr - ai * hi + xr, ar * hi + ai * hr + xi
                xs[r0:r0 + SUBLANES, re0:re0 + cw] = hr
                xs[r0:r0 + SUBLANES, im0:im0 + cw] = hi
                st[ck] = (hr, hi)
            if filler is not None:
                filler.tick(it, n)
            it += 1
        for ck in range(nck):
            hst[0, rows, HALF * hf + cw * ck:HALF * hf + cw * (ck + 1)] = st[ck][0]
            hst[1, rows, HALF * hf + cw * ck:HALF * hf + cw * (ck + 1)] = st[ck][1]


def _lru_block(xcs, la, lb, lst, wlru_ref, vec_ref, *, nb, tt):
    _lru_prep(xcs, la, lb, wlru_ref, vec_ref, nb=nb, tt=tt)
    _lru_scan(la, lb, lst, nb=nb, tt=tt)


def _lru_prep(xcs, la, lb, wlru_ref, vec_ref, *, nb, tt):
    m = nb * tt
    conv = vec_ref[V_CONVB:V_CONVB + 1, :] + vec_ref[V_CONVW:V_CONVW + 1, :] * xcs[0:m, :]
    for j in range(1, CONV_WIDTH):
        conv = conv + vec_ref[V_CONVW + j:V_CONVW + j + 1, :] * xcs[j * nb:j * nb + m, :]
    c16 = conv.astype(bf16)

    def block_dot(g):
        return jnp.concatenate(
            [jnp.dot(c16[:, 256 * hf:256 * (hf + 1)], wlru_ref[g, hf], preferred_element_type=f32)
             for hf in range(2)], axis=1)

    r = jax.nn.sigmoid(block_dot(0) + vec_ref[V_BA:V_BA + 1, :])
    i = jax.nn.sigmoid(block_dot(1) + vec_ref[V_BX:V_BX + 1, :])
    log_a = LRU_C * r * jax.nn.log_sigmoid(vec_ref[V_LAM:V_LAM + 1, :])
    a = jnp.exp(log_a)
    la[...] = a
    g2 = -jnp.tanh(log_a) * (a * a + 1.0)
    lb[...] = conv * i * jnp.where(g2 > 0.0, g2 * lax.rsqrt(g2), 0.0)


def _lru_scan(la, lb, lst, *, nb, tt, filler=None):
    n, it = (nb // SUBLANES) * tt, 0
    for bg in range(nb // SUBLANES):
        b0 = SUBLANES * bg
        h = lst[b0:b0 + SUBLANES, :]
        for t in range(tt):
            r0 = t * nb + b0
            h = la[r0:r0 + SUBLANES, :] * h + lb[r0:r0 + SUBLANES, :]
            lb[r0:r0 + SUBLANES, :] = h
            if filler is not None:
                filler.tick(it, n)
            it += 1
        lst[b0:b0 + SUBLANES, :] = h


def _mixer_tail(xb, win_ref, atts, xs, hst, xcs, la, lb, lst, bbh_ref, cch_ref, ab_ref, wglu_ref, wlru_ref,
                vec_ref, *, nb, tt):
    m = nb * tt

    def proj(cols):
        return jnp.dot(xb, win_ref[:, cols[0]:cols[1]], preferred_element_type=f32)

    att = jnp.concatenate([atts[j] for j in range(4)], axis=1)
    za = (att * jax.nn.silu(proj(C_GA))).astype(bf16)

    ub = proj(C_UB)
    y = _s5_block(ub, xs, hst, bbh_ref, cch_ref, ab_ref, nb=nb, tt=tt) + vec_ref[V_D:V_D + 1, :] * ub
    zz = jax.nn.gelu(y)
    zz = zz * jax.nn.sigmoid(jnp.dot(zz.astype(bf16), wglu_ref[...], preferred_element_type=f32))
    zb = (zz * jax.nn.silu(proj(C_GB))).astype(bf16)

    xcs[3 * nb:3 * nb + m, :] = proj(C_XC)
    _lru_block(xcs, la, lb, lst, wlru_ref, vec_ref, nb=nb, tt=tt)
    zc = (lb[...] * jax.nn.silu(proj(C_GC))).astype(bf16)
    return za, zb, zc


def _merge_gate(xb, win_ref, k):
    c0 = C_GM[0] + D_MODEL * k
    return jax.nn.sigmoid(jnp.dot(xb, win_ref[:, c0:c0 + D_MODEL], preferred_element_type=f32))


def _out_core(x, zs, gates, wbr_refs, wout_ref, ln_g, ln_b, alpha):
    merged = None
    for k in range(3):
        g = gates(k)
        yk = jnp.dot(zs[k], wbr_refs[k][...], preferred_element_type=f32)
        merged = g * yk if merged is None else merged + g * yk
    out = jnp.dot(merged.astype(bf16), wout_ref[...], preferred_element_type=f32)
    h = alpha * x + out
    mu = jnp.mean(h, axis=-1, keepdims=True)
    d = h - mu
    var = jnp.mean(d * d, axis=-1, keepdims=True)
    return d * lax.rsqrt(var + LN_EPS) * ln_g + ln_b


def _prompt_mixer_kernel(*refs, layer, nb, tt, nts, natural_in):
    (x_ref, win_ref, tab_ref, sinks_ref, bbh_ref, cch_ref, ab_ref, wglu_ref, wlru_ref, vec_ref), refs = refs[:10], refs[10:]
    (z_ref, gm_ref), refs = refs[:2], refs[2:]
    if natural_in:
        xtb_ref, refs = refs[0], refs[1:]
    (ko_ref, vo_ref, sre_ref, sim_ref, lruo_ref, convo_ref,
     qs, kvs, atts, xs, hst, xcs, la, lb, lst, tabm, sc_scr, p_scr, gact), refs = refs[:19], refs[19:]
    ts = pl.program_id(0)
    m = nb * tt
    w = WINDOW * nb
    last = ts == nts - 1

    @pl.when(ts == 0)
    def _zero_state():
        kvs[:, 0:w, :] = jnp.zeros((2, w, LANES), f32)
        hst[...] = jnp.zeros(hst.shape, f32)
        lst[...] = jnp.zeros(lst.shape, f32)
        xcs[0:3 * nb, :] = jnp.zeros((3 * nb, BRANCH), f32)

    if natural_in:
        (xsl,) = refs
        for b in range(nb):
            for k in range(N_SLABS):
                xsl.at[k][pl.ds(b, tt, stride=nb), :] = x_ref[b, :, LANES * k:LANES * (k + 1)]
        x = jnp.concatenate([xsl[k] for k in range(N_SLABS)], axis=1)
        xtb_ref[...] = x
    else:
        x = x_ref[...]
    xb = x.astype(bf16)

    def proj(cols):
        return jnp.dot(xb, win_ref[:, cols[0]:cols[1]], preferred_element_type=f32)

    q = proj(C_Q)
    for j in range(4):
        qs[j] = q[:, LANES * j:LANES * (j + 1)] * (HEAD_DIM ** -0.5)
    kv = proj((C_K[0], C_V[1]))
    kvs[0, w:w + m, :] = kv[:, :LANES]
    kvs[1, w:w + m, :] = kv[:, LANES:]
    kj = lax.broadcasted_iota(jnp.int32, (2, 4 * tt, WINDOW + tt), 2)
    tabm[...] = jnp.where(kj + ts * tt >= WINDOW, tab_ref[...], NEG)
    _scores_prompt(qs, kvs, sc_scr, nb=nb, tt=tt)
    ub = proj(C_UB)
    ub16 = ub.astype(bf16)
    _s5_drive(ub16, xs, bbh_ref, 0)

    def gate_piece(j):
        c0 = C_GM[0] + BRANCH * j
        g = jax.nn.sigmoid(jnp.dot(xb, win_ref[:, c0:c0 + BRANCH], preferred_element_type=f32))
        gm_ref[:, BRANCH * j:BRANCH * (j + 1)] = g.astype(bf16)

    gate_pieces = [functools.partial(gate_piece, j) for j in range(3 * D_MODEL // BRANCH)]
    fill = _Filler(gate_pieces)
    _softmax_prompt(layer, tabm, sinks_ref, sc_scr, p_scr, fill, nb=nb, tt=tt)
    fill.drain()

    def branch_a():
        att = jnp.concatenate([atts[j] for j in range(4)], axis=1)
        z_ref[:, 0:BRANCH] = (att * jax.nn.silu(proj(C_GA))).astype(bf16)

    def conv_in():
        xcs[3 * nb:3 * nb + m, :] = proj(C_XC)

    fill = _Filler(_pv_thunks(kvs, atts, p_scr, nb=nb, tt=tt) + [branch_a, conv_in])
    _s5_scan(xs, hst, ab_ref, 0, nb=nb, tt=tt, filler=fill)
    fill.drain()
    kvs[:, 0:w, :] = kvs[:, m:m + w, :]

    y0 = _s5_readout(xs, cch_ref, 0)
    _lru_prep(xcs, la, lb, wlru_ref, vec_ref, nb=nb, tt=tt)
    _s5_drive(ub16, xs, bbh_ref, 1)

    def gate_act(k, cols):
        gact[k] = jax.nn.silu(proj(cols))

    fill = _Filler([functools.partial(gate_act, 0, C_GB), functools.partial(gate_act, 1, C_GC)])
    _s5_scan(xs, hst, ab_ref, 1, nb=nb, tt=tt, filler=fill)
    fill.drain()

    def branch_b():
        y = jnp.concatenate([y0, _s5_readout(xs, cch_ref, 1)], axis=1) + vec_ref[V_D:V_D + 1, :] * ub
        zz = jax.nn.gelu(y)
        zz = zz * jax.nn.sigmoid(jnp.dot(zz.astype(bf16), wglu_ref[...], preferred_element_type=f32))
        z_ref[:, BRANCH:2 * BRANCH] = (zz * gact[0]).astype(bf16)

    fill = _Filler([branch_b])
    _lru_scan(la, lb, lst, nb=nb, tt=tt, filler=fill)
    fill.drain()
    z_ref[:, 2 * BRANCH:3 * BRANCH] = (lb[...] * gact[1]).astype(bf16)

    @pl.when(last)
    def _store_state():
        sre_ref[...] = hst[0]
        sim_ref[...] = hst[1]
        lruo_ref[...] = lst[...]
        for t in range(3):
            convo_ref[:, BRANCH * t:BRANCH * (t + 1)] = xcs[m + t * nb:m + (t + 1) * nb, :]
        for b in range(nb):
            ko_ref[b] = kvs.at[0][pl.ds(b, WINDOW, stride=nb), :]
            vo_ref[b] = kvs.at[1][pl.ds(b, WINDOW, stride=nb), :]

    xcs[0:3 * nb, :] = xcs[m:m + 3 * nb, :]


def _prompt_out_kernel(*refs, layer, alpha, nb, natural_out):
    if natural_out:
        x_ref, z_ref, gm_ref, wa_ref, wb_ref, wc_ref, wout_ref, lng_ref, lnb_ref, y_ref, ysl = refs
    else:
        x_ref, z_ref, gm_ref, wa_ref, wb_ref, wc_ref, wout_ref, lng_ref, lnb_ref, y_ref = refs
    zs = [z_ref[:, BRANCH * k:BRANCH * (k + 1)] for k in range(3)]
    gates = lambda k: gm_ref[:, D_MODEL * k:D_MODEL * (k + 1)].astype(f32)
    y = _out_core(x_ref[...], zs, gates, (wa_ref, wb_ref, wc_ref), wout_ref,
                  lng_ref[layer:layer + 1, :], lnb_ref[layer:layer + 1, :], alpha)
    if natural_out:
        tt = y.shape[0] // nb
        for k in range(N_SLABS):
            ysl[k] = y[:, LANES * k:LANES * (k + 1)]
        for b in range(nb):
            for k in range(N_SLABS):
                y_ref[b, :, LANES * k:LANES * (k + 1)] = ysl.at[k][pl.ds(b, tt, stride=nb), :]
    else:
        y_ref[...] = y


def _sample_kernel(x_ref, win_ref, tab1_ref, tab2_ref, sinks_ref, bbh_ref, cch_ref, ab_ref, wglu_ref, wlru_ref,
                   vec_ref, wa_ref, wb_ref, wc_ref, wout_ref, lng_ref, lnb_ref,
                   kc_ref, vc_ref, sre0_ref, sim0_ref, lru0_ref, conv0_ref,
                   y_ref, ko_ref, vo_ref, sre_ref, sim_ref, lruo_ref, convo_ref,
                   xcar, qs, kvn, atts, xs, hst, xcs, la, lb, lst, sc1, sc2, p1, p2, *, alpha, nb, tt):
    layer = pl.program_id(0)
    g = pl.program_id(1)
    m = nb * tt

    @pl.when(layer == 0)
    def _first_layer_input():
        xcar[g] = x_ref[...]

    x = xcar[g]
    xb = x.astype(bf16)
    hst[0] = sre0_ref[...]
    hst[1] = sim0_ref[...]
    lst[...] = lru0_ref[...]
    for t in range(3):
        xcs[t * nb:(t + 1) * nb, :] = conv0_ref[:, BRANCH * t:BRANCH * (t + 1)]

    def proj(cols):
        return jnp.dot(xb, win_ref[:, cols[0]:cols[1]], preferred_element_type=f32)

    q = proj(C_Q)
    for j in range(4):
        qs[j] = q[:, LANES * j:LANES * (j + 1)] * (HEAD_DIM ** -0.5)
    kv = proj((C_K[0], C_V[1]))
    kvn[0] = kv[:, :LANES]
    kvn[1] = kv[:, LANES:]
    _attend_sample(layer, qs, kvn, atts, kc_ref, vc_ref, tab1_ref, tab2_ref, sinks_ref, ko_ref, vo_ref,
                   sc1, sc2, p1, p2, nb=nb, tt=tt)
    zs = _mixer_tail(xb, win_ref, atts, xs, hst, xcs, la, lb, lst, bbh_ref, cch_ref, ab_ref, wglu_ref,
                     wlru_ref, vec_ref, nb=nb, tt=tt)
    sre_ref[...] = hst[0]
    sim_ref[...] = hst[1]
    lruo_ref[...] = lst[...]
    for t in range(3):
        convo_ref[:, BRANCH * t:BRANCH * (t + 1)] = xcs[m + t * nb:m + (t + 1) * nb, :]

    y = _out_core(x, zs, functools.partial(_merge_gate, xb, win_ref), (wa_ref, wb_ref, wc_ref), wout_ref,
                  lng_ref[pl.ds(layer, 1), :], lnb_ref[pl.ds(layer, 1), :], alpha)
    xcar[g] = y
    y_ref[...] = y


def _layer_spec(shape, layer=None):
    if layer is None:
        idx = lambda l, *_: (l,) + (0,) * len(shape)
    else:
        idx = lambda *_: (layer,) + (0,) * len(shape)
    return pl.BlockSpec((None,) + tuple(shape), idx, pipeline_mode=pl.Buffered(1))


def _const_spec(shape):
    return pl.BlockSpec(tuple(shape), lambda *_: (0,) * len(shape), pipeline_mode=pl.Buffered(1))


def _mixer_weight_specs(layer):
    return [
        _layer_spec((2, 256, SSM_WIDTH), layer),
        _layer_spec((2, SSM_WIDTH, 256), layer),
        _layer_spec((4, HALF), layer),
        _layer_spec((BRANCH, BRANCH), layer),
        _layer_spec((2, 2, 256, 256), layer),
        _layer_spec((16, BRANCH), layer),
    ]


def _mixer_scratch(nb, tt, kv_rows):
    m = nb * tt
    return [
        pltpu.VMEM((4, m, LANES), f32),
        pltpu.VMEM((2, kv_rows, LANES), f32),
        pltpu.VMEM((4, m, LANES), f32),
        pltpu.VMEM((m, SSM_WIDTH), f32),
        pltpu.VMEM((2, nb, SSM_WIDTH), f32),
        pltpu.VMEM((m + 3 * nb, BRANCH), f32),
        pltpu.VMEM((m, BRANCH), f32),
        pltpu.VMEM((m, BRANCH), f32),
        pltpu.VMEM((nb, BRANCH), f32),
    ]


def _prompt_mixer_call(x, wp, tab, layer, *, nb, tt, natural_in):
    nts = (x.shape[1] if natural_in else x.shape[0] // nb) // tt
    m, w, s_len = nb * tt, WINDOW * nb, WINDOW + tt
    kern = functools.partial(_prompt_mixer_kernel, layer=layer, nb=nb, tt=tt, nts=nts, natural_in=natural_in)
    x_spec = (pl.BlockSpec((nb, tt, D_MODEL), lambda t: (0, t, 0)) if natural_in
              else pl.BlockSpec((m, D_MODEL), lambda t: (t, 0)))
    in_specs = [x_spec,
                _layer_spec((D_MODEL, IN_COLS), layer),
                _const_spec(tab.shape), pl.BlockSpec(memory_space=pltpu.SMEM), *_mixer_weight_specs(layer)]
    whole = lambda shape: pl.BlockSpec(shape, lambda t: (0,) * len(shape))
    out_specs = [pl.BlockSpec((m, 3 * BRANCH), lambda t: (t, 0)), pl.BlockSpec((m, 3 * D_MODEL), lambda t: (t, 0))]
    out_shape = [jax.ShapeDtypeStruct((nts * m, 3 * BRANCH), bf16),
                 jax.ShapeDtypeStruct((nts * m, 3 * D_MODEL), bf16)]
    if natural_in:
        out_specs.append(pl.BlockSpec((m, D_MODEL), lambda t: (t, 0)))
        out_shape.append(jax.ShapeDtypeStruct((nts * m, D_MODEL), f32))
    state_shapes = [(nb, WINDOW, LANES), (nb, WINDOW, LANES), (nb, SSM_WIDTH), (nb, SSM_WIDTH),
                    (nb, BRANCH), (nb, 3 * BRANCH)]
    out_specs += [whole(s) for s in state_shapes]
    out_shape += [jax.ShapeDtypeStruct(s, f32) for s in state_shapes]
    scratch = _mixer_scratch(nb, tt, w + m) + [
        pltpu.VMEM((2, 4 * tt, s_len), f32),
        pltpu.VMEM((nb, 2, 4 * tt, s_len), f32),
        pltpu.VMEM((nb, 2, 4 * tt, s_len), bf16),
        pltpu.VMEM((2, m, BRANCH), f32),
    ]
    if natural_in:
        scratch.append(pltpu.VMEM((N_SLABS, m, LANES), f32))
    return pl.pallas_call(
        kern, grid=(nts,), in_specs=in_specs, out_specs=out_specs, out_shape=out_shape,
        scratch_shapes=scratch, name=f"mixer_prompt_{layer}",
        compiler_params=pltpu.CompilerParams(dimension_semantics=("arbitrary",), vmem_limit_bytes=VMEM_LIMIT),
    )(x, wp["win"], tab, wp["sinks"], wp["bbh"], wp["cch"], wp["ab"], wp["wglu"], wp["wlru"], wp["vec"])


def _prompt_out_call(x2d, z2d, gm2d, wp, layer, *, nb, mb, alpha, natural_out):
    rows = x2d.shape[0]
    kern = functools.partial(_prompt_out_kernel, layer=layer, alpha=alpha, nb=nb, natural_out=natural_out)
    if natural_out:
        out_spec = pl.BlockSpec((nb, mb // nb, D_MODEL), lambda i: (0, i, 0))
        out_shape = jax.ShapeDtypeStruct((nb, rows // nb, D_MODEL), f32)
        scratch = [pltpu.VMEM((N_SLABS, mb, LANES), f32)]
    else:
        out_spec = pl.BlockSpec((mb, D_MODEL), lambda i: (i, 0))
        out_shape = jax.ShapeDtypeStruct((rows, D_MODEL), f32)
        scratch = []
    return pl.pallas_call(
        kern, grid=(rows // mb,),
        in_specs=[
            pl.BlockSpec((mb, D_MODEL), lambda i: (i, 0)),
            pl.BlockSpec((mb, 3 * BRANCH), lambda i: (i, 0)),
            pl.BlockSpec((mb, 3 * D_MODEL), lambda i: (i, 0)),
            _layer_spec((BRANCH, D_MODEL), layer), _layer_spec((BRANCH, D_MODEL), layer),
            _layer_spec((BRANCH, D_MODEL), layer),
            _layer_spec((D_MODEL, D_MODEL), layer),
            _const_spec(wp["ln_g"].shape), _const_spec(wp["ln_b"].shape),
        ],
        out_specs=out_spec, out_shape=out_shape, scratch_shapes=scratch, name=f"out_prompt_{layer}",
        compiler_params=pltpu.CompilerParams(dimension_semantics=("arbitrary",), vmem_limit_bytes=VMEM_LIMIT),
    )(x2d, z2d, gm2d, wp["wbr_a"], wp["wbr_b"], wp["wbr_c"], wp["wout"], wp["ln_g"], wp["ln_b"])


def _sample_call(x2d, wp, tabs, state, *, depth, nbg, nb, tt, alpha):
    m = nb * tt
    kern = functools.partial(_sample_kernel, alpha=alpha, nb=nb, tt=tt)
    row_spec = pl.BlockSpec((m, D_MODEL), lambda l, g: (g, 0))
    st = lambda *shape: pl.BlockSpec((nb,) + shape, lambda l, g: (l * nbg + g,) + (0,) * len(shape))
    state_specs = [st(WINDOW, LANES), st(WINDOW, LANES), st(SSM_WIDTH), st(SSM_WIDTH), st(BRANCH), st(3 * BRANCH)]
    state_shapes = [(WINDOW, LANES), (WINDOW, LANES), (SSM_WIDTH,), (SSM_WIDTH,), (BRANCH,), (3 * BRANCH,)]
    in_specs = [row_spec, _layer_spec((D_MODEL, IN_COLS)), _const_spec(tabs[0].shape), _const_spec(tabs[1].shape),
                pl.BlockSpec(memory_space=pltpu.SMEM), *_mixer_weight_specs(None),
                _layer_spec((BRANCH, D_MODEL)), _layer_spec((BRANCH, D_MODEL)), _layer_spec((BRANCH, D_MODEL)),
                _layer_spec((D_MODEL, D_MODEL)), _const_spec(wp["ln_g"].shape), _const_spec(wp["ln_b"].shape),
                *state_specs]
    out_specs = [pl.BlockSpec((m, D_MODEL), lambda l, g: (l * nbg + g, 0)), *state_specs]
    out_shape = [jax.ShapeDtypeStruct((depth * nbg * m, D_MODEL), f32)] + [
        jax.ShapeDtypeStruct((depth * nbg * nb,) + s, f32) for s in state_shapes]
    nsg, hr = nb // SUBLANES, 4 * tt * SUBLANES
    scratch = [pltpu.VMEM((nbg, m, D_MODEL), f32)] + _mixer_scratch(nb, tt, m) + [
        pltpu.VMEM((nsg, 2, hr, SUBLANES * WINDOW), f32),
        pltpu.VMEM((nsg, 2, hr, SUBLANES * tt), f32),
        pltpu.VMEM((nsg, 2, hr, SUBLANES * WINDOW), bf16),
        pltpu.VMEM((nsg, 2, hr, SUBLANES * tt), bf16),
    ]
    return pl.pallas_call(
        kern, grid=(depth, nbg), in_specs=in_specs, out_specs=out_specs, out_shape=out_shape,
        scratch_shapes=scratch, name="sample_layers",
        compiler_params=pltpu.CompilerParams(
            dimension_semantics=("arbitrary", "arbitrary"), vmem_limit_bytes=VMEM_LIMIT),
    )(x2d, wp["win"], tabs[0], tabs[1], wp["sinks"], wp["bbh"], wp["cch"], wp["ab"], wp["wglu"], wp["wlru"],
      wp["vec"], wp["wbr_a"], wp["wbr_b"], wp["wbr_c"], wp["wout"], wp["ln_g"], wp["ln_b"], *state)


def _t5_bucket(dist):
    max_exact = N_BUCKETS // 2
    d = jnp.maximum(dist, 0)
    large = max_exact + (jnp.log(jnp.maximum(d, 1).astype(f32) / max_exact)
                         / math.log(MAX_DISTANCE / max_exact) * (N_BUCKETS - max_exact)).astype(jnp.int32)
    large = jnp.minimum(large, N_BUCKETS - 1)
    return jnp.where(d < max_exact, d, large)


def _band_bias(rel_bias, tt):
    s_len = WINDOW + tt
    by_dist = rel_bias.astype(f32)[_t5_bucket(jnp.arange(WINDOW))].T
    row0 = jnp.concatenate([jnp.full((N_Q_HEADS, 1), NEG, f32), by_dist[:, ::-1],
                            jnp.full((N_Q_HEADS, tt), NEG, f32)], axis=1)
    flat = jnp.tile(row0, (1, tt))[:, :tt * s_len]
    return flat.reshape(N_Q_HEADS, tt, s_len)


def _prompt_table(rel_bias, tt):
    return _band_bias(rel_bias, tt).reshape(2, 4 * tt, WINDOW + tt)


def _sample_tables(rel_bias, tt):
    bs = _band_bias(rel_bias, tt).reshape(2, 4, tt, WINDOW + tt)
    same = jnp.eye(SUBLANES, dtype=bool)
    t1 = jnp.where(same[None, None, None, :, :, None], bs[:, :, :, None, None, :WINDOW], NEG)
    t2 = jnp.where(same[None, None, None, :, None, :], bs[:, :, :, None, WINDOW:, None], NEG)
    rows = 4 * tt * SUBLANES
    return t1.reshape(2, rows, SUBLANES * WINDOW), t2.reshape(2, rows, tt * SUBLANES)


def _weights(w_in, sinks, w_branch_a, lam_re, lam_im, log_step, b_re, b_im, c_re, c_im, ssm_d, w_glu,
             w_branch_b, conv_w, conv_b, w_a, b_a, w_x, b_x, lam, w_branch_c, w_out, ln_g, ln_b):
    depth = w_in.shape[0]
    hg = SSM_GROUPS // 2
    lr = jnp.minimum(lam_re.astype(f32), -1e-4)
    li = lam_im.astype(f32)
    step = jnp.exp(log_step.astype(f32))[..., None]
    mag = jnp.exp(lr * step)
    ab_re = mag * jnp.cos(li * step)
    ab_im = mag * jnp.sin(li * step)
    den = lr * lr + li * li
    f_re = ((ab_re - 1.0) * lr + ab_im * li) / den
    f_im = (ab_im * lr - (ab_re - 1.0) * li) / den
    bre, bim = b_re.astype(f32), b_im.astype(f32)
    bb_re = f_re[..., None] * bre - f_im[..., None] * bim
    bb_im = f_re[..., None] * bim + f_im[..., None] * bre

    def diag_tiles(blocks, n, axis):
        c = blocks.shape[-1]
        per = LANES // c
        tiled = jnp.tile(blocks, (1,) * (blocks.ndim - 1) + (per,))
        blk_id = jnp.arange(n).reshape((n,) + (1,) * (blocks.ndim - 1 - axis))
        lane_blk = jnp.arange(LANES) // c
        return jnp.concatenate([jnp.where(blk_id == per * j + lane_blk, tiled, 0.0) for j in range(n // per)], axis=-1)

    drv = jnp.stack([bb_re, bb_im], axis=1).reshape(depth, 2, 2, hg, SSM_STATE, SSM_GROUP)
    bbh = diag_tiles(drv.transpose(0, 2, 3, 5, 1, 4), hg, 2)
    bbh = bbh.astype(bf16).reshape(depth, 2, hg * SSM_GROUP, SSM_WIDTH)
    rdo = jnp.stack([c_re.astype(f32), -c_im.astype(f32)], axis=1).reshape(depth, 2, 2, hg, SSM_GROUP, SSM_STATE)
    cch = diag_tiles(rdo.transpose(0, 2, 1, 3, 5, 4), hg, 3)
    cch = cch.astype(bf16).reshape(depth, 2, SSM_WIDTH, hg * SSM_GROUP)
    ab = jnp.stack([ab_re.reshape(depth, 2, HALF), ab_im.reshape(depth, 2, HALF)], axis=2).reshape(depth, 4, HALF)
    nblk = LRU_BLOCKS // 2
    blk = jnp.stack([w_a.astype(f32), w_x.astype(f32)], axis=1).reshape(depth, 2, 2, nblk, 64, 64)
    wlru = diag_tiles(blk, nblk, 3).reshape(depth, 2, 2, 256, 256)
    row = lambda v: v.astype(f32)[:, None, :]
    vec = jnp.concatenate([conv_w.astype(f32), row(conv_b), row(b_a), row(b_x), row(lam), row(ssm_d),
                           jnp.zeros((depth, 16 - V_D - 1, BRANCH), f32)], axis=1)
    return {
        "win": w_in.astype(bf16),
        "sinks": sinks.astype(f32),
        "bbh": bbh, "cch": cch, "ab": ab,
        "wglu": w_glu.astype(bf16),
        "wlru": wlru.astype(bf16),
        "vec": vec,
        "wbr_a": w_branch_a.astype(bf16), "wbr_b": w_branch_b.astype(bf16), "wbr_c": w_branch_c.astype(bf16),
        "wout": w_out.astype(bf16),
        "ln_g": ln_g.astype(f32), "ln_b": ln_b.astype(f32),
    }


def _group_rows(x, nbg, nb):
    b, t, c = x.shape
    return x.reshape(nbg, nb, t, c).transpose(0, 2, 1, 3).reshape(nbg * t * nb, c)


def _ungroup_rows(x2d, nbg, nb, t):
    c = x2d.shape[-1]
    return x2d.reshape(nbg, t, nb, c).transpose(0, 2, 1, 3).reshape(nbg * nb, t, c)


def _states_out(lead, ko, vo, sre, sim, lru, conv):
    return (ko.reshape(*lead, WINDOW, 2, HEAD_DIM), vo.reshape(*lead, WINDOW, 2, HEAD_DIM),
            sre.reshape(*lead, SSM_GROUPS, SSM_STATE), sim.reshape(*lead, SSM_GROUPS, SSM_STATE),
            lru.reshape(*lead, BRANCH), conv.reshape(*lead, CONV_WIDTH - 1, BRANCH))


P_TT = 64
OUT_ROWS = 512
S_NBG = 4


def kernel(x_prompt, x_sample, cache_k, cache_v, state_ssm_re, state_ssm_im, state_lru, state_conv, rel_bias, w_in, sinks, w_branch_a, ssm_lambda_re, ssm_lambda_im, ssm_log_step, ssm_b_re, ssm_b_im, ssm_c_re, ssm_c_im, ssm_d, ssm_w_glu, w_branch_b, conv_w, conv_b, lru_w_a, lru_b_a, lru_w_x, lru_b_x, lru_lambda, w_branch_c, w_out, ln_g, ln_b):
    depth = w_in.shape[0]
    alpha = (2 * depth) ** 0.25
    pb, pt, _ = x_prompt.shape
    sb, st, _ = x_sample.shape
    s_nb = sb // S_NBG

    wp = _weights(w_in, sinks, w_branch_a, ssm_lambda_re, ssm_lambda_im, ssm_log_step, ssm_b_re, ssm_b_im,
                  ssm_c_re, ssm_c_im, ssm_d, ssm_w_glu, w_branch_b, conv_w, conv_b, lru_w_a, lru_b_a, lru_w_x,
                  lru_b_x, lru_lambda, w_branch_c, w_out, ln_g, ln_b)
    p_tab = _prompt_table(rel_bias, P_TT)
    s_tabs = _sample_tables(rel_bias, st)

    st_p = []
    x = x_prompt
    for l in range(depth):
        outs = _prompt_mixer_call(x, wp, p_tab, l, nb=pb, tt=P_TT, natural_in=(l == 0))
        if l == 0:
            z, gm, x, *sp = outs
        else:
            z, gm, *sp = outs
        x = _prompt_out_call(x, z, gm, wp, l, nb=pb, mb=OUT_ROWS, alpha=alpha, natural_out=(l == depth - 1))
        st_p.append(_states_out((pb,), *sp))
    y_prompt = x
    stacked_p = [jnp.stack([s[k] for s in st_p]) for k in range(6)]

    n = depth * sb
    s_state = (cache_k.astype(f32).reshape(n, WINDOW, LANES), cache_v.astype(f32).reshape(n, WINDOW, LANES),
               state_ssm_re.astype(f32).reshape(n, SSM_WIDTH), state_ssm_im.astype(f32).reshape(n, SSM_WIDTH),
               state_lru.astype(f32).reshape(n, BRANCH), state_conv.astype(f32).reshape(n, 3 * BRANCH))
    ys, *ss = _sample_call(_group_rows(x_sample, S_NBG, s_nb), wp, s_tabs, s_state, depth=depth, nbg=S_NBG,
                           nb=s_nb, tt=st, alpha=alpha)
    y_sample = _ungroup_rows(ys[(depth - 1) * sb * st:], S_NBG, s_nb, st)
    stacked_s = _states_out((depth, sb), *ss)
    return (y_prompt, y_sample, *stacked_p, *stacked_s)
```

```python
import functools
import math

import jax
import jax.numpy as jnp
from jax import lax
from jax.experimental import pallas as pl
from jax.experimental.pallas import tpu as pltpu

f32 = jnp.float32
bf16 = jnp.bfloat16

D_MODEL = 1024
BRANCH = 512
HEAD_DIM = 64
N_Q_HEADS = 8
WINDOW = 128
N_BUCKETS = 32
MAX_DISTANCE = 128
SSM_GROUPS = 32
SSM_GROUP = 16
SSM_STATE = 64
LRU_BLOCKS = 8
CONV_WIDTH = 4
LRU_C = 8.0
LN_EPS = 1e-5
NEG = -1e30

LANES = 128
SUBLANES = 8
VMEM_LIMIT = 60 * 1024 * 1024

C_Q, C_K, C_V, C_GA, C_UB, C_GB, C_XC, C_GC, C_GM = (
    (0, 512), (512, 640), (640, 768), (768, 1280), (1280, 1792), (1792, 2304),
    (2304, 2816), (2816, 3328), (3328, 6400))
IN_COLS = C_GM[1]
HALF = SSM_GROUPS // 2 * SSM_STATE
SSM_WIDTH = 2 * HALF
N_SLABS = D_MODEL // LANES

V_CONVW, V_CONVB, V_BA, V_BX, V_LAM, V_D = 0, 4, 5, 6, 7, 8


def _split_heads(qa, lo, kvh):
    q_lo = jnp.where(lo, qa, 0.0)
    q_hi = jnp.where(lo, 0.0, qa)
    if kvh == 0:
        return [q_lo, pltpu.roll(q_hi, HEAD_DIM, axis=1)]
    return [pltpu.roll(q_lo, HEAD_DIM, axis=1), q_hi]


def _merge_heads(o0, o1, lo, kvh):
    if kvh == 0:
        return jnp.where(lo, o0, pltpu.roll(o1, HEAD_DIM, axis=1))
    return jnp.where(lo, pltpu.roll(o0, HEAD_DIM, axis=1), o1)


class _Filler:
    def __init__(self, thunks):
        self._thunks = list(thunks)
        self._done = 0

    def tick(self, i, n):
        due = -((i + 1) * len(self._thunks) // -n)
        while self._done < due:
            self._thunks[self._done]()
            self._done += 1

    def drain(self):
        self.tick(0, 1)


def _scores_prompt(qs, kvs, sc_scr, *, nb, tt):
    s_len = WINDOW + tt
    lo = lax.broadcasted_iota(jnp.int32, (tt, LANES), 1) < HEAD_DIM
    for b in range(nb):
        kk16 = kvs.at[0][pl.ds(b, s_len, stride=nb), :].astype(bf16)
        for kvh in range(2):
            parts = []
            for i in range(2):
                parts += _split_heads(qs.at[2 * kvh + i][pl.ds(b, tt, stride=nb), :], lo, kvh)
            q4 = jnp.concatenate(parts, axis=0).astype(bf16)
            sc_scr[b, kvh] = lax.dot_general(q4, kk16, (((1,), (1,)), ((), ())), preferred_element_type=f32)


def _softmax_prompt(layer, tabm, sinks_ref, sc_scr, p_scr, filler, *, nb, tt):
    n, it = nb * 2 * 4, 0
    for b in range(nb):
        for kvh in range(2):
            for g in range(4):
                rs = slice(g * tt, (g + 1) * tt)
                t = tabm[kvh, rs, :]
                sc = jnp.where(t > 0.5 * NEG, sc_scr[b, kvh, rs, :] + t, NEG)
                sink = sinks_ref[layer, 4 * kvh + g]
                mx = jnp.maximum(jnp.max(sc, axis=-1, keepdims=True), sink)
                p = jnp.exp(sc - mx)
                den = jnp.sum(p, axis=-1, keepdims=True) + jnp.exp(sink - mx)
                p_scr[b, kvh, rs, :] = (p / den).astype(bf16)
                filler.tick(it, n)
                it += 1


def _pv_thunks(kvs, atts, p_scr, *, nb, tt):
    s_len = WINDOW + tt
    lo = lax.broadcasted_iota(jnp.int32, (tt, LANES), 1) < HEAD_DIM

    def piece(b):
        vv16 = kvs.at[1][pl.ds(b, s_len, stride=nb), :].astype(bf16)
        for kvh in range(2):
            o = jnp.dot(p_scr[b, kvh], vv16, preferred_element_type=f32)
            for i in range(2):
                slab = _merge_heads(o[2 * i * tt:(2 * i + 1) * tt], o[(2 * i + 1) * tt:(2 * i + 2) * tt], lo, kvh)
                atts.at[2 * kvh + i][pl.ds(b, tt, stride=nb), :] = slab

    return [functools.partial(piece, b) for b in range(nb)]


def _attend_sample(layer, qs, kvn, atts, kc_ref, vc_ref, tab1_ref, tab2_ref, sinks_ref, ko_ref, vo_ref,
                   sc1, sc2, p1, p2, *, nb, tt):
    r = SUBLANES * tt
    lo = lax.broadcasted_iota(jnp.int32, (r, LANES), 1) < HEAD_DIM
    nsg = nb // SUBLANES

    def rows(ref, b0):
        return jnp.concatenate([ref[t * nb + b0:t * nb + b0 + SUBLANES, :] for t in range(tt)], axis=0)

    nt = (((1,), (1,)), ((), ()))
    for sg in range(nsg):
        b0 = SUBLANES * sg
        kc = kc_ref[b0:b0 + SUBLANES].reshape(SUBLANES * WINDOW, LANES).astype(bf16)
        kn = rows(kvn.at[0], b0).astype(bf16)
        for kvh in range(2):
            parts = []
            for i in range(2):
                parts += _split_heads(rows(qs.at[2 * kvh + i], b0), lo, kvh)
            q4 = jnp.concatenate(parts, axis=0).astype(bf16)
            sc1[sg, kvh] = lax.dot_general(q4, kc, nt, preferred_element_type=f32)
            sc2[sg, kvh] = lax.dot_general(q4, kn, nt, preferred_element_type=f32)

    for sg in range(nsg):
        for kvh in range(2):
            for g in range(4):
                rs = slice(g * r, (g + 1) * r)
                t1, t2 = tab1_ref[kvh, rs, :], tab2_ref[kvh, rs, :]
                s1 = jnp.where(t1 > 0.5 * NEG, sc1[sg, kvh, rs, :] + t1, NEG)
                s2 = jnp.where(t2 > 0.5 * NEG, sc2[sg, kvh, rs, :] + t2, NEG)
                sink = sinks_ref[layer, 4 * kvh + g]
                mx = jnp.maximum(jnp.maximum(jnp.max(s1, axis=-1, keepdims=True),
                                             jnp.max(s2, axis=-1, keepdims=True)), sink)
                e1, e2 = jnp.exp(s1 - mx), jnp.exp(s2 - mx)
                den = (jnp.sum(e1, axis=-1, keepdims=True) + jnp.sum(e2, axis=-1, keepdims=True)
                       + jnp.exp(sink - mx))
                p1[sg, kvh, rs, :] = (e1 / den).astype(bf16)
                p2[sg, kvh, rs, :] = (e2 / den).astype(bf16)

    for sg in range(nsg):
        b0 = SUBLANES * sg
        vc = vc_ref[b0:b0 + SUBLANES].reshape(SUBLANES * WINDOW, LANES).astype(bf16)
        vn = rows(kvn.at[1], b0).astype(bf16)
        for kvh in range(2):
            o = (jnp.dot(p1[sg, kvh], vc, preferred_element_type=f32)
                 + jnp.dot(p2[sg, kvh], vn, preferred_element_type=f32))
            for i in range(2):
                slab = _merge_heads(o[2 * i * r:(2 * i + 1) * r], o[(2 * i + 1) * r:(2 * i + 2) * r], lo, kvh)
                for t in range(tt):
                    atts[2 * kvh + i, t * nb + b0:t * nb + b0 + SUBLANES, :] = slab[SUBLANES * t:SUBLANES * (t + 1)]
    for b in range(nb):
        ko_ref[b, 0:WINDOW - tt, :] = kc_ref[b, tt:WINDOW, :]
        vo_ref[b, 0:WINDOW - tt, :] = vc_ref[b, tt:WINDOW, :]
        ko_ref[b, WINDOW - tt:WINDOW, :] = kvn.at[0][pl.ds(b, tt, stride=nb), :]
        vo_ref[b, WINDOW - tt:WINDOW, :] = kvn.at[1][pl.ds(b, tt, stride=nb), :]


def _s5_block(ub, xs, hst, bbh_ref, cch_ref, ab_ref, *, nb, tt):
    ub16 = ub.astype(bf16)
    ys = []
    for hf in range(2):
        _s5_drive(ub16, xs, bbh_ref, hf)
        _s5_scan(xs, hst, ab_ref, hf, nb=nb, tt=tt)
        ys.append(_s5_readout(xs, cch_ref, hf))
    return jnp.concatenate(ys, axis=1)


def _s5_drive(ub16, xs, bbh_ref, hf):
    xs[...] = jnp.dot(ub16[:, 256 * hf:256 * (hf + 1)], bbh_ref[hf], preferred_element_type=f32)


def _s5_readout(xs, cch_ref, hf):
    return jnp.dot(xs[...].astype(bf16), cch_ref[hf], preferred_element_type=f32)


def _s5_scan(xs, hst, ab_ref, hf, *, nb, tt, filler=None):
    nck = 2
    cw = HALF // nck
    n, it = (nb // SUBLANES) * tt, 0
    coef = []
    for ck in range(nck):
        ar = jnp.broadcast_to(ab_ref[2 * hf:2 * hf + 1, cw * ck:cw * (ck + 1)], (SUBLANES, cw))
        ai = jnp.broadcast_to(ab_ref[2 * hf + 1:2 * hf + 2, cw * ck:cw * (ck + 1)], (SUBLANES, cw))
        coef.append((ar, ai))
    for bg in range(nb // SUBLANES):
        b0 = SUBLANES * bg
        rows = slice(b0, b0 + SUBLANES)
        st = [(hst[0, rows, HALF * hf + cw * ck:HALF * hf + cw * (ck + 1)],
               hst[1, rows, HALF * hf + cw * ck:HALF * hf + cw * (ck + 1)]) for ck in range(nck)]
        for t in range(tt):
            r0 = t * nb + b0
            for ck in range(nck):
                re0, im0 = cw * ck, HALF + cw * ck
                (ar, ai), (hr, hi) = coef[ck], st[ck]
                xr = xs[r0:r0 + SUBLANES, re0:re0 + cw]
                xi = xs[r0:r0 + SUBLANES, im0:im0 + cw]
                hr, hi = ar * hr - ai * hi + xr, ar * hi + ai * hr + xi
                xs[r0:r0 + SUBLANES, re0:re0 + cw] = hr
                xs[r0:r0 + SUBLANES, im0:im0 + cw] = hi
                st[ck] = (hr, hi)
            if filler is not None:
                filler.tick(it, n)
            it += 1
        for ck in range(nck):
            hst[0, rows, HALF * hf + cw * ck:HALF * hf + cw * (ck + 1)] = st[ck][0]
            hst[1, rows, HALF * hf + cw * ck:HALF * hf + cw * (ck + 1)] = st[ck][1]


def _lru_block(xcs, la, lb, lst, wlru_ref, vec_ref, *, nb, tt):
    _lru_prep(xcs, la, lb, wlru_ref, vec_ref, nb=nb, tt=tt)
    _lru_scan(la, lb, lst, nb=nb, tt=tt)


def _lru_prep(xcs, la, lb, wlru_ref, vec_ref, *, nb, tt):
    m = nb * tt
    conv = vec_ref[V_CONVB:V_CONVB + 1, :] + vec_ref[V_CONVW:V_CONVW + 1, :] * xcs[0:m, :]
    for j in range(1, CONV_WIDTH):
        conv = conv + vec_ref[V_CONVW + j:V_CONVW + j + 1, :] * xcs[j * nb:j * nb + m, :]
    c16 = conv.astype(bf16)

    def block_dot(g):
        return jnp.concatenate(
            [jnp.dot(c16[:, 256 * hf:256 * (hf + 1)], wlru_ref[g, hf], preferred_element_type=f32)
             for hf in range(2)], axis=1)

    r = jax.nn.sigmoid(block_dot(0) + vec_ref[V_BA:V_BA + 1, :])
    i = jax.nn.sigmoid(block_dot(1) + vec_ref[V_BX:V_BX + 1, :])
    log_a = LRU_C * r * jax.nn.log_sigmoid(vec_ref[V_LAM:V_LAM + 1, :])
    a = jnp.exp(log_a)
    la[...] = a
    g2 = -jnp.tanh(log_a) * (a * a + 1.0)
    lb[...] = conv * i * jnp.where(g2 > 0.0, g2 * lax.rsqrt(g2), 0.0)


def _lru_scan(la, lb, lst, *, nb, tt, filler=None):
    n, it = (nb // SUBLANES) * tt, 0
    for bg in range(nb // SUBLANES):
        b0 = SUBLANES * bg
        h = lst[b0:b0 + SUBLANES, :]
        for t in range(tt):
            r0 = t * nb + b0
            h = la[r0:r0 + SUBLANES, :] * h + lb[r0:r0 + SUBLANES, :]
            lb[r0:r0 + SUBLANES, :] = h
            if filler is not None:
                filler.tick(it, n)
            it += 1
        lst[b0:b0 + SUBLANES, :] = h


def _mixer_tail(xb, win_ref, atts, xs, hst, xcs, la, lb, lst, bbh_ref, cch_ref, ab_ref, wglu_ref, wlru_ref,
                vec_ref, *, nb, tt):
    m = nb * tt

    def proj(cols):
        return jnp.dot(xb, win_ref[:, cols[0]:cols[1]], preferred_element_type=f32)

    att = jnp.concatenate([atts[j] for j in range(4)], axis=1)
    za = (att * jax.nn.silu(proj(C_GA))).astype(bf16)

    ub = proj(C_UB)
    y = _s5_block(ub, xs, hst, bbh_ref, cch_ref, ab_ref, nb=nb, tt=tt) + vec_ref[V_D:V_D + 1, :] * ub
    zz = jax.nn.gelu(y)
    zz = zz * jax.nn.sigmoid(jnp.dot(zz.astype(bf16), wglu_ref[...], preferred_element_type=f32))
    zb = (zz * jax.nn.silu(proj(C_GB))).astype(bf16)

    xcs[3 * nb:3 * nb + m, :] = proj(C_XC)
    _lru_block(xcs, la, lb, lst, wlru_ref, vec_ref, nb=nb, tt=tt)
    zc = (lb[...] * jax.nn.silu(proj(C_GC))).astype(bf16)
    return za, zb, zc


def _merge_gate(xb, win_ref, k):
    c0 = C_GM[0] + D_MODEL * k
    return jax.nn.sigmoid(jnp.dot(xb, win_ref[:, c0:c0 + D_MODEL], preferred_element_type=f32))


def _out_core(x, zs, gates, wbr_refs, wout_ref, ln_g, ln_b, alpha):
    merged = None
    for k in range(3):
        g = gates(k)
        yk = jnp.dot(zs[k], wbr_refs[k][...], preferred_element_type=f32)
        merged = g * yk if merged is None else merged + g * yk
    out = jnp.dot(merged.astype(bf16), wout_ref[...], preferred_element_type=f32)
    h = alpha * x + out
    mu = jnp.mean(h, axis=-1, keepdims=True)
    d = h - mu
    var = jnp.mean(d * d, axis=-1, keepdims=True)
    return d * lax.rsqrt(var + LN_EPS) * ln_g + ln_b


def _prompt_mixer_kernel(*refs, layer, nb, tt, nts, natural_in):
    (x_ref, win_ref, tab_ref, sinks_ref, bbh_ref, cch_ref, ab_ref, wglu_ref, wlru_ref, vec_ref), refs = refs[:10], refs[10:]
    (z_ref, gm_ref), refs = refs[:2], refs[2:]
    if natural_in:
        xtb_ref, refs = refs[0], refs[1:]
    (ko_ref, vo_ref, sre_ref, sim_ref, lruo_ref, convo_ref,
     qs, kvs, atts, xs, hst, xcs, la, lb, lst, tabm, sc_scr, p_scr, gact), refs = refs[:19], refs[19:]
    ts = pl.program_id(0)
    m = nb * tt
    w = WINDOW * nb
    last = ts == nts - 1

    @pl.when(ts == 0)
    def _zero_state():
        kvs[:, 0:w, :] = jnp.zeros((2, w, LANES), f32)
        hst[...] = jnp.zeros(hst.shape, f32)
        lst[...] = jnp.zeros(lst.shape, f32)
        xcs[0:3 * nb, :] = jnp.zeros((3 * nb, BRANCH), f32)

    if natural_in:
        (xsl,) = refs
        for b in range(nb):
            for k in range(N_SLABS):
                xsl.at[k][pl.ds(b, tt, stride=nb), :] = x_ref[b, :, LANES * k:LANES * (k + 1)]
        x = jnp.concatenate([xsl[k] for k in range(N_SLABS)], axis=1)
        xtb_ref[...] = x
    else:
        x = x_ref[...]
    xb = x.astype(bf16)

    def proj(cols):
        return jnp.dot(xb, win_ref[:, cols[0]:cols[1]], preferred_element_type=f32)

    q = proj(C_Q)
    for j in range(4):
        qs[j] = q[:, LANES * j:LANES * (j + 1)] * (HEAD_DIM ** -0.5)
    kv = proj((C_K[0], C_V[1]))
    kvs[0, w:w + m, :] = kv[:, :LANES]
    kvs[1, w:w + m, :] = kv[:, LANES:]
    kj = lax.broadcasted_iota(jnp.int32, (2, 4 * tt, WINDOW + tt), 2)
    tabm[...] = jnp.where(kj + ts * tt >= WINDOW, tab_ref[...], NEG)
    _scores_prompt(qs, kvs, sc_scr, nb=nb, tt=tt)
    ub = proj(C_UB)
    ub16 = ub.astype(bf16)
    _s5_drive(ub16, xs, bbh_ref, 0)

    def gate_piece(j):
        c0 = C_GM[0] + BRANCH * j
        g = jax.nn.sigmoid(jnp.dot(xb, win_ref[:, c0:c0 + BRANCH], preferred_element_type=f32))
        gm_ref[:, BRANCH * j:BRANCH * (j + 1)] = g.astype(bf16)

    gate_pieces = [functools.partial(gate_piece, j) for j in range(3 * D_MODEL // BRANCH)]
    fill = _Filler(gate_pieces)
    _softmax_prompt(layer, tabm, sinks_ref, sc_scr, p_scr, fill, nb=nb, tt=tt)
    fill.drain()

    def branch_a():
        att = jnp.concatenate([atts[j] for j in range(4)], axis=1)
        z_ref[:, 0:BRANCH] = (att * jax.nn.silu(proj(C_GA))).astype(bf16)

    def conv_in():
        xcs[3 * nb:3 * nb + m, :] = proj(C_XC)

    fill = _Filler(_pv_thunks(kvs, atts, p_scr, nb=nb, tt=tt) + [branch_a, conv_in])
    _s5_scan(xs, hst, ab_ref, 0, nb=nb, tt=tt, filler=fill)
    fill.drain()
    kvs[:, 0:w, :] = kvs[:, m:m + w, :]

    y0 = _s5_readout(xs, cch_ref, 0)
    _lru_prep(xcs, la, lb, wlru_ref, vec_ref, nb=nb, tt=tt)
    _s5_drive(ub16, xs, bbh_ref, 1)

    def gate_act(k, cols):
        gact[k] = jax.nn.silu(proj(cols))

    fill = _Filler([functools.partial(gate_act, 0, C_GB), functools.partial(gate_act, 1, C_GC)])
    _s5_scan(xs, hst, ab_ref, 1, nb=nb, tt=tt, filler=fill)
    fill.drain()

    def branch_b():
        y = jnp.concatenate([y0, _s5_readout(xs, cch_ref, 1)], axis=1) + vec_ref[V_D:V_D + 1, :] * ub
        zz = jax.nn.gelu(y)
        zz = zz * jax.nn.sigmoid(jnp.dot(zz.astype(bf16), wglu_ref[...], preferred_element_type=f32))
        z_ref[:, BRANCH:2 * BRANCH] = (zz * gact[0]).astype(bf16)

    fill = _Filler([branch_b])
    _lru_scan(la, lb, lst, nb=nb, tt=tt, filler=fill)
    fill.drain()
    z_ref[:, 2 * BRANCH:3 * BRANCH] = (lb[...] * gact[1]).astype(bf16)

    @pl.when(last)
    def _store_state():
        sre_ref[...] = hst[0]
        sim_ref[...] = hst[1]
        lruo_ref[...] = lst[...]
        for t in range(3):
            convo_ref[:, BRANCH * t:BRANCH * (t + 1)] = xcs[m + t * nb:m + (t + 1) * nb, :]
        for b in range(nb):
            ko_ref[b] = kvs.at[0][pl.ds(b, WINDOW, stride=nb), :]
            vo_ref[b] = kvs.at[1][pl.ds(b, WINDOW, stride=nb), :]

    xcs[0:3 * nb, :] = xcs[m:m + 3 * nb, :]


def _prompt_out_kernel(*refs, layer, alpha, nb, natural_out):
    if natural_out:
        x_ref, z_ref, gm_ref, wa_ref, wb_ref, wc_ref, wout_ref, lng_ref, lnb_ref, y_ref, ysl = refs
    else:
        x_ref, z_ref, gm_ref, wa_ref, wb_ref, wc_ref, wout_ref, lng_ref, lnb_ref, y_ref = refs
    hm = x_ref.shape[0] // 2
    halves = []
    for h in range(2):
        rs = slice(h * hm, (h + 1) * hm)
        zs = [z_ref[rs, BRANCH * k:BRANCH * (k + 1)] for k in range(3)]
        gates = lambda k, rs=rs: gm_ref[rs, D_MODEL * k:D_MODEL * (k + 1)].astype(f32)
        halves.append(_out_core(x_ref[rs, :], zs, gates, (wa_ref, wb_ref, wc_ref), wout_ref,
                                lng_ref[layer:layer + 1, :], lnb_ref[layer:layer + 1, :], alpha))
    y = jnp.concatenate(halves, axis=0)
    if natural_out:
        tt = y.shape[0] // nb
        for k in range(N_SLABS):
            ysl[k] = y[:, LANES * k:LANES * (k + 1)]
        for b in range(nb):
            for k in range(N_SLABS):
                y_ref[b, :, LANES * k:LANES * (k + 1)] = ysl.at[k][pl.ds(b, tt, stride=nb), :]
    else:
        y_ref[...] = y


def _sample_kernel(x_ref, win_ref, tab1_ref, tab2_ref, sinks_ref, bbh_ref, cch_ref, ab_ref, wglu_ref, wlru_ref,
                   vec_ref, wa_ref, wb_ref, wc_ref, wout_ref, lng_ref, lnb_ref,
                   kc_ref, vc_ref, sre0_ref, sim0_ref, lru0_ref, conv0_ref,
                   y_ref, ko_ref, vo_ref, sre_ref, sim_ref, lruo_ref, convo_ref,
                   xcar, qs, kvn, atts, xs, hst, xcs, la, lb, lst, sc1, sc2, p1, p2, *, alpha, nb, tt):
    layer = pl.program_id(0)
    g = pl.program_id(1)
    m = nb * tt

    @pl.when(layer == 0)
    def _first_layer_input():
        xcar[g] = x_ref[...]

    x = xcar[g]
    xb = x.astype(bf16)
    hst[0] = sre0_ref[...]
    hst[1] = sim0_ref[...]
    lst[...] = lru0_ref[...]
    for t in range(3):
        xcs[t * nb:(t + 1) * nb, :] = conv0_ref[:, BRANCH * t:BRANCH * (t + 1)]

    def proj(cols):
        return jnp.dot(xb, win_ref[:, cols[0]:cols[1]], preferred_element_type=f32)

    q = proj(C_Q)
    for j in range(4):
        qs[j] = q[:, LANES * j:LANES * (j + 1)] * (HEAD_DIM ** -0.5)
    kv = proj((C_K[0], C_V[1]))
    kvn[0] = kv[:, :LANES]
    kvn[1] = kv[:, LANES:]
    _attend_sample(layer, qs, kvn, atts, kc_ref, vc_ref, tab1_ref, tab2_ref, sinks_ref, ko_ref, vo_ref,
                   sc1, sc2, p1, p2, nb=nb, tt=tt)
    zs = _mixer_tail(xb, win_ref, atts, xs, hst, xcs, la, lb, lst, bbh_ref, cch_ref, ab_ref, wglu_ref,
                     wlru_ref, vec_ref, nb=nb, tt=tt)
    sre_ref[...] = hst[0]
    sim_ref[...] = hst[1]
    lruo_ref[...] = lst[...]
    for t in range(3):
        convo_ref[:, BRANCH * t:BRANCH * (t + 1)] = xcs[m + t * nb:m + (t + 1) * nb, :]

    y = _out_core(x, zs, functools.partial(_merge_gate, xb, win_ref), (wa_ref, wb_ref, wc_ref), wout_ref,
                  lng_ref[pl.ds(layer, 1), :], lnb_ref[pl.ds(layer, 1), :], alpha)
    xcar[g] = y
    y_ref[...] = y


def _layer_spec(shape, layer=None):
    if layer is None:
        idx = lambda l, *_: (l,) + (0,) * len(shape)
    else:
        idx = lambda *_: (layer,) + (0,) * len(shape)
    return pl.BlockSpec((None,) + tuple(shape), idx, pipeline_mode=pl.Buffered(1))


def _const_spec(shape):
    return pl.BlockSpec(tuple(shape), lambda *_: (0,) * len(shape), pipeline_mode=pl.Buffered(1))


def _mixer_weight_specs(layer):
    return [
        _layer_spec((2, 256, SSM_WIDTH), layer),
        _layer_spec((2, SSM_WIDTH, 256), layer),
        _layer_spec((4, HALF), layer),
        _layer_spec((BRANCH, BRANCH), layer),
        _layer_spec((2, 2, 256, 256), layer),
        _layer_spec((16, BRANCH), layer),
    ]


def _mixer_scratch(nb, tt, kv_rows):
    m = nb * tt
    return [
        pltpu.VMEM((4, m, LANES), f32),
        pltpu.VMEM((2, kv_rows, LANES), f32),
        pltpu.VMEM((4, m, LANES), f32),
        pltpu.VMEM((m, SSM_WIDTH), f32),
        pltpu.VMEM((2, nb, SSM_WIDTH), f32),
        pltpu.VMEM((m + 3 * nb, BRANCH), f32),
        pltpu.VMEM((m, BRANCH), f32),
        pltpu.VMEM((m, BRANCH), f32),
        pltpu.VMEM((nb, BRANCH), f32),
    ]


def _prompt_mixer_call(x, wp, tab, layer, *, nb, tt, natural_in):
    nts = (x.shape[1] if natural_in else x.shape[0] // nb) // tt
    m, w, s_len = nb * tt, WINDOW * nb, WINDOW + tt
    kern = functools.partial(_prompt_mixer_kernel, layer=layer, nb=nb, tt=tt, nts=nts, natural_in=natural_in)
    x_spec = (pl.BlockSpec((nb, tt, D_MODEL), lambda t: (0, t, 0)) if natural_in
              else pl.BlockSpec((m, D_MODEL), lambda t: (t, 0)))
    in_specs = [x_spec,
                _layer_spec((D_MODEL, IN_COLS), layer),
                _const_spec(tab.shape), pl.BlockSpec(memory_space=pltpu.SMEM), *_mixer_weight_specs(layer)]
    whole = lambda shape: pl.BlockSpec(shape, lambda t: (0,) * len(shape))
    out_specs = [pl.BlockSpec((m, 3 * BRANCH), lambda t: (t, 0)), pl.BlockSpec((m, 3 * D_MODEL), lambda t: (t, 0))]
    out_shape = [jax.ShapeDtypeStruct((nts * m, 3 * BRANCH), bf16),
                 jax.ShapeDtypeStruct((nts * m, 3 * D_MODEL), bf16)]
    if natural_in:
        out_specs.append(pl.BlockSpec((m, D_MODEL), lambda t: (t, 0)))
        out_shape.append(jax.ShapeDtypeStruct((nts * m, D_MODEL), f32))
    state_shapes = [(nb, WINDOW, LANES), (nb, WINDOW, LANES), (nb, SSM_WIDTH), (nb, SSM_WIDTH),
                    (nb, BRANCH), (nb, 3 * BRANCH)]
    out_specs += [whole(s) for s in state_shapes]
    out_shape += [jax.ShapeDtypeStruct(s, f32) for s in state_shapes]
    scratch = _mixer_scratch(nb, tt, w + m) + [
        pltpu.VMEM((2, 4 * tt, s_len), f32),
        pltpu.VMEM((nb, 2, 4 * tt, s_len), f32),
        pltpu.VMEM((nb, 2, 4 * tt, s_len), bf16),
        pltpu.VMEM((2, m, BRANCH), f32),
    ]
    if natural_in:
        scratch.append(pltpu.VMEM((N_SLABS, m, LANES), f32))
    return pl.pallas_call(
        kern, grid=(nts,), in_specs=in_specs, out_specs=out_specs, out_shape=out_shape,
        scratch_shapes=scratch, name=f"mixer_prompt_{layer}",
        compiler_params=pltpu.CompilerParams(dimension_semantics=("arbitrary",), vmem_limit_bytes=VMEM_LIMIT),
    )(x, wp["win"], tab, wp["sinks"], wp["bbh"], wp["cch"], wp["ab"], wp["wglu"], wp["wlru"], wp["vec"])


def _prompt_out_call(x2d, z2d, gm2d, wp, layer, *, nb, mb, alpha, natural_out):
    rows = x2d.shape[0]
    kern = functools.partial(_prompt_out_kernel, layer=layer, alpha=alpha, nb=nb, natural_out=natural_out)
    if natural_out:
        out_spec = pl.BlockSpec((nb, mb // nb, D_MODEL), lambda i: (0, i, 0))
        out_shape = jax.ShapeDtypeStruct((nb, rows // nb, D_MODEL), f32)
        scratch = [pltpu.VMEM((N_SLABS, mb, LANES), f32)]
    else:
        out_spec = pl.BlockSpec((mb, D_MODEL), lambda i: (i, 0))
        out_shape = jax.ShapeDtypeStruct((rows, D_MODEL), f32)
        scratch = []
    return pl.pallas_call(
        kern, grid=(rows // mb,),
        in_specs=[
            pl.BlockSpec((mb, D_MODEL), lambda i: (i, 0)),
            pl.BlockSpec((mb, 3 * BRANCH), lambda i: (i, 0)),
            pl.BlockSpec((mb, 3 * D_MODEL), lambda i: (i, 0)),
            _layer_spec((BRANCH, D_MODEL), layer), _layer_spec((BRANCH, D_MODEL), layer),
            _layer_spec((BRANCH, D_MODEL), layer),
            _layer_spec((D_MODEL, D_MODEL), layer),
            _const_spec(wp["ln_g"].shape), _const_spec(wp["ln_b"].shape),
        ],
        out_specs=out_spec, out_shape=out_shape, scratch_shapes=scratch, name=f"out_prompt_{layer}",
        compiler_params=pltpu.CompilerParams(dimension_semantics=("arbitrary",), vmem_limit_bytes=VMEM_LIMIT),
    )(x2d, z2d, gm2d, wp["wbr_a"], wp["wbr_b"], wp["wbr_c"], wp["wout"], wp["ln_g"], wp["ln_b"])


def _sample_call(x2d, wp, tabs, state, *, depth, nbg, nb, tt, alpha):
    m = nb * tt
    kern = functools.partial(_sample_kernel, alpha=alpha, nb=nb, tt=tt)
    row_spec = pl.BlockSpec((m, D_MODEL), lambda l, g: (g, 0))
    st = lambda *shape: pl.BlockSpec((nb,) + shape, lambda l, g: (l * nbg + g,) + (0,) * len(shape))
    state_specs = [st(WINDOW, LANES), st(WINDOW, LANES), st(SSM_WIDTH), st(SSM_WIDTH), st(BRANCH), st(3 * BRANCH)]
    state_shapes = [(WINDOW, LANES), (WINDOW, LANES), (SSM_WIDTH,), (SSM_WIDTH,), (BRANCH,), (3 * BRANCH,)]
    in_specs = [row_spec, _layer_spec((D_MODEL, IN_COLS)), _const_spec(tabs[0].shape), _const_spec(tabs[1].shape),
                pl.BlockSpec(memory_space=pltpu.SMEM), *_mixer_weight_specs(None),
                _layer_spec((BRANCH, D_MODEL)), _layer_spec((BRANCH, D_MODEL)), _layer_spec((BRANCH, D_MODEL)),
                _layer_spec((D_MODEL, D_MODEL)), _const_spec(wp["ln_g"].shape), _const_spec(wp["ln_b"].shape),
                *state_specs]
    out_specs = [pl.BlockSpec((m, D_MODEL), lambda l, g: (l * nbg + g, 0)), *state_specs]
    out_shape = [jax.ShapeDtypeStruct((depth * nbg * m, D_MODEL), f32)] + [
        jax.ShapeDtypeStruct((depth * nbg * nb,) + s, f32) for s in state_shapes]
    nsg, hr = nb // SUBLANES, 4 * tt * SUBLANES
    scratch = [pltpu.VMEM((nbg, m, D_MODEL), f32)] + _mixer_scratch(nb, tt, m) + [
        pltpu.VMEM((nsg, 2, hr, SUBLANES * WINDOW), f32),
        pltpu.VMEM((nsg, 2, hr, SUBLANES * tt), f32),
        pltpu.VMEM((nsg, 2, hr, SUBLANES * WINDOW), bf16),
        pltpu.VMEM((nsg, 2, hr, SUBLANES * tt), bf16),
    ]
    return pl.pallas_call(
        kern, grid=(depth, nbg), in_specs=in_specs, out_specs=out_specs, out_shape=out_shape,
        scratch_shapes=scratch, name="sample_layers",
        compiler_params=pltpu.CompilerParams(
            dimension_semantics=("arbitrary", "arbitrary"), vmem_limit_bytes=VMEM_LIMIT),
    )(x2d, wp["win"], tabs[0], tabs[1], wp["sinks"], wp["bbh"], wp["cch"], wp["ab"], wp["wglu"], wp["wlru"],
      wp["vec"], wp["wbr_a"], wp["wbr_b"], wp["wbr_c"], wp["wout"], wp["ln_g"], wp["ln_b"], *state)


def _t5_bucket(dist):
    max_exact = N_BUCKETS // 2
    d = jnp.maximum(dist, 0)
    large = max_exact + (jnp.log(jnp.maximum(d, 1).astype(f32) / max_exact)
                         / math.log(MAX_DISTANCE / max_exact) * (N_BUCKETS - max_exact)).astype(jnp.int32)
    large = jnp.minimum(large, N_BUCKETS - 1)
    return jnp.where(d < max_exact, d, large)


def _band_bias(rel_bias, tt):
    s_len = WINDOW + tt
    by_dist = rel_bias.astype(f32)[_t5_bucket(jnp.arange(WINDOW))].T
    row0 = jnp.concatenate([jnp.full((N_Q_HEADS, 1), NEG, f32), by_dist[:, ::-1],
                            jnp.full((N_Q_HEADS, tt), NEG, f32)], axis=1)
    flat = jnp.tile(row0, (1, tt))[:, :tt * s_len]
    return flat.reshape(N_Q_HEADS, tt, s_len)


def _prompt_table(rel_bias, tt):
    return _band_bias(rel_bias, tt).reshape(2, 4 * tt, WINDOW + tt)


def _sample_tables(rel_bias, tt):
    bs = _band_bias(rel_bias, tt).reshape(2, 4, tt, WINDOW + tt)
    same = jnp.eye(SUBLANES, dtype=bool)
    t1 = jnp.where(same[None, None, None, :, :, None], bs[:, :, :, None, None, :WINDOW], NEG)
    t2 = jnp.where(same[None, None, None, :, None, :], bs[:, :, :, None, WINDOW:, None], NEG)
    rows = 4 * tt * SUBLANES
    return t1.reshape(2, rows, SUBLANES * WINDOW), t2.reshape(2, rows, tt * SUBLANES)


def _weights(w_in, sinks, w_branch_a, lam_re, lam_im, log_step, b_re, b_im, c_re, c_im, ssm_d, w_glu,
             w_branch_b, conv_w, conv_b, w_a, b_a, w_x, b_x, lam, w_branch_c, w_out, ln_g, ln_b):
    depth = w_in.shape[0]
    hg = SSM_GROUPS // 2
    lr = jnp.minimum(lam_re.astype(f32), -1e-4)
    li = lam_im.astype(f32)
    step = jnp.exp(log_step.astype(f32))[..., None]
    mag = jnp.exp(lr * step)
    ab_re = mag * jnp.cos(li * step)
    ab_im = mag * jnp.sin(li * step)
    den = lr * lr + li * li
    f_re = ((ab_re - 1.0) * lr + ab_im * li) / den
    f_im = (ab_im * lr - (ab_re - 1.0) * li) / den
    bre, bim = b_re.astype(f32), b_im.astype(f32)
    bb_re = f_re[..., None] * bre - f_im[..., None] * bim
    bb_im = f_re[..., None] * bim + f_im[..., None] * bre

    def diag_tiles(blocks, n, axis):
        c = blocks.shape[-1]
        per = LANES // c
        tiled = jnp.tile(blocks, (1,) * (blocks.ndim - 1) + (per,))
        blk_id = jnp.arange(n).reshape((n,) + (1,) * (blocks.ndim - 1 - axis))
        lane_blk = jnp.arange(LANES) // c
        return jnp.concatenate([jnp.where(blk_id == per * j + lane_blk, tiled, 0.0) for j in range(n // per)], axis=-1)

    drv = jnp.stack([bb_re, bb_im], axis=1).reshape(depth, 2, 2, hg, SSM_STATE, SSM_GROUP)
    bbh = diag_tiles(drv.transpose(0, 2, 3, 5, 1, 4), hg, 2)
    bbh = bbh.astype(bf16).reshape(depth, 2, hg * SSM_GROUP, SSM_WIDTH)
    rdo = jnp.stack([c_re.astype(f32), -c_im.astype(f32)], axis=1).reshape(depth, 2, 2, hg, SSM_GROUP, SSM_STATE)
    cch = diag_tiles(rdo.transpose(0, 2, 1, 3, 5, 4), hg, 3)
    cch = cch.astype(bf16).reshape(depth, 2, SSM_WIDTH, hg * SSM_GROUP)
    ab = jnp.stack([ab_re.reshape(depth, 2, HALF), ab_im.reshape(depth, 2, HALF)], axis=2).reshape(depth, 4, HALF)
    nblk = LRU_BLOCKS // 2
    blk = jnp.stack([w_a.astype(f32), w_x.astype(f32)], axis=1).reshape(depth, 2, 2, nblk, 64, 64)
    wlru = diag_tiles(blk, nblk, 3).reshape(depth, 2, 2, 256, 256)
    row = lambda v: v.astype(f32)[:, None, :]
    vec = jnp.concatenate([conv_w.astype(f32), row(conv_b), row(b_a), row(b_x), row(lam), row(ssm_d),
                           jnp.zeros((depth, 16 - V_D - 1, BRANCH), f32)], axis=1)
    return {
        "win": w_in.astype(bf16),
        "sinks": sinks.astype(f32),
        "bbh": bbh, "cch": cch, "ab": ab,
        "wglu": w_glu.astype(bf16),
        "wlru": wlru.astype(bf16),
        "vec": vec,
        "wbr_a": w_branch_a.astype(bf16), "wbr_b": w_branch_b.astype(bf16), "wbr_c": w_branch_c.astype(bf16),
        "wout": w_out.astype(bf16),
        "ln_g": ln_g.astype(f32), "ln_b": ln_b.astype(f32),
    }


def _group_rows(x, nbg, nb):
    b, t, c = x.shape
    return x.reshape(nbg, nb, t, c).transpose(0, 2, 1, 3).reshape(nbg * t * nb, c)


def _ungroup_rows(x2d, nbg, nb, t):
    c = x2d.shape[-1]
    return x2d.reshape(nbg, t, nb, c).transpose(0, 2, 1, 3).reshape(nbg * nb, t, c)


def _states_out(lead, ko, vo, sre, sim, lru, conv):
    return (ko.reshape(*lead, WINDOW, 2, HEAD_DIM), vo.reshape(*lead, WINDOW, 2, HEAD_DIM),
            sre.reshape(*lead, SSM_GROUPS, SSM_STATE), sim.reshape(*lead, SSM_GROUPS, SSM_STATE),
            lru.reshape(*lead, BRANCH), conv.reshape(*lead, CONV_WIDTH - 1, BRANCH))


P_TT = 64
OUT_ROWS = 1024
S_NBG = 4


def kernel(x_prompt, x_sample, cache_k, cache_v, state_ssm_re, state_ssm_im, state_lru, state_conv, rel_bias, w_in, sinks, w_branch_a, ssm_lambda_re, ssm_lambda_im, ssm_log_step, ssm_b_re, ssm_b_im, ssm_c_re, ssm_c_im, ssm_d, ssm_w_glu, w_branch_b, conv_w, conv_b, lru_w_a, lru_b_a, lru_w_x, lru_b_x, lru_lambda, w_branch_c, w_out, ln_g, ln_b):
    depth = w_in.shape[0]
    alpha = (2 * depth) ** 0.25
    pb, pt, _ = x_prompt.shape
    sb, st, _ = x_sample.shape
    s_nb = sb // S_NBG

    wp = _weights(w_in, sinks, w_branch_a, ssm_lambda_re, ssm_lambda_im, ssm_log_step, ssm_b_re, ssm_b_im,
                  ssm_c_re, ssm_c_im, ssm_d, ssm_w_glu, w_branch_b, conv_w, conv_b, lru_w_a, lru_b_a, lru_w_x,
                  lru_b_x, lru_lambda, w_branch_c, w_out, ln_g, ln_b)
    p_tab = _prompt_table(rel_bias, P_TT)
    s_tabs = _sample_tables(rel_bias, st)

    st_p = []
    x = x_prompt
    for l in range(depth):
        outs = _prompt_mixer_call(x, wp, p_tab, l, nb=pb, tt=P_TT, natural_in=(l == 0))
        if l == 0:
            z, gm, x, *sp = outs
        else:
            z, gm, *sp = outs
        x = _prompt_out_call(x, z, gm, wp, l, nb=pb, mb=OUT_ROWS, alpha=alpha, natural_out=(l == depth - 1))
        st_p.append(_states_out((pb,), *sp))
    y_prompt = x
    stacked_p = [jnp.stack([s[k] for s in st_p]) for k in range(6)]

    n = depth * sb
    s_state = (cache_k.astype(f32).reshape(n, WINDOW, LANES), cache_v.astype(f32).reshape(n, WINDOW, LANES),
               state_ssm_re.astype(f32).reshape(n, SSM_WIDTH), state_ssm_im.astype(f32).reshape(n, SSM_WIDTH),
               state_lru.astype(f32).reshape(n, BRANCH), state_conv.astype(f32).reshape(n, 3 * BRANCH))
    ys, *ss = _sample_call(_group_rows(x_sample, S_NBG, s_nb), wp, s_tabs, s_state, depth=depth, nbg=S_NBG,
                           nb=s_nb, tt=st, alpha=alpha)
    y_sample = _ungroup_rows(ys[(depth - 1) * sb * st:], S_NBG, s_nb, st)
    stacked_s = _states_out((depth, sb), *ss)
    return (y_prompt, y_sample, *stacked_p, *stacked_s)
```
